```python
import math, functools
import jax, jax.numpy as jnp
from jax import lax
import numpy as np

D_MODEL = 1024
BATCH = 1
SEQ = 16384
DEPTH = 2
DEC_BATCH = 32
DEC_SEQ = 4
PAST_LEN = 16384
PAGE_SIZE = 128

D_MIX = D_MODEL
C_A = D_MIX // 4
CONV_W = 31
H_B = 4
DK_B = 64
DV_B = 2 * DK_B
D_B = H_B * DV_B
H_C = 4
DK_C = 64
DV_C = 64
D_C = H_C * DV_C
N_IN = 3 * C_A + 2 * (2 * H_B * DK_B) + H_B * DV_B + D_B + 2 * H_C * DK_C + 2 * H_C * DV_C + D_C + 2 * H_C
QBLOCK = 128
MLSTM_CHUNK = 64

kernel_name = 'hymba_conv_diffattn_mlstm_step'


def _split_points():
    sizes = (C_A, C_A, C_A,
             2 * H_B * DK_B, 2 * H_B * DK_B, H_B * DV_B, D_B,
             H_C * DK_C, H_C * DK_C, H_C * DV_C, H_C * DV_C, D_C, H_C, H_C)
    return [int(s) for s in np.cumsum(sizes)[:-1]]


def _rmsnorm(x, g, eps=1e-6):
    xf = x.astype(jnp.float32)
    y = xf * lax.rsqrt(jnp.mean(xf * xf, axis=-1, keepdims=True) + eps)
    return (y * g.astype(jnp.float32)).astype(x.dtype)


def _layernorm(x, g, b, eps=1e-5):
    xf = x.astype(jnp.float32)
    mu = jnp.mean(xf, axis=-1, keepdims=True)
    xc = xf - mu
    y = xc * lax.rsqrt(jnp.mean(xc * xc, axis=-1, keepdims=True) + eps)
    return (y * g.astype(jnp.float32) + b.astype(jnp.float32)).astype(x.dtype)


def _conv_branch(a_val, a_gate, buf, conv_w, conv_b, ln_g, ln_b, pw):
    u = a_val * jax.nn.sigmoid(a_gate)
    full = jnp.concatenate([buf.astype(u.dtype), u], axis=1)
    y = lax.conv_general_dilated(full, conv_w[:, None, :].astype(u.dtype), window_strides=(1,),
                                 padding='VALID', dimension_numbers=('NWC', 'WIO', 'NWC'),
                                 feature_group_count=C_A) + conv_b
    y = jax.nn.silu(_layernorm(y, ln_g, ln_b))
    y = jnp.einsum('btc,ce->bte', y, pw)
    return y, full[:, full.shape[1] - (CONV_W - 1):]


def _diff_attn_prompt(q, k, v, lam):
    bsz, t_len = q.shape[0], q.shape[1]
    n_blocks = t_len // QBLOCK
    scale = DK_B ** -0.5
    kpos = jnp.arange(t_len)

    def block(i):
        qb = lax.dynamic_slice_in_dim(q, i * QBLOCK, QBLOCK, axis=1)
        s = jnp.einsum('bqhcd,bkhcd->bhcqk', qb, k, preferred_element_type=jnp.float32) * scale
        qpos = i * QBLOCK + jnp.arange(QBLOCK)
        mask = kpos[None, :] <= qpos[:, None]
        p = jax.nn.softmax(jnp.where(mask, s, -jnp.inf), axis=-1)
        a = p[:, :, 0] - lam * p[:, :, 1]
        return jnp.einsum('bhqk,bkhe->bqhe', a.astype(v.dtype), v)

    o = lax.map(block, jnp.arange(n_blocks))
    return jnp.moveaxis(o, 0, 1).reshape(bsz, t_len, H_B, DV_B)


def _diff_attn_sample(q, k, v, lam, k_past, v_past):
    t_len = q.shape[1]
    p_len = k_past.shape[1]
    scale = DK_B ** -0.5
    s_past = jnp.einsum('bqhcd,bkhcd->bhcqk', q, k_past, preferred_element_type=jnp.float32) * scale
    s_new = jnp.einsum('bqhcd,bkhcd->bhcqk', q, k, preferred_element_type=jnp.float32) * scale
    causal = jnp.tril(jnp.ones((t_len, t_len), dtype=bool))
    s_new = jnp.where(causal, s_new, -jnp.inf)
    p = jax.nn.softmax(jnp.concatenate([s_past, s_new], axis=-1), axis=-1)
    a = (p[:, :, 0] - lam * p[:, :, 1]).astype(v.dtype)
    return (jnp.einsum('bhqk,bkhe->bqhe', a[..., :p_len], v_past.astype(v.dtype))
            + jnp.einsum('bhqk,bkhe->bqhe', a[..., p_len:], v))


def _mlstm_chunk(carry, xs):
    c_mat, n_vec, m_prev = carry
    q, k, v, ig, lf = xs
    l_len = q.shape[1]
    b = jnp.cumsum(lf, axis=1)
    log_d = b[:, :, None, :] - b[:, None, :, :] + ig[:, None, :, :]
    causal = (jnp.arange(l_len)[:, None] >= jnp.arange(l_len)[None, :])[None, :, :, None]
    log_d = jnp.where(causal, log_d, -jnp.inf)
    inter = b + m_prev[:, None, :]
    m_t = jnp.maximum(inter, jnp.max(log_d, axis=2))
    d_w = jnp.exp(log_d - m_t[:, :, None, :])
    i_w = jnp.exp(inter - m_t)
    w = d_w * jnp.einsum('bthd,bshd->btsh', q, k)
    num = jnp.einsum('btsh,bshe->bthe', w, v) + i_w[..., None] * jnp.einsum('bthd,bhde->bthe', q, c_mat)
    den = jnp.sum(w, axis=2) + i_w * jnp.einsum('bthd,bhd->bth', q, n_vec)
    h = num / jnp.maximum(jnp.abs(den), jnp.exp(-m_t))[..., None]
    w_last = d_w[:, -1]
    i_last = i_w[:, -1]
    c_new = i_last[:, :, None, None] * c_mat + jnp.einsum('bsh,bshd,bshe->bhde', w_last, k, v)
    n_new = i_last[:, :, None] * n_vec + jnp.einsum('bsh,bshd->bhd', w_last, k)
    return (c_new, n_new, m_t[:, -1]), h


def _mlstm(state, q, k, v, ig, lf):
    bsz, t_len = q.shape[0], q.shape[1]
    l_len = MLSTM_CHUNK if t_len % MLSTM_CHUNK == 0 else t_len
    n_chunks = t_len // l_len

    def chunks(a):
        a = a.astype(jnp.float32)
        return jnp.moveaxis(a.reshape((bsz, n_chunks, l_len) + a.shape[2:]), 1, 0)

    state, h = lax.scan(_mlstm_chunk, state, (chunks(q), chunks(k), chunks(v), chunks(ig), chunks(lf)))
    return state, jnp.moveaxis(h, 0, 1).reshape(bsz, t_len, H_C, DV_C)


def _layer(x, lidx, conv_buf, mstate, attn_fn, ln_pre, w_in, b_in, conv_w, conv_b, conv_ln_g, conv_ln_b,
           conv_pw, lam_qk, attn_g, f_bias, mlstm_g, w_out, ln_post):
    bsz, t_len = x.shape[0], x.shape[1]
    h = _rmsnorm(x, ln_pre)
    u = jnp.einsum('btd,dn->btn', h, w_in) + b_in
    (a_val, a_gate, a_z, b_q, b_k, b_v, b_z,
     c_q, c_k, c_v, c_o, c_z, c_i, c_f) = jnp.split(u, _split_points(), axis=-1)

    y_a, conv_buf_new = _conv_branch(a_val, a_gate, conv_buf, conv_w, conv_b, conv_ln_g, conv_ln_b, conv_pw)

    lam_init = 0.8 - 0.6 * math.exp(-0.3 * lidx)
    lq = lam_qk.astype(jnp.float32)
    lam = jnp.exp(jnp.sum(lq[0] * lq[1])) - jnp.exp(jnp.sum(lq[2] * lq[3])) + lam_init
    q = b_q.reshape(bsz, t_len, H_B, 2, DK_B)
    k = b_k.reshape(bsz, t_len, H_B, 2, DK_B)
    v = b_v.reshape(bsz, t_len, H_B, DV_B)
    o = attn_fn(q, k, v, lam)
    y_b = (_rmsnorm(o, attn_g) * (1.0 - lam_init)).reshape(bsz, t_len, D_B)

    qc = c_q.reshape(bsz, t_len, H_C, DK_C)
    kc = c_k.reshape(bsz, t_len, H_C, DK_C) * (DK_C ** -0.5)
    vc = c_v.reshape(bsz, t_len, H_C, DV_C)
    ig = c_i.astype(jnp.float32)
    lf = jax.nn.log_sigmoid(c_f.astype(jnp.float32) + f_bias.astype(jnp.float32))
    mstate_new, h_c = _mlstm(mstate, qc, kc, vc, ig, lf)
    h_c = jax.nn.sigmoid(c_o).reshape(bsz, t_len, H_C, DV_C) * h_c.astype(x.dtype)
    y_c = _rmsnorm(h_c, mlstm_g).reshape(bsz, t_len, D_C)

    mix = jnp.concatenate([y_a * jax.nn.silu(a_z), y_b * jax.nn.silu(b_z), y_c * jax.nn.silu(c_z)], axis=-1)
    out = jnp.einsum('btm,md->btd', mix, w_out)
    x = x + _rmsnorm(out, ln_post)
    return x, b_k.reshape(bsz, t_len, H_B, 2 * DK_B), v, conv_buf_new, mstate_new


def setup_inputs(seed: int = 0) -> dict:
    key = jax.random.key(seed)
    ks = jax.random.split(key, 24)
    nrm = jax.random.normal
    n_pages = PAST_LEN // PAGE_SIZE
    n_used = DEC_BATCH * n_pages
    n_pool = n_used + max(1, n_used // 4)
    page_table = jax.random.permutation(ks[8], n_pool)[:n_used].reshape(DEC_BATCH, n_pages).astype(jnp.int32)
    return {
        'x_prompt': nrm(ks[0], (BATCH, SEQ, D_MODEL), jnp.float32),
        'x_sample': nrm(ks[1], (DEC_BATCH, DEC_SEQ, D_MODEL), jnp.float32),
        'cache_k': nrm(ks[2], (DEPTH, n_pool, PAGE_SIZE, H_B, 2 * DK_B), jnp.float32),
        'cache_v': nrm(ks[3], (DEPTH, n_pool, PAGE_SIZE, H_B, DV_B), jnp.float32),
        'state_conv': 0.5 * nrm(ks[4], (DEPTH, DEC_BATCH, CONV_W - 1, C_A), jnp.float32),
        'state_C': 0.3 * nrm(ks[5], (DEPTH, DEC_BATCH, H_C, DK_C, DV_C), jnp.float32),
        'state_n': 0.3 * nrm(ks[6], (DEPTH, DEC_BATCH, H_C, DK_C), jnp.float32),
        'state_m': 0.5 * nrm(ks[7], (DEPTH, DEC_BATCH, H_C), jnp.float32),
        'page_table': page_table,
        'ln_pre': 1.0 + 0.02 * nrm(ks[9], (DEPTH, D_MODEL), jnp.float32),
        'w_in': nrm(ks[10], (DEPTH, D_MODEL, N_IN), jnp.float32) * D_MODEL ** -0.5,
        'b_in': 0.02 * nrm(ks[11], (DEPTH, N_IN), jnp.float32),
        'conv_w': nrm(ks[12], (DEPTH, CONV_W, C_A), jnp.float32) * CONV_W ** -0.5,
        'conv_b': 0.02 * nrm(ks[13], (DEPTH, C_A), jnp.float32),
        'conv_ln_g': 1.0 + 0.02 * nrm(ks[14], (DEPTH, C_A), jnp.float32),
        'conv_ln_b': 0.02 * nrm(ks[15], (DEPTH, C_A), jnp.float32),
        'conv_pw': nrm(ks[16], (DEPTH, C_A, C_A), jnp.float32) * C_A ** -0.5,
        'lam_qk': 0.1 * nrm(ks[17], (DEPTH, 4, DK_B), jnp.float32),
        'attn_g': 1.0 + 0.02 * nrm(ks[18], (DEPTH, DV_B), jnp.float32),
        'f_bias': jnp.linspace(3.0, 6.0, H_C, dtype=jnp.float32)[None, :] + 0.1 * nrm(ks[19], (DEPTH, H_C), jnp.float32),
        'mlstm_g': 1.0 + 0.02 * nrm(ks[20], (DEPTH, H_C, DV_C), jnp.float32),
        'w_out': nrm(ks[21], (DEPTH, D_MIX, D_MODEL), jnp.float32) * D_MIX ** -0.5,
        'ln_post': 1.0 + 0.02 * nrm(ks[22], (DEPTH, D_MODEL), jnp.float32),
    }


def reference(x_prompt, x_sample, cache_k, cache_v, state_conv, state_C, state_n, state_m, page_table,
              ln_pre, w_in, b_in, conv_w, conv_b, conv_ln_g, conv_ln_b, conv_pw, lam_qk, attn_g, f_bias,
              mlstm_g, w_out, ln_post):
    f32 = jnp.float32
    bp = x_prompt.shape[0]
    bs = x_sample.shape[0]
    past = page_table.shape[1] * cache_k.shape[2]

    def weights(l):
        return (ln_pre[l], w_in[l], b_in[l], conv_w[l], conv_b[l], conv_ln_g[l], conv_ln_b[l], conv_pw[l],
                lam_qk[l], attn_g[l], f_bias[l], mlstm_g[l], w_out[l], ln_post[l])

    xp = x_prompt
    pk, pv, pconv, pc, pn, pm = [], [], [], [], [], []
    for l in range(DEPTH):
        buf0 = jnp.zeros((bp, CONV_W - 1, C_A), xp.dtype)
        m0 = (jnp.zeros((bp, H_C, DK_C, DV_C), f32), jnp.zeros((bp, H_C, DK_C), f32), jnp.zeros((bp, H_C), f32))
        xp, kr, vr, cb, (cm, nv, mv) = _layer(xp, l, buf0, m0, _diff_attn_prompt, *weights(l))
        pk.append(kr); pv.append(vr); pconv.append(cb)
        pc.append(cm.astype(state_C.dtype)); pn.append(nv.astype(state_n.dtype)); pm.append(mv.astype(state_m.dtype))

    xs = x_sample
    sk, sv, sconv, sc, sn, sm = [], [], [], [], [], []
    for l in range(DEPTH):
        k_past = cache_k[l, page_table].reshape(bs, past, H_B, 2, DK_B)
        v_past = cache_v[l, page_table].reshape(bs, past, H_B, DV_B)
        attn_fn = functools.partial(_diff_attn_sample, k_past=k_past, v_past=v_past)
        st = (state_C[l].astype(f32), state_n[l].astype(f32), state_m[l].astype(f32))
        xs, kr, vr, cb, (cm, nv, mv) = _layer(xs, l, state_conv[l], st, attn_fn, *weights(l))
        sk.append(kr); sv.append(vr); sconv.append(cb)
        sc.append(cm.astype(state_C.dtype)); sn.append(nv.astype(state_n.dtype)); sm.append(mv.astype(state_m.dtype))

    return (xp, xs,
            jnp.stack(pk), jnp.stack(pv), jnp.stack(pconv), jnp.stack(pc), jnp.stack(pn), jnp.stack(pm),
            jnp.stack(sk), jnp.stack(sv), jnp.stack(sconv), jnp.stack(sc), jnp.stack(sn), jnp.stack(sm))
```

```python
import functools
import math

import jax
import jax.numpy as jnp
from jax import lax
from jax.experimental import pallas as pl
from jax.experimental.pallas import tpu as pltpu

F32 = jnp.float32
BF16 = jnp.bfloat16

C_A = 256
CONV_W = 31
H_B = 4
DK_B = 64
DV_B = 128
D_B = H_B * DV_B
H_C = 4
DK_C = 64
DV_C = 64
D_C = H_C * DV_C

OFF_A = 0
OFF_Q = 3 * C_A
OFF_K = OFF_Q + 2 * H_B * DK_B
OFF_V = OFF_K + 2 * H_B * DK_B
OFF_Z = OFF_V + D_B
OFF_C = OFF_Z + D_B
OFF_G = OFF_C + 5 * D_C
N_MAIN = OFF_G
N_GATE = 2 * H_C

HIST = 32
NEG = -1e30
QK_SCALE = DK_B ** -0.5
KC_SCALE = DK_C ** -0.5

VMEM_LIMIT_BYTES = 48 * 1024 * 1024


def _cparams(n_axes):
    return pltpu.CompilerParams(dimension_semantics=("arbitrary",) * n_axes,
                                vmem_limit_bytes=VMEM_LIMIT_BYTES)


def _pick(n, candidates):
    for c in candidates:
        if n % c == 0:
            return c
    return n


def _sigmoid(x):
    return 1.0 / (1.0 + jnp.exp(-x))


def _silu(x):
    return x * _sigmoid(x)


def _log_sigmoid(x):
    return jnp.minimum(x, 0.0) - jnp.log(1.0 + jnp.exp(-jnp.abs(x)))


def _inproj_body(x_ref, g_ref, w_ref, b_ref, wg_ref, wgt_ref, bgc_ref, bgr_ref,
                 a_ref, q_ref, k_ref, kb_ref, v_ref, vb_ref, z_ref, c_ref, gc_ref, gr_ref):
    xf = x_ref[...]
    h = xf * lax.rsqrt(jnp.mean(xf * xf, axis=-1, keepdims=True) + 1e-6) * g_ref[...]
    hb = h.astype(BF16)

    def seg(lo, hi):
        return jnp.dot(hb, w_ref[:, lo:hi], preferred_element_type=F32) + b_ref[:, lo:hi]

    a_ref[...] = seg(OFF_A, OFF_Q)
    q_ref[...] = (seg(OFF_Q, OFF_K) * QK_SCALE).astype(BF16)
    kk = seg(OFF_K, OFF_V)
    k_ref[...] = kk
    kb_ref[...] = kk.astype(BF16)
    vv = seg(OFF_V, OFF_Z)
    v_ref[...] = vv
    vb_ref[...] = vv.astype(BF16)
    z_ref[...] = seg(OFF_Z, OFF_C)
    c_ref[...] = seg(OFF_C, OFF_G)
    gc_ref[...] = jnp.dot(hb, wg_ref[...], preferred_element_type=F32) + bgc_ref[...]
    gr_ref[...] = lax.dot_general(wgt_ref[...], hb, (((1,), (1,)), ((), ())),
                                  preferred_element_type=F32) + bgr_ref[...]


def _inproj(x2d, ln_pre, w_in, b_in):
    rows, d = x2d.shape
    tm = _pick(rows, (256, 128))
    w_main = w_in[:, :N_MAIN].astype(BF16)
    w_gate = w_in[:, N_MAIN:].astype(BF16)
    b_main = b_in[:N_MAIN].reshape(1, N_MAIN)
    b_gate = b_in[N_MAIN:]
    row = lambda i: (i, 0)
    fixed = lambda i: (0, 0)
    widths = (OFF_Q - OFF_A, OFF_K - OFF_Q, OFF_V - OFF_K, OFF_V - OFF_K, OFF_Z - OFF_V,
              OFF_Z - OFF_V, OFF_C - OFF_Z, OFF_G - OFF_C)
    dtypes = (F32, BF16, F32, BF16, F32, BF16, F32, F32)
    out_shape = [jax.ShapeDtypeStruct((rows, w), dt) for w, dt in zip(widths, dtypes)]
    out_specs = [pl.BlockSpec((tm, w), row) for w in widths]
    out_shape += [jax.ShapeDtypeStruct((rows, N_GATE), F32), jax.ShapeDtypeStruct((N_GATE, rows), F32)]
    out_specs += [pl.BlockSpec((tm, N_GATE), row), pl.BlockSpec((N_GATE, tm), lambda i: (0, i))]
    return pl.pallas_call(
        _inproj_body,
        grid=(rows // tm,),
        in_specs=[pl.BlockSpec((tm, d), row),
                  pl.BlockSpec((1, d), fixed),
                  pl.BlockSpec((d, N_MAIN), fixed),
                  pl.BlockSpec((1, N_MAIN), fixed),
                  pl.BlockSpec((d, N_GATE), fixed),
                  pl.BlockSpec((N_GATE, d), fixed),
                  pl.BlockSpec((1, N_GATE), fixed),
                  pl.BlockSpec((N_GATE, 1), fixed)],
        out_specs=out_specs,
        out_shape=out_shape,
        compiler_params=_cparams(1),
        name="inproj",
    )(x2d, ln_pre.reshape(1, d), w_main, b_main, w_gate, w_gate.T,
      b_gate.reshape(1, N_GATE), b_gate.reshape(N_GATE, 1))


def _conv_body(a_ref, st_ref, cw_ref, cb_ref, lg_ref, lb_ref, y_ref, hist_ref, fbuf, *, tt, rc):
    t = pl.program_id(1)

    @pl.when(t == 0)
    def _():
        fbuf[0:HIST, :] = st_ref[0]

    a = a_ref[0]
    fbuf[HIST:HIST + tt, :] = a[:, 0:C_A] * _sigmoid(a[:, C_A:2 * C_A])
    first = HIST - (CONV_W - 1)
    for r0 in range(0, tt, rc):
        acc = jnp.zeros((rc, C_A), F32)
        for j in range(CONV_W):
            acc = acc + cw_ref[j:j + 1, :] * fbuf[pl.ds(first + j + r0, rc), :]
        y = acc + cb_ref[...]
        mu = jnp.mean(y, axis=-1, keepdims=True)
        yc = y - mu
        yn = yc * lax.rsqrt(jnp.mean(yc * yc, axis=-1, keepdims=True) + 1e-5)
        y_ref[0, r0:r0 + rc, :] = _silu(yn * lg_ref[...] + lb_ref[...])
    new_hist = fbuf[pl.ds(tt, HIST), :]
    hist_ref[0] = new_hist
    fbuf[0:HIST, :] = new_hist


def _conv(a3d, state, conv_w, conv_b, ln_g, ln_b):
    bsz, t_len, _ = a3d.shape
    tt = _pick(t_len, (256, 128, 64, 32, 16, 8))
    rc = min(tt, 64)
    st = jnp.pad(state, ((0, 0), (HIST - (CONV_W - 1), 0), (0, 0)))
    fixed = lambda b, t: (0, 0)
    y, hist = pl.pallas_call(
        functools.partial(_conv_body, tt=tt, rc=rc),
        grid=(bsz, t_len // tt),
        in_specs=[pl.BlockSpec((1, tt, 3 * C_A), lambda b, t: (b, t, 0)),
                  pl.BlockSpec((1, HIST, C_A), lambda b, t: (b, 0, 0)),
                  pl.BlockSpec((CONV_W, C_A), fixed),
                  pl.BlockSpec((1, C_A), fixed),
                  pl.BlockSpec((1, C_A), fixed),
                  pl.BlockSpec((1, C_A), fixed)],
        out_specs=[pl.BlockSpec((1, tt, C_A), lambda b, t: (b, t, 0)),
                   pl.BlockSpec((1, HIST, C_A), lambda b, t: (b, 0, 0))],
        out_shape=[jax.ShapeDtypeStruct((bsz, t_len, C_A), F32),
                   jax.ShapeDtypeStruct((bsz, HIST, C_A), F32)],
        scratch_shapes=[pltpu.VMEM((HIST + max(tt, 8), C_A), F32)],
        compiler_params=_cparams(2),
        name="conv",
    )(a3d, st, conv_w, conv_b.reshape(1, C_A), ln_g.reshape(1, C_A), ln_b.reshape(1, C_A))
    return y, hist[:, HIST - (CONV_W - 1):, :]


def _lambda(lq, lam_init):
    s01 = jnp.sum(lq[0:1, :] * lq[1:2, :], axis=-1, keepdims=True)
    s23 = jnp.sum(lq[2:3, :] * lq[3:4, :], axis=-1, keepdims=True)
    return jnp.exp(s01) - jnp.exp(s23) + lam_init


def _attn_finish(o, g, z, lam_init):
    y = o * lax.rsqrt(jnp.mean(o * o, axis=-1, keepdims=True) + 1e-6) * g * (1.0 - lam_init)
    return y * _silu(z)


def _attn_body(q_ref, k_ref, v_ref, z_ref, lq_ref, g_ref, o_ref, m_scr, l_scr, acc_scr,
               *, bq, bk, lam_init):
    i = pl.program_id(1)
    j = pl.program_id(2)
    j_first_diag = (i * bq) // bk
    j_last = ((i + 1) * bq - 1) // bk

    @pl.when(j == 0)
    def _():
        m_scr[...] = jnp.full(m_scr.shape, NEG, F32)
        l_scr[...] = jnp.zeros(l_scr.shape, F32)
        acc_scr[...] = jnp.zeros(acc_scr.shape, F32)

    def step(masked):
        q = q_ref[...]
        k = k_ref[...]
        v = v_ref[...]
        lane = lax.broadcasted_iota(jnp.int32, q.shape, 1)
        zero = jnp.zeros_like(q)
        if masked:
            qpos = i * bq + lax.broadcasted_iota(jnp.int32, (bq, bk), 0)
            kpos = j * bk + lax.broadcasted_iota(jnp.int32, (bq, bk), 1)
            keep = kpos <= qpos
        for c in range(2):
            qc = jnp.where((lane < DK_B) == (c == 0), q, zero)
            s = lax.dot_general(qc, k, (((1,), (1,)), ((), ())), preferred_element_type=F32)
            if masked:
                s = jnp.where(keep, s, NEG)
            m_prev = m_scr[c]
            m_new = jnp.maximum(m_prev, jnp.max(s, axis=-1, keepdims=True))
            alpha = jnp.exp(m_prev - m_new)
            p = jnp.exp(s - m_new)
            l_scr[c] = alpha * l_scr[c] + jnp.sum(p, axis=-1, keepdims=True)
            acc_scr[c] = alpha * acc_scr[c] + jnp.dot(p.astype(BF16), v, preferred_element_type=F32)
            m_scr[c] = m_new

    @pl.when(j < j_first_diag)
    def _():
        step(False)

    @pl.when(jnp.logical_and(j >= j_first_diag, j <= j_last))
    def _():
        step(True)

    @pl.when(j == j_last)
    def _():
        lam = _lambda(lq_ref[...], lam_init)
        o = acc_scr[0] / l_scr[0] - lam * (acc_scr[1] / l_scr[1])
        o_ref[...] = _attn_finish(o, g_ref[...], z_ref[...], lam_init)


def _attn_prompt(q, k, v, z, lam_qk, attn_g, lam_init):
    t_len = q.shape[0]
    bq = _pick(t_len, (512, 256, 128))
    bk = _pick(t_len, (512, 256, 128))
    nq, nk = t_len // bq, t_len // bk
    kv_map = lambda h, i, j: (jnp.minimum(j, ((i + 1) * bq - 1) // bk), h)
    q_map = lambda h, i, j: (i, h)
    fixed = lambda h, i, j: (0, 0)
    return pl.pallas_call(
        functools.partial(_attn_body, bq=bq, bk=bk, lam_init=lam_init),
        grid=(H_B, nq, nk),
        in_specs=[pl.BlockSpec((bq, DV_B), q_map),
                  pl.BlockSpec((bk, DV_B), kv_map),
                  pl.BlockSpec((bk, DV_B), kv_map),
                  pl.BlockSpec((bq, DV_B), q_map),
                  pl.BlockSpec((4, DK_B), fixed),
                  pl.BlockSpec((1, DV_B), fixed)],
        out_specs=pl.BlockSpec((bq, DV_B), q_map),
        out_shape=jax.ShapeDtypeStruct((t_len, D_B), F32),
        scratch_shapes=[pltpu.VMEM((2, bq, 1), F32), pltpu.VMEM((2, bq, 1), F32),
                        pltpu.VMEM((2, bq, DV_B), F32)],
        compiler_params=_cparams(3),
        name="attn_prompt",
    )(q, k, v, z, lam_qk, attn_g.reshape(1, DV_B))


QROWS = 16


def _decode_body(pt_ref, q_ref, kn_ref, vn_ref, z_ref, lq_ref, g_ref, *rest,
                 n_pages_step, t_new, lam_init):
    k_refs = rest[:n_pages_step]
    v_refs = rest[n_pages_step:2 * n_pages_step]
    o_ref, m_scr, l_scr, acc_scr = rest[2 * n_pages_step:]
    g = pl.program_id(1)
    n_groups = pl.num_programs(1)

    @pl.when(g == 0)
    def _():
        m_scr[...] = jnp.full(m_scr.shape, NEG, F32)
        l_scr[...] = jnp.zeros(l_scr.shape, F32)
        acc_scr[...] = jnp.zeros(acc_scr.shape, F32)

    q16 = q_ref[0]
    row = lax.broadcasted_iota(jnp.int32, q16.shape, 0)
    lane = lax.broadcasted_iota(jnp.int32, q16.shape, 1)
    qmask = ((lane % DV_B) < DK_B) == (row < QROWS // 2)
    qd = jnp.where(qmask, q16, jnp.zeros_like(q16))

    for h in range(H_B):
        hs = slice(h * DV_B, (h + 1) * DV_B)
        qh = qd[:, hs]
        s_list = [lax.dot_general(qh, kr[:, hs].astype(BF16), (((1,), (1,)), ((), ())),
                                  preferred_element_type=F32) for kr in k_refs]
        m_prev = m_scr[h]
        m_cur = s_list[0].max(axis=-1, keepdims=True)
        for s in s_list[1:]:
            m_cur = jnp.maximum(m_cur, s.max(axis=-1, keepdims=True))
        m_new = jnp.maximum(m_prev, m_cur)
        alpha = jnp.exp(m_prev - m_new)
        l_new = alpha * l_scr[h]
        acc = alpha * acc_scr[h]
        for s, vr in zip(s_list, v_refs):
            p = jnp.exp(s - m_new)
            l_new = l_new + jnp.sum(p, axis=-1, keepdims=True)
            acc = acc + jnp.dot(p.astype(BF16), vr[:, hs].astype(BF16), preferred_element_type=F32)
        m_scr[h] = m_new
        l_scr[h] = l_new
        acc_scr[h] = acc

    @pl.when(g == n_groups - 1)
    def _():
        lam = _lambda(lq_ref[...], lam_init)
        kn = kn_ref[0].astype(BF16).astype(F32)
        vn = vn_ref[0].astype(BF16).astype(F32)
        qf = qd.astype(F32)
        tok = lax.broadcasted_iota(jnp.int32, (QROWS, 1), 0) % (QROWS // 2)
        for h in range(H_B):
            hs = slice(h * DV_B, (h + 1) * DV_B)
            s_new = []
            for jn in range(t_new):
                sj = jnp.sum(qf[:, hs] * kn[jn:jn + 1, hs], axis=-1, keepdims=True)
                s_new.append(jnp.where(tok >= jn, sj, NEG))
            m_prev = m_scr[h]
            m_new = m_prev
            for sj in s_new:
                m_new = jnp.maximum(m_new, sj)
            alpha = jnp.exp(m_prev - m_new)
            l_fin = alpha * l_scr[h]
            acc = alpha * acc_scr[h]
            for jn, sj in enumerate(s_new):
                pj = jnp.exp(sj - m_new)
                l_fin = l_fin + pj
                acc = acc + pj.astype(BF16).astype(F32) * vn[jn:jn + 1, hs]
            a = acc / l_fin
            o = a[0:QROWS // 2, :] - lam * a[QROWS // 2:QROWS, :]
            o_ref[0, :, hs] = _attn_finish(o, g_ref[...], z_ref[0, :, hs], lam_init)


def _attn_decode(q, k_new, v_new, z, cache_k_l, cache_v_l, page_table, lam_qk, attn_g, lam_init,
                 layer):
    bsz, t_new, _ = q.shape
    n_pages = page_table.shape[1]
    page = cache_k_l.shape[2]
    pps = _pick(n_pages, (8, 4, 2, 1))
    half = QROWS // 2
    q16 = jnp.zeros((bsz, QROWS, D_B), BF16)
    q16 = q16.at[:, 0:t_new].set(q).at[:, half:half + t_new].set(q)
    z8 = jnp.pad(z, ((0, 0), (0, half - t_new), (0, 0)))
    fixed = lambda b, g, pt: (0, 0)
    per_b = lambda b, g, pt: (b, 0, 0)

    def page_spec(jp):
        return pl.BlockSpec((None, None, page, D_B),
                            lambda b, g, pt: (layer, pt[b, g * pps + jp], 0, 0))

    in_specs = [pl.BlockSpec((1, QROWS, D_B), per_b),
                pl.BlockSpec((1, t_new, D_B), per_b),
                pl.BlockSpec((1, t_new, D_B), per_b),
                pl.BlockSpec((1, half, D_B), per_b),
                pl.BlockSpec((4, DK_B), fixed),
                pl.BlockSpec((1, DV_B), fixed)]
    in_specs += [page_spec(jp) for jp in range(pps)] * 2
    out = pl.pallas_call(
        functools.partial(_decode_body, n_pages_step=pps, t_new=t_new, lam_init=lam_init),
        grid_spec=pltpu.PrefetchScalarGridSpec(
            num_scalar_prefetch=1,
            grid=(bsz, n_pages // pps),
            in_specs=in_specs,
            out_specs=pl.BlockSpec((1, half, D_B), per_b),
            scratch_shapes=[pltpu.VMEM((H_B, QROWS, 1), F32), pltpu.VMEM((H_B, QROWS, 1), F32),
                            pltpu.VMEM((H_B, QROWS, DV_B), F32)]),
        out_shape=jax.ShapeDtypeStruct((bsz, half, D_B), F32),
        compiler_params=_cparams(2),
        name="attn_decode",
    )(page_table, q16, k_new, v_new, z8, lam_qk, attn_g.reshape(1, DV_B),
      *([cache_k_l] * pps), *([cache_v_l] * pps))
    return out[:, :t_new, :]


def _mlstm_body(c_ref, gc_ref, gr_ref, fbc_ref, fbr_ref, mg_ref, c0_ref, n0_ref, m0_ref,
                y_ref, cn_ref, nn_ref, mn_ref, c_scr, n_scr, m_scr, *, ln, t_valid, t_pad):
    t = pl.program_id(1)
    n_t = pl.num_programs(1)

    @pl.when(t == 0)
    def _():
        c_scr[...] = c0_ref[0]
        n_scr[...] = n0_ref[0]
        m_scr[...] = m0_ref[0]

    cblk = c_ref[0]
    gcol = gc_ref[0]
    grow = gr_ref[0]
    is_f_c = lax.broadcasted_iota(jnp.int32, (ln, N_GATE), 1) >= H_C
    is_f_r = lax.broadcasted_iota(jnp.int32, (N_GATE, ln), 0) >= H_C
    g_c = jnp.where(is_f_c, _log_sigmoid(gcol + fbc_ref[...]), gcol)
    g_r = jnp.where(is_f_r, _log_sigmoid(grow + fbr_ref[...]), grow)
    if t_pad != t_valid:
        ok_c = (t * ln + lax.broadcasted_iota(jnp.int32, (ln, N_GATE), 0)) < t_valid
        ok_r = (t * ln + lax.broadcasted_iota(jnp.int32, (N_GATE, ln), 1)) < t_valid
        g_c = jnp.where(ok_c, g_c, jnp.where(is_f_c, 0.0, NEG))
        g_r = jnp.where(ok_r, g_r, jnp.where(is_f_r, 0.0, NEG))
    rr = lax.broadcasted_iota(jnp.int32, (ln, ln), 0)
    cc = lax.broadcasted_iota(jnp.int32, (ln, ln), 1)
    causal = rr >= cc
    tri = causal.astype(F32)
    cum_c = jnp.dot(tri, jnp.where(is_f_c, g_c, 0.0), preferred_element_type=F32,
                    precision=lax.Precision.HIGHEST)
    cum_r = lax.dot_general(jnp.where(is_f_r, g_r, 0.0), tri, (((1,), (1,)), ((), ())),
                            preferred_element_type=F32, precision=lax.Precision.HIGHEST)
    m_all = m_scr[...]
    lane = lax.broadcasted_iota(jnp.int32, (1, 128), 1)
    m_out = jnp.zeros((1, 128), F32)

    for h in range(H_C):
        qf = cblk[:, h * DK_C:(h + 1) * DK_C]
        kf = cblk[:, D_C + h * DK_C:D_C + (h + 1) * DK_C] * KC_SCALE
        vf = cblk[:, 2 * D_C + h * DV_C:2 * D_C + (h + 1) * DV_C]
        of = cblk[:, 3 * D_C + h * DV_C:3 * D_C + (h + 1) * DV_C]
        zf = cblk[:, 4 * D_C + h * DV_C:4 * D_C + (h + 1) * DV_C]
        qb, kb, vb = qf.astype(BF16), kf.astype(BF16), vf.astype(BF16)
        bc = cum_c[:, H_C + h:H_C + h + 1]
        igc = g_c[:, h:h + 1]
        br = cum_r[H_C + h:H_C + h + 1, :]
        igr = g_r[h:h + 1, :]
        m_prev = m_all[:, h:h + 1]
        c_mat = c_scr[h]
        n_vec = n_scr[h]

        log_d = jnp.where(causal, bc - br + igr, NEG)
        inter = bc + m_prev
        m_t = jnp.maximum(inter, jnp.max(log_d, axis=-1, keepdims=True))
        d_w = jnp.where(causal, jnp.exp(log_d - m_t), 0.0)
        i_w = jnp.exp(inter - m_t)
        qk = lax.dot_general(qb, kb, (((1,), (1,)), ((), ())), preferred_element_type=F32)
        w = d_w * qk
        num = (jnp.dot(w.astype(BF16), vb, preferred_element_type=F32)
               + i_w * jnp.dot(qb, c_mat.astype(BF16), preferred_element_type=F32))
        den = jnp.sum(w, axis=-1, keepdims=True) + i_w * jnp.sum(qf * n_vec, axis=-1, keepdims=True)
        hh = num / jnp.maximum(jnp.abs(den), jnp.exp(-m_t))

        b_last = bc[ln - 1:ln, :]
        m_last = m_t[ln - 1:ln, :]
        i_last = i_w[ln - 1:ln, :]
        w_last = jnp.exp(b_last - bc + igc - m_last)
        kw = w_last * kf
        c_scr[h] = i_last * c_mat + lax.dot_general(kw.astype(BF16), vb, (((0,), (0,)), ((), ())),
                                                    preferred_element_type=F32)
        n_scr[h] = i_last * n_vec + jnp.sum(kw, axis=0, keepdims=True)
        m_out = jnp.where(lane == h, m_last, m_out)

        hg = _sigmoid(of) * hh
        yh = hg * lax.rsqrt(jnp.mean(hg * hg, axis=-1, keepdims=True) + 1e-6)
        yh = yh * mg_ref[:, h * DV_C:(h + 1) * DV_C]
        y_ref[0, :, h * DV_C:(h + 1) * DV_C] = yh * _silu(zf)

    m_scr[...] = m_out

    @pl.when(t == n_t - 1)
    def _():
        cn_ref[0] = c_scr[...]
        nn_ref[0] = n_scr[...]
        mn_ref[0] = m_scr[...]


def _mlstm(c3d, gcol, grow, f_bias, mlstm_g, c0, n0, m0, ln):
    bsz, t_valid, _ = c3d.shape
    t_pad = -(-t_valid // ln) * ln
    if t_pad != t_valid:
        extra = t_pad - t_valid
        c3d = jnp.pad(c3d, ((0, 0), (0, extra), (0, 0)))
        gcol = jnp.pad(gcol, ((0, 0), (0, extra), (0, 0)))
        grow = jnp.pad(grow, ((0, 0), (0, 0), (0, extra)))
    m0p = jnp.pad(m0, ((0, 0), (0, 128 - H_C))).reshape(bsz, 1, 128)
    n0r = n0.reshape(bsz, H_C, 1, DK_C)
    fb8 = jnp.concatenate([jnp.zeros((H_C,), F32), f_bias])
    fixed = lambda b, t: (0, 0)
    y, cn, nn, mn = pl.pallas_call(
        functools.partial(_mlstm_body, ln=ln, t_valid=t_valid, t_pad=t_pad),
        grid=(bsz, t_pad // ln),
        in_specs=[pl.BlockSpec((1, ln, 5 * D_C), lambda b, t: (b, t, 0)),
                  pl.BlockSpec((1, ln, N_GATE), lambda b, t: (b, t, 0)),
                  pl.BlockSpec((1, N_GATE, ln), lambda b, t: (b, 0, t)),
                  pl.BlockSpec((1, N_GATE), fixed),
                  pl.BlockSpec((N_GATE, 1), fixed),
                  pl.BlockSpec((1, D_C), fixed),
                  pl.BlockSpec((1, H_C, DK_C, DV_C), lambda b, t: (b, 0, 0, 0)),
                  pl.BlockSpec((1, H_C, 1, DK_C), lambda b, t: (b, 0, 0, 0)),
                  pl.BlockSpec((1, 1, 128), lambda b, t: (b, 0, 0))],
        out_specs=[pl.BlockSpec((1, ln, D_C), lambda b, t: (b, t, 0)),
                   pl.BlockSpec((1, H_C, DK_C, DV_C), lambda b, t: (b, 0, 0, 0)),
                   pl.BlockSpec((1, H_C, 1, DK_C), lambda b, t: (b, 0, 0, 0)),
                   pl.BlockSpec((1, 1, 128), lambda b, t: (b, 0, 0))],
        out_shape=[jax.ShapeDtypeStruct((bsz, t_pad, D_C), F32),
                   jax.ShapeDtypeStruct((bsz, H_C, DK_C, DV_C), F32),
                   jax.ShapeDtypeStruct((bsz, H_C, 1, DK_C), F32),
                   jax.ShapeDtypeStruct((bsz, 1, 128), F32)],
        scratch_shapes=[pltpu.VMEM((H_C, DK_C, DV_C), F32), pltpu.VMEM((H_C, 1, DK_C), F32),
                        pltpu.VMEM((1, 128), F32)],
        compiler_params=_cparams(2),
        name="mlstm",
    )(c3d, gcol, grow, fb8.reshape(1, N_GATE), fb8.reshape(N_GATE, 1), mlstm_g.reshape(1, D_C),
      c0, n0r, m0p)
    return y[:, :t_valid], cn, nn.reshape(bsz, H_C, DK_C), mn[:, 0, :H_C]


def _outproj_body(x_ref, ya_ref, az_ref, yb_ref, yc_ref, pw_ref, w_ref, g_ref, o_ref):
    ya = jnp.dot(ya_ref[...].astype(BF16), pw_ref[...], preferred_element_type=F32)
    ya = ya * _silu(az_ref[...])
    out = jnp.dot(ya.astype(BF16), w_ref[0:C_A, :], preferred_element_type=F32)
    out = out + jnp.dot(yb_ref[...].astype(BF16), w_ref[C_A:C_A + D_B, :], preferred_element_type=F32)
    out = out + jnp.dot(yc_ref[...].astype(BF16), w_ref[C_A + D_B:, :], preferred_element_type=F32)
    y = out * lax.rsqrt(jnp.mean(out * out, axis=-1, keepdims=True) + 1e-6) * g_ref[...]
    o_ref[...] = x_ref[...] + y


def _outproj(x2d, ya, a2d, yb, yc, conv_pw, w_out, ln_post):
    rows, d = x2d.shape
    tm = _pick(rows, (256, 128))
    row = lambda i: (i, 0)
    fixed = lambda i: (0, 0)
    return pl.pallas_call(
        _outproj_body,
        grid=(rows // tm,),
        in_specs=[pl.BlockSpec((tm, d), row),
                  pl.BlockSpec((tm, C_A), row),
                  pl.BlockSpec((tm, C_A), lambda i: (i, 2)),
                  pl.BlockSpec((tm, D_B), row),
                  pl.BlockSpec((tm, D_C), row),
                  pl.BlockSpec((C_A, C_A), fixed),
                  pl.BlockSpec((C_A + D_B + D_C, d), fixed),
                  pl.BlockSpec((1, d), fixed)],
        out_specs=pl.BlockSpec((tm, d), row),
        out_shape=jax.ShapeDtypeStruct((rows, d), F32),
        compiler_params=_cparams(1),
        name="outproj",
    )(x2d, ya, a2d, yb, yc, conv_pw.astype(BF16), w_out.astype(BF16), ln_post.reshape(1, d))


def _layer(x, lidx, conv_state, mstate, attn, weights, mlstm_chunk):
    (ln_pre, w_in, b_in, conv_w, conv_b, conv_ln_g, conv_ln_b, conv_pw, lam_qk, attn_g, f_bias,
     mlstm_g, w_out, ln_post) = weights
    bsz, t_len, d = x.shape
    rows = bsz * t_len
    lam_init = 0.8 - 0.6 * math.exp(-0.3 * lidx)
    x2d = x.reshape(rows, d)
    a, q, k, kb, v, vb, z, c, gcol, grow = _inproj(x2d, ln_pre, w_in, b_in)

    ya, conv_new = _conv(a.reshape(bsz, t_len, 3 * C_A), conv_state, conv_w, conv_b, conv_ln_g,
                         conv_ln_b)
    yb = attn(q, kb, vb, k, v, z, lam_qk, attn_g, lam_init)
    grow3 = jnp.swapaxes(grow.reshape(N_GATE, bsz, t_len), 0, 1)
    yc, c_new, n_new, m_new = _mlstm(c.reshape(bsz, t_len, 5 * D_C), gcol.reshape(bsz, t_len, N_GATE),
                                     grow3, f_bias, mlstm_g, *mstate, mlstm_chunk)
    x_new = _outproj(x2d, ya.reshape(rows, C_A), a, yb.reshape(rows, D_B), yc.reshape(rows, D_C),
                     conv_pw, w_out, ln_post)
    return (x_new.reshape(bsz, t_len, d), k.reshape(bsz, t_len, H_B, 2 * DK_B),
            v.reshape(bsz, t_len, H_B, DV_B), conv_new, (c_new, n_new, m_new))


def kernel(x_prompt, x_sample, cache_k, cache_v, state_conv, state_C, state_n, state_m, page_table,
           ln_pre, w_in, b_in, conv_w, conv_b, conv_ln_g, conv_ln_b, conv_pw, lam_qk, attn_g, f_bias,
           mlstm_g, w_out, ln_post):
    depth = w_in.shape[0]
    bp, tp, _ = x_prompt.shape
    bs, ts, _ = x_sample.shape
    assert bp == 1, "prompt attention sweep is written for a single prompt sequence"
    n_pool, page = cache_k.shape[1], cache_k.shape[2]
    ck = cache_k.reshape(depth, n_pool, page, D_B)
    cv = cache_v.reshape(depth, n_pool, page, D_B)

    def weights(l):
        return (ln_pre[l], w_in[l], b_in[l], conv_w[l], conv_b[l], conv_ln_g[l], conv_ln_b[l],
                conv_pw[l], lam_qk[l], attn_g[l], f_bias[l], mlstm_g[l], w_out[l], ln_post[l])

    def prompt_attn(q, kb, vb, k, v, z, lq, g, lam_init):
        return _attn_prompt(q, kb, vb, z, lq, g, lam_init)

    prompt_chunk = _pick(tp, (128, 64, 32, 16, 8))
    xp = x_prompt
    outs_p = [[] for _ in range(6)]
    for l in range(depth):
        buf0 = jnp.zeros((bp, CONV_W - 1, C_A), F32)
        m0 = (jnp.zeros((bp, H_C, DK_C, DV_C), F32), jnp.zeros((bp, H_C, DK_C), F32),
              jnp.zeros((bp, H_C), F32))
        xp, kr, vr, cb, (cm, nv, mv) = _layer(xp, l, buf0, m0, prompt_attn, weights(l), prompt_chunk)
        for lst, val in zip(outs_p, (kr, vr, cb, cm, nv, mv)):
            lst.append(val)

    xs = x_sample
    outs_s = [[] for _ in range(6)]
    for l in range(depth):
        def sample_attn(q, kb, vb, k, v, z, lq, g, lam_init, l=l):
            return _attn_decode(q.reshape(bs, ts, D_B), k.reshape(bs, ts, D_B), v.reshape(bs, ts, D_B),
                                z.reshape(bs, ts, D_B), ck, cv, page_table, lq, g, lam_init, l)

        st = (state_C[l], state_n[l], state_m[l])
        xs, kr, vr, cb, (cm, nv, mv) = _layer(xs, l, state_conv[l], st, sample_attn, weights(l), 8)
        for lst, val in zip(outs_s, (kr, vr, cb, cm, nv, mv)):
            lst.append(val)

    return (xp, xs, *[jnp.stack(o) for o in outs_p], *[jnp.stack(o) for o in outs_s])
```

```python
import functools
import math

import jax
import jax.numpy as jnp
from jax import lax
from jax.experimental import pallas as pl
from jax.experimental.pallas import tpu as pltpu

F32 = jnp.float32
BF16 = jnp.bfloat16

C_A = 256
CONV_W = 31
H_B = 4
DK_B = 64
DV_B = 128
D_B = H_B * DV_B
H_C = 4
DK_C = 64
DV_C = 64
D_C = H_C * DV_C

OFF_A = 0
OFF_Q = 3 * C_A
OFF_K = OFF_Q + 2 * H_B * DK_B
OFF_V = OFF_K + 2 * H_B * DK_B
OFF_Z = OFF_V + D_B
OFF_C = OFF_Z + D_B
OFF_G = OFF_C + 5 * D_C
N_MAIN = OFF_G
N_GATE = 2 * H_C

HIST = 32
NEG = -1e30
QK_SCALE = DK_B ** -0.5
KC_SCALE = DK_C ** -0.5

VMEM_LIMIT_BYTES = 48 * 1024 * 1024


def _cparams(n_axes):
    return pltpu.CompilerParams(dimension_semantics=("arbitrary",) * n_axes,
                                vmem_limit_bytes=VMEM_LIMIT_BYTES)


def _pick(n, candidates):
    for c in candidates:
        if n % c == 0:
            return c
    return n


def _sigmoid(x):
    return 1.0 / (1.0 + jnp.exp(-x))


def _silu(x):
    return x * _sigmoid(x)


def _log_sigmoid(x):
    return jnp.minimum(x, 0.0) - jnp.log(1.0 + jnp.exp(-jnp.abs(x)))


def _inproj_body(x_ref, g_ref, w_ref, b_ref, wg_ref, wgt_ref, bgc_ref, bgr_ref, *rest, flash_operands):
    if flash_operands:
        wvt_ref, bvt_ref = rest[:2]
        rest = rest[2:]
    a_ref, q_ref, k_ref, v_ref, z_ref, c_ref, gc_ref, gr_ref = rest[:8]
    xf = x_ref[...]
    h = xf * lax.rsqrt(jnp.mean(xf * xf, axis=-1, keepdims=True) + 1e-6) * g_ref[...]
    hb = h.astype(BF16)
    rows_last = (((1,), (1,)), ((), ()))

    def seg(lo, hi):
        return jnp.dot(hb, w_ref[:, lo:hi], preferred_element_type=F32) + b_ref[:, lo:hi]

    a_ref[...] = seg(OFF_A, OFF_Q)
    q_ref[...] = (seg(OFF_Q, OFF_K) * QK_SCALE).astype(BF16)
    kk = seg(OFF_K, OFF_V)
    k_ref[...] = kk
    v_ref[...] = seg(OFF_V, OFF_Z)
    z_ref[...] = seg(OFF_Z, OFF_C)
    c_ref[...] = seg(OFF_C, OFF_G)
    gc_ref[...] = jnp.dot(hb, wg_ref[...], preferred_element_type=F32) + bgc_ref[...]
    gr_ref[...] = lax.dot_general(wgt_ref[...], hb, rows_last, preferred_element_type=F32) + bgr_ref[...]
    if flash_operands:
        kb_ref, vt_ref = rest[8:]
        kb_ref[...] = kk.astype(BF16)
        vt = lax.dot_general(wvt_ref[...], hb, rows_last, preferred_element_type=F32) + bvt_ref[...]
        vt_ref[...] = vt.astype(BF16)


def _inproj(x2d, ln_pre, w_in, b_in, flash_operands):
    rows, d = x2d.shape
    tm = _pick(rows, (256, 128))
    w_main = w_in[:, :N_MAIN].astype(BF16)
    w_gate = w_in[:, N_MAIN:].astype(BF16)
    b_main = b_in[:N_MAIN].reshape(1, N_MAIN)
    b_gate = b_in[N_MAIN:]
    row = lambda i: (i, 0)
    col = lambda i: (0, i)
    fixed = lambda i: (0, 0)
    widths = (OFF_Q - OFF_A, OFF_K - OFF_Q, OFF_V - OFF_K, OFF_Z - OFF_V, OFF_C - OFF_Z, OFF_G - OFF_C)
    dtypes = (F32, BF16, F32, F32, F32, F32)
    out_shape = [jax.ShapeDtypeStruct((rows, w), dt) for w, dt in zip(widths, dtypes)]
    out_specs = [pl.BlockSpec((tm, w), row) for w in widths]
    out_shape += [jax.ShapeDtypeStruct((rows, N_GATE), F32), jax.ShapeDtypeStruct((N_GATE, rows), F32)]
    out_specs += [pl.BlockSpec((tm, N_GATE), row), pl.BlockSpec((N_GATE, tm), col)]
    in_specs = [pl.BlockSpec((tm, d), row),
                pl.BlockSpec((1, d), fixed),
                pl.BlockSpec((d, N_MAIN), fixed),
                pl.BlockSpec((1, N_MAIN), fixed),
                pl.BlockSpec((d, N_GATE), fixed),
                pl.BlockSpec((N_GATE, d), fixed),
                pl.BlockSpec((1, N_GATE), fixed),
                pl.BlockSpec((N_GATE, 1), fixed)]
    args = [x2d, ln_pre.reshape(1, d), w_main, b_main, w_gate, w_gate.T,
            b_gate.reshape(1, N_GATE), b_gate.reshape(N_GATE, 1)]
    if flash_operands:
        in_specs += [pl.BlockSpec((D_B, d), fixed), pl.BlockSpec((D_B, 1), fixed)]
        args += [w_main[:, OFF_V:OFF_Z].T, b_in[OFF_V:OFF_Z].reshape(D_B, 1)]
        out_shape += [jax.ShapeDtypeStruct((rows, D_B), BF16), jax.ShapeDtypeStruct((D_B, rows), BF16)]
        out_specs += [pl.BlockSpec((tm, D_B), row), pl.BlockSpec((D_B, tm), col)]
    return pl.pallas_call(
        functools.partial(_inproj_body, flash_operands=flash_operands),
        grid=(rows // tm,),
        in_specs=in_specs,
        out_specs=out_specs,
        out_shape=out_shape,
        compiler_params=_cparams(1),
        name="inproj",
    )(*args)


def _conv_body(a_ref, st_ref, cw_ref, cb_ref, lg_ref, lb_ref, y_ref, hist_ref, fbuf, *, tt, rc):
    t = pl.program_id(1)

    @pl.when(t == 0)
    def _():
        fbuf[0:HIST, :] = st_ref[0]

    a = a_ref[0]
    fbuf[HIST:HIST + tt, :] = a[:, 0:C_A] * _sigmoid(a[:, C_A:2 * C_A])
    first = HIST - (CONV_W - 1)
    for r0 in range(0, tt, rc):
        acc = jnp.zeros((rc, C_A), F32)
        for j in range(CONV_W):
            acc = acc + cw_ref[j:j + 1, :] * fbuf[pl.ds(first + j + r0, rc), :]
        y = acc + cb_ref[...]
        mu = jnp.mean(y, axis=-1, keepdims=True)
        yc = y - mu
        yn = yc * lax.rsqrt(jnp.mean(yc * yc, axis=-1, keepdims=True) + 1e-5)
        y_ref[0, r0:r0 + rc, :] = _silu(yn * lg_ref[...] + lb_ref[...])
    new_hist = fbuf[pl.ds(tt, HIST), :]
    hist_ref[0] = new_hist
    fbuf[0:HIST, :] = new_hist


def _conv(a3d, state, conv_w, conv_b, ln_g, ln_b):
    bsz, t_len, _ = a3d.shape
    tt = _pick(t_len, (256, 128, 64, 32, 16, 8))
    rc = min(tt, 64)
    st = jnp.pad(state, ((0, 0), (HIST - (CONV_W - 1), 0), (0, 0)))
    fixed = lambda b, t: (0, 0)
    y, hist = pl.pallas_call(
        functools.partial(_conv_body, tt=tt, rc=rc),
        grid=(bsz, t_len // tt),
        in_specs=[pl.BlockSpec((1, tt, 3 * C_A), lambda b, t: (b, t, 0)),
                  pl.BlockSpec((1, HIST, C_A), lambda b, t: (b, 0, 0)),
                  pl.BlockSpec((CONV_W, C_A), fixed),
                  pl.BlockSpec((1, C_A), fixed),
                  pl.BlockSpec((1, C_A), fixed),
                  pl.BlockSpec((1, C_A), fixed)],
        out_specs=[pl.BlockSpec((1, tt, C_A), lambda b, t: (b, t, 0)),
                   pl.BlockSpec((1, HIST, C_A), lambda b, t: (b, 0, 0))],
        out_shape=[jax.ShapeDtypeStruct((bsz, t_len, C_A), F32),
                   jax.ShapeDtypeStruct((bsz, HIST, C_A), F32)],
        scratch_shapes=[pltpu.VMEM((HIST + max(tt, 8), C_A), F32)],
        compiler_params=_cparams(2),
        name="conv",
    )(a3d, st, conv_w, conv_b.reshape(1, C_A), ln_g.reshape(1, C_A), ln_b.reshape(1, C_A))
    return y, hist[:, HIST - (CONV_W - 1):, :]


def _lambda(lq, lam_init):
    s01 = jnp.sum(lq[0:1, :] * lq[1:2, :], axis=-1, keepdims=True)
    s23 = jnp.sum(lq[2:3, :] * lq[3:4, :], axis=-1, keepdims=True)
    return jnp.exp(s01) - jnp.exp(s23) + lam_init


def _attn_finish(o, g, z, lam_init):
    y = o * lax.rsqrt(jnp.mean(o * o, axis=-1, keepdims=True) + 1e-6) * g * (1.0 - lam_init)
    return y * _silu(z)


N_MAPS = 2 * H_B


def _attn_body(it_ref, jt_ref, q_ref, k_ref, vt_ref, z_ref, lq_ref, g_ref, o_ref,
               m_scr, l_scr, acc_scr, *, blk, lam_init):
    step_id = pl.program_id(0)
    i = it_ref[step_id]
    j = jt_ref[step_id]

    @pl.when(j == 0)
    def _():
        m_scr[...] = jnp.full(m_scr.shape, NEG, F32)
        l_scr[...] = jnp.zeros(l_scr.shape, F32)
        acc_scr[...] = jnp.zeros(acc_scr.shape, F32)

    def sweep(diagonal):
        if diagonal:
            kpos = lax.broadcasted_iota(jnp.int32, (blk, blk), 0)
            qpos = lax.broadcasted_iota(jnp.int32, (blk, blk), 1)
            keep = kpos <= qpos
        lane = lax.broadcasted_iota(jnp.int32, (blk, DV_B), 1)
        for h in range(H_B):
            hs = slice(h * DV_B, (h + 1) * DV_B)
            qh = q_ref[:, hs]
            kh = k_ref[:, hs]
            vth = vt_ref[hs, :]
            for c in range(2):
                r = 2 * h + c
                qc = jnp.where((lane < DK_B) == (c == 0), qh, jnp.zeros_like(qh))
                st = lax.dot_general(kh, qc, (((1,), (1,)), ((), ())), preferred_element_type=F32)
                if diagonal:
                    st = jnp.where(keep, st, NEG)
                m_prev = m_scr[r:r + 1, :]
                m_new = jnp.maximum(m_prev, jnp.max(st, axis=0, keepdims=True))
                alpha = jnp.exp(m_prev - m_new)
                pt = jnp.exp(st - m_new)
                l_scr[r:r + 1, :] = alpha * l_scr[r:r + 1, :] + jnp.sum(pt, axis=0, keepdims=True)
                acc_scr[r] = alpha * acc_scr[r] + jnp.dot(vth, pt.astype(BF16),
                                                          preferred_element_type=F32)
                m_scr[r:r + 1, :] = m_new

    @pl.when(j < i)
    def _():
        sweep(False)

    @pl.when(j == i)
    def _():
        sweep(True)
        lam = _lambda(lq_ref[...], lam_init)
        for h in range(H_B):
            hs = slice(h * DV_B, (h + 1) * DV_B)
            r = 2 * h
            ot = (acc_scr[r] * (1.0 / l_scr[r:r + 1, :])
                  - lam * (acc_scr[r + 1] * (1.0 / l_scr[r + 1:r + 2, :])))
            yt = ot * lax.rsqrt(jnp.mean(ot * ot, axis=0, keepdims=True) + 1e-6) * g_ref[...]
            o_ref[:, hs] = yt.T * (1.0 - lam_init) * _silu(z_ref[:, hs])


def _attn_prompt(q, kb, vt, z, lam_qk, attn_g, lam_init):
    t_len = q.shape[0]
    blk = _pick(t_len, (512, 256, 128))
    nb = t_len // blk
    pairs = [(i, j) for i in range(nb) for j in range(i + 1)]
    i_tab = jnp.asarray([p[0] for p in pairs], jnp.int32)
    j_tab = jnp.asarray([p[1] for p in pairs], jnp.int32)
    q_map = lambda s, it, jt: (it[s], 0)
    k_map = lambda s, it, jt: (jt[s], 0)
    vt_map = lambda s, it, jt: (0, jt[s])
    fixed = lambda s, it, jt: (0, 0)
    return pl.pallas_call(
        functools.partial(_attn_body, blk=blk, lam_init=lam_init),
        grid_spec=pltpu.PrefetchScalarGridSpec(
            num_scalar_prefetch=2,
            grid=(len(pairs),),
            in_specs=[pl.BlockSpec((blk, D_B), q_map),
                      pl.BlockSpec((blk, D_B), k_map),
                      pl.BlockSpec((D_B, blk), vt_map),
                      pl.BlockSpec((blk, D_B), q_map),
                      pl.BlockSpec((4, DK_B), fixed),
                      pl.BlockSpec((DV_B, 1), fixed)],
            out_specs=pl.BlockSpec((blk, D_B), q_map),
            scratch_shapes=[pltpu.VMEM((N_MAPS, blk), F32), pltpu.VMEM((N_MAPS, blk), F32),
                            pltpu.VMEM((N_MAPS, DV_B, blk), F32)]),
        out_shape=jax.ShapeDtypeStruct((t_len, D_B), F32),
        compiler_params=_cparams(1),
        name="attn_prompt",
    )(i_tab, j_tab, q, kb, vt, z, lam_qk, attn_g.reshape(DV_B, 1))


QROWS = 16


QR = H_B * QROWS


def _decode_body(pt_ref, q_ref, kn_ref, vn_ref, z_ref, lq_ref, g_ref, *rest,
                 n_pages_step, t_new, lam_init):
    k_refs = rest[:n_pages_step]
    v_refs = rest[n_pages_step:2 * n_pages_step]
    o_ref, m_scr, l_scr, acc_scr = rest[2 * n_pages_step:]
    g = pl.program_id(1)
    n_groups = pl.num_programs(1)

    @pl.when(g == 0)
    def _():
        m_scr[...] = jnp.full(m_scr.shape, NEG, F32)
        l_scr[...] = jnp.zeros(l_scr.shape, F32)
        acc_scr[...] = jnp.zeros(acc_scr.shape, F32)

    q64 = q_ref[0]
    row = lax.broadcasted_iota(jnp.int32, q64.shape, 0)
    lane = lax.broadcasted_iota(jnp.int32, q64.shape, 1)
    qd = jnp.where((lane < DK_B) == ((row % QROWS) < QROWS // 2), q64, jnp.zeros_like(q64))

    n_cols = k_refs[0].shape[0]
    col_head = lax.broadcasted_iota(jnp.int32, (QR, n_cols), 1) % H_B
    row_head = lax.broadcasted_iota(jnp.int32, (QR, n_cols), 0) // QROWS
    own_head = col_head == row_head
    s_list = []
    for kr in k_refs:
        s = lax.dot_general(qd, kr[...].astype(BF16), (((1,), (1,)), ((), ())),
                            preferred_element_type=F32)
        s_list.append(jnp.where(own_head, s, NEG))
    m_prev = m_scr[...]
    m_cur = s_list[0].max(axis=-1, keepdims=True)
    for s in s_list[1:]:
        m_cur = jnp.maximum(m_cur, s.max(axis=-1, keepdims=True))
    m_new = jnp.maximum(m_prev, m_cur)
    alpha = jnp.exp(m_prev - m_new)
    l_new = alpha * l_scr[...]
    acc = alpha * acc_scr[...]
    for s, vr in zip(s_list, v_refs):
        p = jnp.exp(s - m_new)
        l_new = l_new + jnp.sum(p, axis=-1, keepdims=True)
        acc = acc + jnp.dot(p.astype(BF16), vr[...].astype(BF16), preferred_element_type=F32)
    m_scr[...] = m_new
    l_scr[...] = l_new
    acc_scr[...] = acc

    @pl.when(g == n_groups - 1)
    def _():
        lam = _lambda(lq_ref[...], lam_init)
        qf = qd.astype(F32)
        tok = lax.broadcasted_iota(jnp.int32, (QR, 1), 0) % (QROWS // 2)
        s_new = []
        for jn in range(t_new):
            sj = jnp.sum(qf * kn_ref[0, jn].astype(BF16).astype(F32), axis=-1, keepdims=True)
            s_new.append(jnp.where(tok >= jn, sj, NEG))
        m_prev = m_scr[...]
        m_fin = m_prev
        for sj in s_new:
            m_fin = jnp.maximum(m_fin, sj)
        alpha = jnp.exp(m_prev - m_fin)
        l_fin = alpha * l_scr[...]
        acc = alpha * acc_scr[...]
        for jn, sj in enumerate(s_new):
            pj = jnp.exp(sj - m_fin)
            l_fin = l_fin + pj
            acc = acc + pj.astype(BF16).astype(F32) * vn_ref[0, jn].astype(BF16).astype(F32)
        a = acc / l_fin
        half = QROWS // 2
        for h in range(H_B):
            hs = slice(h * DV_B, (h + 1) * DV_B)
            o = a[h * QROWS:h * QROWS + half, :] - lam * a[h * QROWS + half:(h + 1) * QROWS, :]
            o_ref[0, :, hs] = _attn_finish(o, g_ref[...], z_ref[0, :, hs], lam_init)


def _attn_decode(q, k_new, v_new, z, cache_k_l, cache_v_l, page_table, lam_qk, attn_g, lam_init,
                 layer):
    bsz, t_new, _ = q.shape
    n_pages = page_table.shape[1]
    page_rows = cache_k_l.shape[2]
    pps = _pick(n_pages, (8, 4, 2, 1))
    half = QROWS // 2
    qh = jnp.swapaxes(q.reshape(bsz, t_new, H_B, DV_B), 1, 2)
    q64 = jnp.zeros((bsz, H_B, QROWS, DV_B), BF16)
    q64 = q64.at[:, :, 0:t_new].set(qh).at[:, :, half:half + t_new].set(qh).reshape(bsz, QR, DV_B)
    on_rows = lambda x: jnp.repeat(x.reshape(bsz, t_new, H_B, DV_B), QROWS, axis=2)
    z8 = jnp.pad(z, ((0, 0), (0, half - t_new), (0, 0)))
    fixed = lambda b, g, pt: (0, 0)
    per_b = lambda b, g, pt: (b, 0, 0)
    per_b4 = lambda b, g, pt: (b, 0, 0, 0)

    def page_spec(jp):
        return pl.BlockSpec((None, None, page_rows, DV_B),
                            lambda b, g, pt: (layer, pt[b, g * pps + jp], 0, 0))

    in_specs = [pl.BlockSpec((1, QR, DV_B), per_b),
                pl.BlockSpec((1, t_new, QR, DV_B), per_b4),
                pl.BlockSpec((1, t_new, QR, DV_B), per_b4),
                pl.BlockSpec((1, half, D_B), per_b),
                pl.BlockSpec((4, DK_B), fixed),
                pl.BlockSpec((1, DV_B), fixed)]
    in_specs += [page_spec(jp) for jp in range(pps)] * 2
    out = pl.pallas_call(
        functools.partial(_decode_body, n_pages_step=pps, t_new=t_new, lam_init=lam_init),
        grid_spec=pltpu.PrefetchScalarGridSpec(
            num_scalar_prefetch=1,
            grid=(bsz, n_pages // pps),
            in_specs=in_specs,
            out_specs=pl.BlockSpec((1, half, D_B), per_b),
            scratch_shapes=[pltpu.VMEM((QR, 1), F32), pltpu.VMEM((QR, 1), F32),
                            pltpu.VMEM((QR, DV_B), F32)]),
        out_shape=jax.ShapeDtypeStruct((bsz, half, D_B), F32),
        compiler_params=_cparams(2),
        name="attn_decode",
    )(page_table, q64, on_rows(k_new), on_rows(v_new), z8, lam_qk, attn_g.reshape(1, DV_B),
      *([cache_k_l] * pps), *([cache_v_l] * pps))
    return out[:, :t_new, :]


def _mlstm_body(c_ref, gc_ref, gr_ref, fbc_ref, fbr_ref, mg_ref, c0_ref, n0_ref, m0_ref,
                y_ref, cn_ref, nn_ref, mn_ref, c_scr, n_scr, m_scr, *, ln, t_valid, t_pad):
    t = pl.program_id(1)
    n_t = pl.num_programs(1)

    @pl.when(t == 0)
    def _():
        c_scr[...] = c0_ref[0]
        n_scr[...] = n0_ref[0]
        m_scr[...] = m0_ref[0]

    cblk = c_ref[0]
    gcol = gc_ref[0]
    grow = gr_ref[0]
    is_f_c = lax.broadcasted_iota(jnp.int32, (ln, N_GATE), 1) >= H_C
    is_f_r = lax.broadcasted_iota(jnp.int32, (N_GATE, ln), 0) >= H_C
    g_c = jnp.where(is_f_c, _log_sigmoid(gcol + fbc_ref[...]), gcol)
    g_r = jnp.where(is_f_r, _log_sigmoid(grow + fbr_ref[...]), grow)
    if t_pad != t_valid:
        ok_c = (t * ln + lax.broadcasted_iota(jnp.int32, (ln, N_GATE), 0)) < t_valid
        ok_r = (t * ln + lax.broadcasted_iota(jnp.int32, (N_GATE, ln), 1)) < t_valid
        g_c = jnp.where(ok_c, g_c, jnp.where(is_f_c, 0.0, NEG))
        g_r = jnp.where(ok_r, g_r, jnp.where(is_f_r, 0.0, NEG))
    rr = lax.broadcasted_iota(jnp.int32, (ln, ln), 0)
    cc = lax.broadcasted_iota(jnp.int32, (ln, ln), 1)
    causal = rr >= cc
    tri = causal.astype(F32)
    cum_c = jnp.dot(tri, jnp.where(is_f_c, g_c, 0.0), preferred_element_type=F32,
                    precision=lax.Precision.HIGHEST)
    cum_r = lax.dot_general(jnp.where(is_f_r, g_r, 0.0), tri, (((1,), (1,)), ((), ())),
                            preferred_element_type=F32, precision=lax.Precision.HIGHEST)
    m_all = m_scr[...]
    lane = lax.broadcasted_iota(jnp.int32, (1, 128), 1)
    m_out = jnp.zeros((1, 128), F32)

    for h in range(H_C):
        qf = cblk[:, h * DK_C:(h + 1) * DK_C]
        kf = cblk[:, D_C + h * DK_C:D_C + (h + 1) * DK_C] * KC_SCALE
        vf = cblk[:, 2 * D_C + h * DV_C:2 * D_C + (h + 1) * DV_C]
        of = cblk[:, 3 * D_C + h * DV_C:3 * D_C + (h + 1) * DV_C]
        zf = cblk[:, 4 * D_C + h * DV_C:4 * D_C + (h + 1) * DV_C]
        qb, kb, vb = qf.astype(BF16), kf.astype(BF16), vf.astype(BF16)
        bc = cum_c[:, H_C + h:H_C + h + 1]
        igc = g_c[:, h:h + 1]
        br = cum_r[H_C + h:H_C + h + 1, :]
        igr = g_r[h:h + 1, :]
        m_prev = m_all[:, h:h + 1]
        c_mat = c_scr[h]
        n_vec = n_scr[h]

        log_d = jnp.where(causal, bc - br + igr, NEG)
        inter = bc + m_prev
        m_t = jnp.maximum(inter, jnp.max(log_d, axis=-1, keepdims=True))
        d_w = jnp.where(causal, jnp.exp(log_d - m_t), 0.0)
        i_w = jnp.exp(inter - m_t)
        qk = lax.dot_general(qb, kb, (((1,), (1,)), ((), ())), preferred_element_type=F32)
        w = d_w * qk
        num = (jnp.dot(w.astype(BF16), vb, preferred_element_type=F32)
               + i_w * jnp.dot(qb, c_mat.astype(BF16), preferred_element_type=F32))
        den = jnp.sum(w, axis=-1, keepdims=True) + i_w * jnp.sum(qf * n_vec, axis=-1, keepdims=True)
        hh = num / jnp.maximum(jnp.abs(den), jnp.exp(-m_t))

        b_last = bc[ln - 1:ln, :]
        m_last = m_t[ln - 1:ln, :]
        i_last = i_w[ln - 1:ln, :]
        w_last = jnp.exp(b_last - bc + igc - m_last)
        kw = w_last * kf
        c_scr[h] = i_last * c_mat + lax.dot_general(kw.astype(BF16), vb, (((0,), (0,)), ((), ())),
                                                    preferred_element_type=F32)
        n_scr[h] = i_last * n_vec + jnp.sum(kw, axis=0, keepdims=True)
        m_out = jnp.where(lane == h, m_last, m_out)

        hg = _sigmoid(of) * hh
        yh = hg * lax.rsqrt(jnp.mean(hg * hg, axis=-1, keepdims=True) + 1e-6)
        yh = yh * mg_ref[:, h * DV_C:(h + 1) * DV_C]
        y_ref[0, :, h * DV_C:(h + 1) * DV_C] = yh * _silu(zf)

    m_scr[...] = m_out

    @pl.when(t == n_t - 1)
    def _():
        cn_ref[0] = c_scr[...]
        nn_ref[0] = n_scr[...]
        mn_ref[0] = m_scr[...]


def _mlstm(c3d, gcol, grow, f_bias, mlstm_g, c0, n0, m0, ln):
    bsz, t_valid, _ = c3d.shape
    t_pad = -(-t_valid // ln) * ln
    if t_pad != t_valid:
        extra = t_pad - t_valid
        c3d = jnp.pad(c3d, ((0, 0), (0, extra), (0, 0)))
        gcol = jnp.pad(gcol, ((0, 0), (0, extra), (0, 0)))
        grow = jnp.pad(grow, ((0, 0), (0, 0), (0, extra)))
    m0p = jnp.pad(m0, ((0, 0), (0, 128 - H_C))).reshape(bsz, 1, 128)
    n0r = n0.reshape(bsz, H_C, 1, DK_C)
    fb8 = jnp.concatenate([jnp.zeros((H_C,), F32), f_bias])
    fixed = lambda b, t: (0, 0)
    y, cn, nn, mn = pl.pallas_call(
        functools.partial(_mlstm_body, ln=ln, t_valid=t_valid, t_pad=t_pad),
        grid=(bsz, t_pad // ln),
        in_specs=[pl.BlockSpec((1, ln, 5 * D_C), lambda b, t: (b, t, 0)),
                  pl.BlockSpec((1, ln, N_GATE), lambda b, t: (b, t, 0)),
                  pl.BlockSpec((1, N_GATE, ln), lambda b, t: (b, 0, t)),
                  pl.BlockSpec((1, N_GATE), fixed),
                  pl.BlockSpec((N_GATE, 1), fixed),
                  pl.BlockSpec((1, D_C), fixed),
                  pl.BlockSpec((1, H_C, DK_C, DV_C), lambda b, t: (b, 0, 0, 0)),
                  pl.BlockSpec((1, H_C, 1, DK_C), lambda b, t: (b, 0, 0, 0)),
                  pl.BlockSpec((1, 1, 128), lambda b, t: (b, 0, 0))],
        out_specs=[pl.BlockSpec((1, ln, D_C), lambda b, t: (b, t, 0)),
                   pl.BlockSpec((1, H_C, DK_C, DV_C), lambda b, t: (b, 0, 0, 0)),
                   pl.BlockSpec((1, H_C, 1, DK_C), lambda b, t: (b, 0, 0, 0)),
                   pl.BlockSpec((1, 1, 128), lambda b, t: (b, 0, 0))],
        out_shape=[jax.ShapeDtypeStruct((bsz, t_pad, D_C), F32),
                   jax.ShapeDtypeStruct((bsz, H_C, DK_C, DV_C), F32),
                   jax.ShapeDtypeStruct((bsz, H_C, 1, DK_C), F32),
                   jax.ShapeDtypeStruct((bsz, 1, 128), F32)],
        scratch_shapes=[pltpu.VMEM((H_C, DK_C, DV_C), F32), pltpu.VMEM((H_C, 1, DK_C), F32),
                        pltpu.VMEM((1, 128), F32)],
        compiler_params=_cparams(2),
        name="mlstm",
    )(c3d, gcol, grow, fb8.reshape(1, N_GATE), fb8.reshape(N_GATE, 1), mlstm_g.reshape(1, D_C),
      c0, n0r, m0p)
    return y[:, :t_valid], cn, nn.reshape(bsz, H_C, DK_C), mn[:, 0, :H_C]


def _outproj_body(x_ref, ya_ref, az_ref, yb_ref, yc_ref, pw_ref, w_ref, g_ref, o_ref):
    ya = jnp.dot(ya_ref[...].astype(BF16), pw_ref[...], preferred_element_type=F32)
    ya = ya * _silu(az_ref[...])
    out = jnp.dot(ya.astype(BF16), w_ref[0:C_A, :], preferred_element_type=F32)
    out = out + jnp.dot(yb_ref[...].astype(BF16), w_ref[C_A:C_A + D_B, :], preferred_element_type=F32)
    out = out + jnp.dot(yc_ref[...].astype(BF16), w_ref[C_A + D_B:, :], preferred_element_type=F32)
    y = out * lax.rsqrt(jnp.mean(out * out, axis=-1, keepdims=True) + 1e-6) * g_ref[...]
    o_ref[...] = x_ref[...] + y


def _outproj(x2d, ya, a2d, yb, yc, conv_pw, w_out, ln_post):
    rows, d = x2d.shape
    tm = _pick(rows, (256, 128))
    row = lambda i: (i, 0)
    fixed = lambda i: (0, 0)
    return pl.pallas_call(
        _outproj_body,
        grid=(rows // tm,),
        in_specs=[pl.BlockSpec((tm, d), row),
                  pl.BlockSpec((tm, C_A), row),
                  pl.BlockSpec((tm, C_A), lambda i: (i, 2)),
                  pl.BlockSpec((tm, D_B), row),
                  pl.BlockSpec((tm, D_C), row),
                  pl.BlockSpec((C_A, C_A), fixed),
                  pl.BlockSpec((C_A + D_B + D_C, d), fixed),
                  pl.BlockSpec((1, d), fixed)],
        out_specs=pl.BlockSpec((tm, d), row),
        out_shape=jax.ShapeDtypeStruct((rows, d), F32),
        compiler_params=_cparams(1),
        name="outproj",
    )(x2d, ya, a2d, yb, yc, conv_pw.astype(BF16), w_out.astype(BF16), ln_post.reshape(1, d))


def _layer(x, lidx, conv_state, mstate, attn, flash_operands, weights, mlstm_chunk):
    (ln_pre, w_in, b_in, conv_w, conv_b, conv_ln_g, conv_ln_b, conv_pw, lam_qk, attn_g, f_bias,
     mlstm_g, w_out, ln_post) = weights
    bsz, t_len, d = x.shape
    rows = bsz * t_len
    lam_init = 0.8 - 0.6 * math.exp(-0.3 * lidx)
    x2d = x.reshape(rows, d)
    a, q, k, v, z, c, gcol, grow, *flash = _inproj(x2d, ln_pre, w_in, b_in, flash_operands)

    ya, conv_new = _conv(a.reshape(bsz, t_len, 3 * C_A), conv_state, conv_w, conv_b, conv_ln_g,
                         conv_ln_b)
    yb = attn(q, k, v, z, lam_qk, attn_g, lam_init, *flash)
    grow3 = jnp.swapaxes(grow.reshape(N_GATE, bsz, t_len), 0, 1)
    yc, c_new, n_new, m_new = _mlstm(c.reshape(bsz, t_len, 5 * D_C), gcol.reshape(bsz, t_len, N_GATE),
                                     grow3, f_bias, mlstm_g, *mstate, mlstm_chunk)
    x_new = _outproj(x2d, ya.reshape(rows, C_A), a, yb.reshape(rows, D_B), yc.reshape(rows, D_C),
                     conv_pw, w_out, ln_post)
    return (x_new.reshape(bsz, t_len, d), k.reshape(bsz, t_len, H_B, 2 * DK_B),
            v.reshape(bsz, t_len, H_B, DV_B), conv_new, (c_new, n_new, m_new))


def kernel(x_prompt, x_sample, cache_k, cache_v, state_conv, state_C, state_n, state_m, page_table,
           ln_pre, w_in, b_in, conv_w, conv_b, conv_ln_g, conv_ln_b, conv_pw, lam_qk, attn_g, f_bias,
           mlstm_g, w_out, ln_post):
    depth = w_in.shape[0]
    bp, tp, _ = x_prompt.shape
    bs, ts, _ = x_sample.shape
    assert bp == 1, "prompt attention sweep is written for a single prompt sequence"
    n_pool, page = cache_k.shape[1], cache_k.shape[2]
    ck = cache_k.reshape(depth, n_pool, page * H_B, DV_B)
    cv = cache_v.reshape(depth, n_pool, page * H_B, DV_B)

    def weights(l):
        return (ln_pre[l], w_in[l], b_in[l], conv_w[l], conv_b[l], conv_ln_g[l], conv_ln_b[l],
                conv_pw[l], lam_qk[l], attn_g[l], f_bias[l], mlstm_g[l], w_out[l], ln_post[l])

    def prompt_attn(q, k, v, z, lq, g, lam_init, kb, vt):
        return _attn_prompt(q, kb, vt, z, lq, g, lam_init)

    prompt_chunk = _pick(tp, (128, 64, 32, 16, 8))
    xp = x_prompt
    outs_p = [[] for _ in range(6)]
    for l in range(depth):
        buf0 = jnp.zeros((bp, CONV_W - 1, C_A), F32)
        m0 = (jnp.zeros((bp, H_C, DK_C, DV_C), F32), jnp.zeros((bp, H_C, DK_C), F32),
              jnp.zeros((bp, H_C), F32))
        xp, kr, vr, cb, (cm, nv, mv) = _layer(xp, l, buf0, m0, prompt_attn, True, weights(l),
                                              prompt_chunk)
        for lst, val in zip(outs_p, (kr, vr, cb, cm, nv, mv)):
            lst.append(val)

    xs = x_sample
    outs_s = [[] for _ in range(6)]
    for l in range(depth):
        def sample_attn(q, k, v, z, lq, g, lam_init, l=l):
            return _attn_decode(q.reshape(bs, ts, D_B), k.reshape(bs, ts, D_B), v.reshape(bs, ts, D_B),
                                z.reshape(bs, ts, D_B), ck, cv, page_table, lq, g, lam_init, l)

        st = (state_C[l], state_n[l], state_m[l])
        xs, kr, vr, cb, (cm, nv, mv) = _layer(xs, l, state_conv[l], st, sample_attn, False,
                                              weights(l), 8)
        for lst, val in zip(outs_s, (kr, vr, cb, cm, nv, mv)):
            lst.append(val)

    return (xp, xs, *[jnp.stack(o) for o in outs_p], *[jnp.stack(o) for o in outs_s])
```

```python
import functools
import math

import jax
import jax.numpy as jnp
from jax import lax
from jax.experimental import pallas as pl
from jax.experimental.pallas import tpu as pltpu

F32 = jnp.float32
BF16 = jnp.bfloat16

C_A = 256
CONV_W = 31
H_B = 4
DK_B = 64
DV_B = 128
D_B = H_B * DV_B
H_C = 4
DK_C = 64
DV_C = 64
D_C = H_C * DV_C

OFF_A = 0
OFF_Q = 3 * C_A
OFF_K = OFF_Q + 2 * H_B * DK_B
OFF_V = OFF_K + 2 * H_B * DK_B
OFF_Z = OFF_V + D_B
OFF_C = OFF_Z + D_B
OFF_G = OFF_C + 5 * D_C
N_MAIN = OFF_G
N_GATE = 2 * H_C

HIST = 32
NEG = -1e30
QK_SCALE = DK_B ** -0.5
KC_SCALE = DK_C ** -0.5
LOG2E = math.log2(math.e)

VMEM_LIMIT_BYTES = 48 * 1024 * 1024


def _cparams(n_axes):
    return pltpu.CompilerParams(dimension_semantics=("arbitrary",) * n_axes,
                                vmem_limit_bytes=VMEM_LIMIT_BYTES)


def _pick(n, candidates):
    for c in candidates:
        if n % c == 0:
            return c
    return n


def _sigmoid(x):
    return 1.0 / (1.0 + jnp.exp(-x))


def _silu(x):
    return x * _sigmoid(x)


def _log_sigmoid(x):
    return jnp.minimum(x, 0.0) - jnp.log(1.0 + jnp.exp(-jnp.abs(x)))


def _inproj_body(x_ref, g_ref, w_ref, b_ref, wg_ref, wgt_ref, bgc_ref, bgr_ref, *rest, flash_operands):
    if flash_operands:
        wvt_ref, bvt_ref = rest[:2]
        rest = rest[2:]
    a_ref, q_ref, k_ref, v_ref, z_ref, c_ref, gc_ref, gr_ref = rest[:8]
    xf = x_ref[...]
    h = xf * lax.rsqrt(jnp.mean(xf * xf, axis=-1, keepdims=True) + 1e-6) * g_ref[...]
    hb = h.astype(BF16)
    rows_last = (((1,), (1,)), ((), ()))

    def seg(lo, hi):
        return jnp.dot(hb, w_ref[:, lo:hi], preferred_element_type=F32) + b_ref[:, lo:hi]

    a_ref[...] = seg(OFF_A, OFF_Q)
    q_scale = QK_SCALE * LOG2E if flash_operands else QK_SCALE
    q_ref[...] = (seg(OFF_Q, OFF_K) * q_scale).astype(BF16)
    kk = seg(OFF_K, OFF_V)
    vv = seg(OFF_V, OFF_Z)
    tm = kk.shape[0]
    for h in range(H_B):
        k_ref[pl.ds(h, tm, stride=H_B), :] = kk[:, h * DV_B:(h + 1) * DV_B]
        v_ref[pl.ds(h, tm, stride=H_B), :] = vv[:, h * DV_B:(h + 1) * DV_B]
    z_ref[...] = seg(OFF_Z, OFF_C)
    c_ref[...] = seg(OFF_C, OFF_G)
    gc_ref[...] = jnp.dot(hb, wg_ref[...], preferred_element_type=F32) + bgc_ref[...]
    gr_ref[...] = lax.dot_general(wgt_ref[...], hb, rows_last, preferred_element_type=F32) + bgr_ref[...]
    if flash_operands:
        kb_ref, vt_ref = rest[8:]
        kb_ref[...] = kk.astype(BF16)
        vt = lax.dot_general(wvt_ref[...], hb, rows_last, preferred_element_type=F32) + bvt_ref[...]
        vt_ref[...] = vt.astype(BF16)


def _inproj(x2d, ln_pre, w_in, b_in, flash_operands):
    rows, d = x2d.shape
    tm = _pick(rows, (256, 128))
    w_main = w_in[:, :N_MAIN].astype(BF16)
    w_gate = w_in[:, N_MAIN:].astype(BF16)
    b_main = b_in[:N_MAIN].reshape(1, N_MAIN)
    b_gate = b_in[N_MAIN:]
    row = lambda i: (i, 0)
    col = lambda i: (0, i)
    fixed = lambda i: (0, 0)
    head_rows = jax.ShapeDtypeStruct((rows * H_B, DV_B), F32)
    head_spec = pl.BlockSpec((tm * H_B, DV_B), row)
    out_shape = [jax.ShapeDtypeStruct((rows, OFF_Q - OFF_A), F32),
                 jax.ShapeDtypeStruct((rows, OFF_K - OFF_Q), BF16),
                 head_rows, head_rows,
                 jax.ShapeDtypeStruct((rows, OFF_C - OFF_Z), F32),
                 jax.ShapeDtypeStruct((rows, OFF_G - OFF_C), F32)]
    out_specs = [pl.BlockSpec((tm, OFF_Q - OFF_A), row), pl.BlockSpec((tm, OFF_K - OFF_Q), row),
                 head_spec, head_spec,
                 pl.BlockSpec((tm, OFF_C - OFF_Z), row), pl.BlockSpec((tm, OFF_G - OFF_C), row)]
    out_shape += [jax.ShapeDtypeStruct((rows, N_GATE), F32), jax.ShapeDtypeStruct((N_GATE, rows), F32)]
    out_specs += [pl.BlockSpec((tm, N_GATE), row), pl.BlockSpec((N_GATE, tm), col)]
    in_specs = [pl.BlockSpec((tm, d), row),
                pl.BlockSpec((1, d), fixed),
                pl.BlockSpec((d, N_MAIN), fixed),
                pl.BlockSpec((1, N_MAIN), fixed),
                pl.BlockSpec((d, N_GATE), fixed),
                pl.BlockSpec((N_GATE, d), fixed),
                pl.BlockSpec((1, N_GATE), fixed),
                pl.BlockSpec((N_GATE, 1), fixed)]
    args = [x2d, ln_pre.reshape(1, d), w_main, b_main, w_gate, w_gate.T,
            b_gate.reshape(1, N_GATE), b_gate.reshape(N_GATE, 1)]
    if flash_operands:
        in_specs += [pl.BlockSpec((D_B, d), fixed), pl.BlockSpec((D_B, 1), fixed)]
        args += [w_main[:, OFF_V:OFF_Z].T, b_in[OFF_V:OFF_Z].reshape(D_B, 1)]
        out_shape += [jax.ShapeDtypeStruct((rows, D_B), BF16), jax.ShapeDtypeStruct((D_B, rows), BF16)]
        out_specs += [pl.BlockSpec((tm, D_B), row), pl.BlockSpec((D_B, tm), col)]
    return pl.pallas_call(
        functools.partial(_inproj_body, flash_operands=flash_operands),
        grid=(rows // tm,),
        in_specs=in_specs,
        out_specs=out_specs,
        out_shape=out_shape,
        compiler_params=_cparams(1),
        name="inproj",
    )(*args)


def _conv_body(a_ref, st_ref, cw_ref, cb_ref, lg_ref, lb_ref, y_ref, hist_ref, fbuf, *, tt, rc):
    t = pl.program_id(1)

    @pl.when(t == 0)
    def _():
        fbuf[0:HIST, :] = st_ref[0]

    a = a_ref[0]
    fbuf[HIST:HIST + tt, :] = a[:, 0:C_A] * _sigmoid(a[:, C_A:2 * C_A])
    first = HIST - (CONV_W - 1)
    for r0 in range(0, tt, rc):
        acc = jnp.zeros((rc, C_A), F32)
        for j in range(CONV_W):
            acc = acc + cw_ref[j:j + 1, :] * fbuf[pl.ds(first + j + r0, rc), :]
        y = acc + cb_ref[...]
        mu = jnp.mean(y, axis=-1, keepdims=True)
        yc = y - mu
        yn = yc * lax.rsqrt(jnp.mean(yc * yc, axis=-1, keepdims=True) + 1e-5)
        y_ref[0, r0:r0 + rc, :] = _silu(yn * lg_ref[...] + lb_ref[...])
    new_hist = fbuf[pl.ds(tt, HIST), :]
    hist_ref[0] = new_hist
    fbuf[0:HIST, :] = new_hist


def _conv(a3d, state, conv_w, conv_b, ln_g, ln_b):
    bsz, t_len, _ = a3d.shape
    tt = _pick(t_len, (256, 128, 64, 32, 16, 8))
    rc = min(tt, 64)
    st = jnp.pad(state, ((0, 0), (HIST - (CONV_W - 1), 0), (0, 0)))
    fixed = lambda b, t: (0, 0)
    y, hist = pl.pallas_call(
        functools.partial(_conv_body, tt=tt, rc=rc),
        grid=(bsz, t_len // tt),
        in_specs=[pl.BlockSpec((1, tt, 3 * C_A), lambda b, t: (b, t, 0)),
                  pl.BlockSpec((1, HIST, C_A), lambda b, t: (b, 0, 0)),
                  pl.BlockSpec((CONV_W, C_A), fixed),
                  pl.BlockSpec((1, C_A), fixed),
                  pl.BlockSpec((1, C_A), fixed),
                  pl.BlockSpec((1, C_A), fixed)],
        out_specs=[pl.BlockSpec((1, tt, C_A), lambda b, t: (b, t, 0)),
                   pl.BlockSpec((1, HIST, C_A), lambda b, t: (b, 0, 0))],
        out_shape=[jax.ShapeDtypeStruct((bsz, t_len, C_A), F32),
                   jax.ShapeDtypeStruct((bsz, HIST, C_A), F32)],
        scratch_shapes=[pltpu.VMEM((HIST + max(tt, 8), C_A), F32)],
        compiler_params=_cparams(2),
        name="conv",
    )(a3d, st, conv_w, conv_b.reshape(1, C_A), ln_g.reshape(1, C_A), ln_b.reshape(1, C_A))
    return y, hist[:, HIST - (CONV_W - 1):, :]


def _lambda(lq, lam_init):
    s01 = jnp.sum(lq[0:1, :] * lq[1:2, :], axis=-1, keepdims=True)
    s23 = jnp.sum(lq[2:3, :] * lq[3:4, :], axis=-1, keepdims=True)
    return jnp.exp(s01) - jnp.exp(s23) + lam_init


def _attn_finish(o, g, z, lam_init):
    y = o * lax.rsqrt(jnp.mean(o * o, axis=-1, keepdims=True) + 1e-6) * g * (1.0 - lam_init)
    return y * _silu(z)


N_MAPS = 2 * H_B
SUM_ROWS = 16


def _attn_body(it_ref, jt_ref, q_ref, k_ref, vt_ref, z_ref, lq_ref, g_ref, o_ref,
               m_scr, acc_scr, *, blk, lam_init):
    step_id = pl.program_id(0)
    i = it_ref[step_id]
    j = jt_ref[step_id]

    @pl.when(j == 0)
    def _():
        m_scr[...] = jnp.full(m_scr.shape, NEG, F32)
        acc_scr[...] = jnp.zeros(acc_scr.shape, F32)

    def sweep(diagonal):
        if diagonal:
            kpos = lax.broadcasted_iota(jnp.int32, (blk, blk), 0)
            qpos = lax.broadcasted_iota(jnp.int32, (blk, blk), 1)
            keep = kpos <= qpos
        lane = lax.broadcasted_iota(jnp.int32, (blk, DV_B), 1)
        ones = jnp.ones((SUM_ROWS, blk), BF16)

        def scores(r):
            h, c = divmod(r, 2)
            hs = slice(h * DV_B, (h + 1) * DV_B)
            qh = q_ref[:, hs]
            qc = jnp.where((lane < DK_B) == (c == 0), qh, jnp.zeros_like(qh))
            st = lax.dot_general(k_ref[:, hs], qc, (((1,), (1,)), ((), ())),
                                 preferred_element_type=F32)
            if diagonal:
                st = jnp.where(keep, st, NEG)
            m_prev = m_scr[r:r + 1, :]
            m_new = jnp.maximum(m_prev, jnp.max(st, axis=0, keepdims=True))
            return st, m_prev, m_new

        def values(r, pt, alpha):
            h = r // 2
            vth = jnp.concatenate([vt_ref[h * DV_B:(h + 1) * DV_B, :], ones], axis=0)
            acc_scr[r] = alpha * acc_scr[r] + jnp.dot(vth, pt, preferred_element_type=F32)

        nxt = scores(0)
        pending = None
        for r in range(N_MAPS):
            st, m_prev, m_new = nxt
            if r + 1 < N_MAPS:
                nxt = scores(r + 1)
            alpha = jnp.exp2(m_prev - m_new)
            pt = jnp.exp2(st - m_new).astype(BF16)
            m_scr[r:r + 1, :] = m_new
            if pending is not None:
                values(*pending)
            pending = (r, pt, alpha)
        values(*pending)

    @pl.when(j < i)
    def _():
        sweep(False)

    @pl.when(j == i)
    def _():
        sweep(True)
        lam = _lambda(lq_ref[...], lam_init)
        for h in range(H_B):
            hs = slice(h * DV_B, (h + 1) * DV_B)
            r = 2 * h
            a1 = acc_scr[r, 0:DV_B, :] * (1.0 / acc_scr[r, DV_B:DV_B + 1, :])
            a2 = acc_scr[r + 1, 0:DV_B, :] * (1.0 / acc_scr[r + 1, DV_B:DV_B + 1, :])
            ot = a1 - lam * a2
            yt = ot * lax.rsqrt(jnp.mean(ot * ot, axis=0, keepdims=True) + 1e-6) * g_ref[...]
            o_ref[:, hs] = yt.T * (1.0 - lam_init) * _silu(z_ref[:, hs])


def _attn_prompt(q, kb, vt, z, lam_qk, attn_g, lam_init):
    t_len = q.shape[0]
    blk = _pick(t_len, (512, 256, 128))
    nb = t_len // blk
    pairs = [(i, j) for i in range(nb) for j in range(i + 1)]
    i_tab = jnp.asarray([p[0] for p in pairs], jnp.int32)
    j_tab = jnp.asarray([p[1] for p in pairs], jnp.int32)
    q_map = lambda s, it, jt: (it[s], 0)
    k_map = lambda s, it, jt: (jt[s], 0)
    vt_map = lambda s, it, jt: (0, jt[s])
    fixed = lambda s, it, jt: (0, 0)
    return pl.pallas_call(
        functools.partial(_attn_body, blk=blk, lam_init=lam_init),
        grid_spec=pltpu.PrefetchScalarGridSpec(
            num_scalar_prefetch=2,
            grid=(len(pairs),),
            in_specs=[pl.BlockSpec((blk, D_B), q_map),
                      pl.BlockSpec((blk, D_B), k_map),
                      pl.BlockSpec((D_B, blk), vt_map),
                      pl.BlockSpec((blk, D_B), q_map),
                      pl.BlockSpec((4, DK_B), fixed),
                      pl.BlockSpec((DV_B, 1), fixed)],
            out_specs=pl.BlockSpec((blk, D_B), q_map),
            scratch_shapes=[pltpu.VMEM((N_MAPS, blk), F32),
                            pltpu.VMEM((N_MAPS, DV_B + SUM_ROWS, blk), F32)]),
        out_shape=jax.ShapeDtypeStruct((t_len, D_B), F32),
        compiler_params=_cparams(1),
        name="attn_prompt",
    )(i_tab, j_tab, q, kb, vt, z, lam_qk, attn_g.reshape(DV_B, 1))


QROWS = 16


QR = H_B * QROWS


def _decode_body(pt_ref, q_ref, kn_ref, vn_ref, z_ref, lq_ref, g_ref, *rest,
                 n_pages_step, t_new, lam_init):
    k_refs = rest[:n_pages_step]
    v_refs = rest[n_pages_step:2 * n_pages_step]
    o_ref, m_scr, l_scr, acc_scr = rest[2 * n_pages_step:]
    g = pl.program_id(1)
    n_groups = pl.num_programs(1)

    @pl.when(g == 0)
    def _():
        m_scr[...] = jnp.full(m_scr.shape, NEG, F32)
        l_scr[...] = jnp.zeros(l_scr.shape, F32)
        acc_scr[...] = jnp.zeros(acc_scr.shape, F32)

    q64 = q_ref[0]
    row = lax.broadcasted_iota(jnp.int32, q64.shape, 0)
    lane = lax.broadcasted_iota(jnp.int32, q64.shape, 1)
    qd = jnp.where((lane < DK_B) == ((row % QROWS) < QROWS // 2), q64, jnp.zeros_like(q64))

    n_cols = k_refs[0].shape[0]
    col_head = lax.broadcasted_iota(jnp.int32, (QR, n_cols), 1) % H_B
    row_head = lax.broadcasted_iota(jnp.int32, (QR, n_cols), 0) // QROWS
    own_head = col_head == row_head
    s_list = []
    for kr in k_refs:
        s = lax.dot_general(qd, kr[...].astype(BF16), (((1,), (1,)), ((), ())),
                            preferred_element_type=F32)
        s_list.append(jnp.where(own_head, s, NEG))
    m_prev = m_scr[...]
    m_cur = s_list[0].max(axis=-1, keepdims=True)
    for s in s_list[1:]:
        m_cur = jnp.maximum(m_cur, s.max(axis=-1, keepdims=True))
    m_new = jnp.maximum(m_prev, m_cur)
    alpha = jnp.exp(m_prev - m_new)
    l_new = alpha * l_scr[...]
    acc = alpha * acc_scr[...]
    for s, vr in zip(s_list, v_refs):
        p = jnp.exp(s - m_new)
        l_new = l_new + jnp.sum(p, axis=-1, keepdims=True)
        acc = acc + jnp.dot(p.astype(BF16), vr[...].astype(BF16), preferred_element_type=F32)
    m_scr[...] = m_new
    l_scr[...] = l_new
    acc_scr[...] = acc

    @pl.when(g == n_groups - 1)
    def _():
        lam = _lambda(lq_ref[...], lam_init)
        qf = qd.astype(F32)
        tok = lax.broadcasted_iota(jnp.int32, (QR, 1), 0) % (QROWS // 2)
        s_new = []
        for jn in range(t_new):
            sj = jnp.sum(qf * kn_ref[0, jn].astype(BF16).astype(F32), axis=-1, keepdims=True)
            s_new.append(jnp.where(tok >= jn, sj, NEG))
        m_prev = m_scr[...]
        m_fin = m_prev
        for sj in s_new:
            m_fin = jnp.maximum(m_fin, sj)
        alpha = jnp.exp(m_prev - m_fin)
        l_fin = alpha * l_scr[...]
        acc = alpha * acc_scr[...]
        for jn, sj in enumerate(s_new):
            pj = jnp.exp(sj - m_fin)
            l_fin = l_fin + pj
            acc = acc + pj.astype(BF16).astype(F32) * vn_ref[0, jn].astype(BF16).astype(F32)
        a = acc / l_fin
        half = QROWS // 2
        for h in range(H_B):
            hs = slice(h * DV_B, (h + 1) * DV_B)
            o = a[h * QROWS:h * QROWS + half, :] - lam * a[h * QROWS + half:(h + 1) * QROWS, :]
            o_ref[0, :, hs] = _attn_finish(o, g_ref[...], z_ref[0, :, hs], lam_init)


def _attn_decode(q, k_new, v_new, z, cache_k_l, cache_v_l, page_table, lam_qk, attn_g, lam_init,
                 layer):
    bsz, t_new, _ = q.shape
    n_pages = page_table.shape[1]
    page_rows = cache_k_l.shape[2]
    pps = _pick(n_pages, (16, 8, 4, 2, 1))
    half = QROWS // 2
    qh = jnp.swapaxes(q.reshape(bsz, t_new, H_B, DV_B), 1, 2)
    q64 = jnp.zeros((bsz, H_B, QROWS, DV_B), BF16)
    q64 = q64.at[:, :, 0:t_new].set(qh).at[:, :, half:half + t_new].set(qh).reshape(bsz, QR, DV_B)
    on_rows = lambda x: jnp.repeat(x.reshape(bsz, t_new, H_B, DV_B), QROWS, axis=2)
    z8 = jnp.pad(z, ((0, 0), (0, half - t_new), (0, 0)))
    fixed = lambda b, g, pt: (0, 0)
    per_b = lambda b, g, pt: (b, 0, 0)
    per_b4 = lambda b, g, pt: (b, 0, 0, 0)

    def page_spec(jp):
        return pl.BlockSpec((None, None, page_rows, DV_B),
                            lambda b, g, pt: (layer, pt[b, g * pps + jp], 0, 0))

    in_specs = [pl.BlockSpec((1, QR, DV_B), per_b),
                pl.BlockSpec((1, t_new, QR, DV_B), per_b4),
                pl.BlockSpec((1, t_new, QR, DV_B), per_b4),
                pl.BlockSpec((1, half, D_B), per_b),
                pl.BlockSpec((4, DK_B), fixed),
                pl.BlockSpec((1, DV_B), fixed)]
    in_specs += [page_spec(jp) for jp in range(pps)] * 2
    out = pl.pallas_call(
        functools.partial(_decode_body, n_pages_step=pps, t_new=t_new, lam_init=lam_init),
        grid_spec=pltpu.PrefetchScalarGridSpec(
            num_scalar_prefetch=1,
            grid=(bsz, n_pages // pps),
            in_specs=in_specs,
            out_specs=pl.BlockSpec((1, half, D_B), per_b),
            scratch_shapes=[pltpu.VMEM((QR, 1), F32), pltpu.VMEM((QR, 1), F32),
                            pltpu.VMEM((QR, DV_B), F32)]),
        out_shape=jax.ShapeDtypeStruct((bsz, half, D_B), F32),
        compiler_params=_cparams(2),
        name="attn_decode",
    )(page_table, q64, on_rows(k_new), on_rows(v_new), z8, lam_qk, attn_g.reshape(1, DV_B),
      *([cache_k_l] * pps), *([cache_v_l] * pps))
    return out[:, :t_new, :]


def _mlstm_body(c_ref, gc_ref, gr_ref, fbc_ref, fbr_ref, mg_ref, c0_ref, n0_ref, m0_ref,
                y_ref, cn_ref, nn_ref, mn_ref, c_scr, n_scr, m_scr, *, ln, t_valid, t_pad):
    t = pl.program_id(1)
    n_t = pl.num_programs(1)

    @pl.when(t == 0)
    def _():
        c_scr[...] = c0_ref[0]
        n_scr[...] = n0_ref[0]
        m_scr[...] = m0_ref[0]

    cblk = c_ref[0]
    gcol = gc_ref[0]
    grow = gr_ref[0]
    is_f_c = lax.broadcasted_iota(jnp.int32, (ln, N_GATE), 1) >= H_C
    is_f_r = lax.broadcasted_iota(jnp.int32, (N_GATE, ln), 0) >= H_C
    g_c = jnp.where(is_f_c, _log_sigmoid(gcol + fbc_ref[...]), gcol)
    g_r = jnp.where(is_f_r, _log_sigmoid(grow + fbr_ref[...]), grow)
    if t_pad != t_valid:
        ok_c = (t * ln + lax.broadcasted_iota(jnp.int32, (ln, N_GATE), 0)) < t_valid
        ok_r = (t * ln + lax.broadcasted_iota(jnp.int32, (N_GATE, ln), 1)) < t_valid
        g_c = jnp.where(ok_c, g_c, jnp.where(is_f_c, 0.0, NEG))
        g_r = jnp.where(ok_r, g_r, jnp.where(is_f_r, 0.0, NEG))
    rr = lax.broadcasted_iota(jnp.int32, (ln, ln), 0)
    cc = lax.broadcasted_iota(jnp.int32, (ln, ln), 1)
    causal = rr >= cc
    tri = causal.astype(F32)
    cum_c = jnp.dot(tri, jnp.where(is_f_c, g_c, 0.0), preferred_element_type=F32,
                    precision=lax.Precision.HIGHEST)
    cum_r = lax.dot_general(jnp.where(is_f_r, g_r, 0.0), tri, (((1,), (1,)), ((), ())),
                            preferred_element_type=F32, precision=lax.Precision.HIGHEST)
    m_all = m_scr[...]
    lane = lax.broadcasted_iota(jnp.int32, (1, 128), 1)
    m_out = jnp.zeros((1, 128), F32)

    for h in range(H_C):
        qf = cblk[:, h * DK_C:(h + 1) * DK_C]
        kf = cblk[:, D_C + h * DK_C:D_C + (h + 1) * DK_C] * KC_SCALE
        vf = cblk[:, 2 * D_C + h * DV_C:2 * D_C + (h + 1) * DV_C]
        of = cblk[:, 3 * D_C + h * DV_C:3 * D_C + (h + 1) * DV_C]
        zf = cblk[:, 4 * D_C + h * DV_C:4 * D_C + (h + 1) * DV_C]
        qb, kb, vb = qf.astype(BF16), kf.astype(BF16), vf.astype(BF16)
        bc = cum_c[:, H_C + h:H_C + h + 1]
        igc = g_c[:, h:h + 1]
        br = cum_r[H_C + h:H_C + h + 1, :]
        igr = g_r[h:h + 1, :]
        m_prev = m_all[:, h:h + 1]
        c_mat = c_scr[h]
        n_vec = n_scr[h]

        log_d = jnp.where(causal, bc - br + igr, NEG)
        inter = bc + m_prev
        m_t = jnp.maximum(inter, jnp.max(log_d, axis=-1, keepdims=True))
        d_w = jnp.where(causal, jnp.exp(log_d - m_t), 0.0)
        i_w = jnp.exp(inter - m_t)
        qk = lax.dot_general(qb, kb, (((1,), (1,)), ((), ())), preferred_element_type=F32)
        w = d_w * qk
        num = (jnp.dot(w.astype(BF16), vb, preferred_element_type=F32)
               + i_w * jnp.dot(qb, c_mat.astype(BF16), preferred_element_type=F32))
        den = jnp.sum(w, axis=-1, keepdims=True) + i_w * jnp.sum(qf * n_vec, axis=-1, keepdims=True)
        hh = num / jnp.maximum(jnp.abs(den), jnp.exp(-m_t))

        b_last = bc[ln - 1:ln, :]
        m_last = m_t[ln - 1:ln, :]
        i_last = i_w[ln - 1:ln, :]
        w_last = jnp.exp(b_last - bc + igc - m_last)
        kw = w_last * kf
        c_scr[h] = i_last * c_mat + lax.dot_general(kw.astype(BF16), vb, (((0,), (0,)), ((), ())),
                                                    preferred_element_type=F32)
        n_scr[h] = i_last * n_vec + jnp.sum(kw, axis=0, keepdims=True)
        m_out = jnp.where(lane == h, m_last, m_out)

        hg = _sigmoid(of) * hh
        yh = hg * lax.rsqrt(jnp.mean(hg * hg, axis=-1, keepdims=True) + 1e-6)
        yh = yh * mg_ref[:, h * DV_C:(h + 1) * DV_C]
        y_ref[0, :, h * DV_C:(h + 1) * DV_C] = yh * _silu(zf)

    m_scr[...] = m_out

    @pl.when(t == n_t - 1)
    def _():
        cn_ref[0] = c_scr[...]
        nn_ref[0] = n_scr[...]
        mn_ref[0] = m_scr[...]


def _mlstm(c3d, gcol, grow, f_bias, mlstm_g, c0, n0, m0, ln):
    bsz, t_valid, _ = c3d.shape
    t_pad = -(-t_valid // ln) * ln
    if t_pad != t_valid:
        extra = t_pad - t_valid
        c3d = jnp.pad(c3d, ((0, 0), (0, extra), (0, 0)))
        gcol = jnp.pad(gcol, ((0, 0), (0, extra), (0, 0)))
        grow = jnp.pad(grow, ((0, 0), (0, 0), (0, extra)))
    m0p = jnp.pad(m0, ((0, 0), (0, 128 - H_C))).reshape(bsz, 1, 128)
    n0r = n0.reshape(bsz, H_C, 1, DK_C)
    fb8 = jnp.concatenate([jnp.zeros((H_C,), F32), f_bias])
    fixed = lambda b, t: (0, 0)
    y, cn, nn, mn = pl.pallas_call(
        functools.partial(_mlstm_body, ln=ln, t_valid=t_valid, t_pad=t_pad),
        grid=(bsz, t_pad // ln),
        in_specs=[pl.BlockSpec((1, ln, 5 * D_C), lambda b, t: (b, t, 0)),
                  pl.BlockSpec((1, ln, N_GATE), lambda b, t: (b, t, 0)),
                  pl.BlockSpec((1, N_GATE, ln), lambda b, t: (b, 0, t)),
                  pl.BlockSpec((1, N_GATE), fixed),
                  pl.BlockSpec((N_GATE, 1), fixed),
                  pl.BlockSpec((1, D_C), fixed),
                  pl.BlockSpec((1, H_C, DK_C, DV_C), lambda b, t: (b, 0, 0, 0)),
                  pl.BlockSpec((1, H_C, 1, DK_C), lambda b, t: (b, 0, 0, 0)),
                  pl.BlockSpec((1, 1, 128), lambda b, t: (b, 0, 0))],
        out_specs=[pl.BlockSpec((1, ln, D_C), lambda b, t: (b, t, 0)),
                   pl.BlockSpec((1, H_C, DK_C, DV_C), lambda b, t: (b, 0, 0, 0)),
                   pl.BlockSpec((1, H_C, 1, DK_C), lambda b, t: (b, 0, 0, 0)),
                   pl.BlockSpec((1, 1, 128), lambda b, t: (b, 0, 0))],
        out_shape=[jax.ShapeDtypeStruct((bsz, t_pad, D_C), F32),
                   jax.ShapeDtypeStruct((bsz, H_C, DK_C, DV_C), F32),
                   jax.ShapeDtypeStruct((bsz, H_C, 1, DK_C), F32),
                   jax.ShapeDtypeStruct((bsz, 1, 128), F32)],
        scratch_shapes=[pltpu.VMEM((H_C, DK_C, DV_C), F32), pltpu.VMEM((H_C, 1, DK_C), F32),
                        pltpu.VMEM((1, 128), F32)],
        compiler_params=_cparams(2),
        name="mlstm",
    )(c3d, gcol, grow, fb8.reshape(1, N_GATE), fb8.reshape(N_GATE, 1), mlstm_g.reshape(1, D_C),
      c0, n0r, m0p)
    return y[:, :t_valid], cn, nn.reshape(bsz, H_C, DK_C), mn[:, 0, :H_C]


def _outproj_body(x_ref, ya_ref, az_ref, yb_ref, yc_ref, pw_ref, w_ref, g_ref, o_ref):
    ya = jnp.dot(ya_ref[...].astype(BF16), pw_ref[...], preferred_element_type=F32)
    ya = ya * _silu(az_ref[...])
    out = jnp.dot(ya.astype(BF16), w_ref[0:C_A, :], preferred_element_type=F32)
    out = out + jnp.dot(yb_ref[...].astype(BF16), w_ref[C_A:C_A + D_B, :], preferred_element_type=F32)
    out = out + jnp.dot(yc_ref[...].astype(BF16), w_ref[C_A + D_B:, :], preferred_element_type=F32)
    y = out * lax.rsqrt(jnp.mean(out * out, axis=-1, keepdims=True) + 1e-6) * g_ref[...]
    o_ref[...] = x_ref[...] + y


def _outproj(x2d, ya, a2d, yb, yc, conv_pw, w_out, ln_post):
    rows, d = x2d.shape
    tm = _pick(rows, (256, 128))
    row = lambda i: (i, 0)
    fixed = lambda i: (0, 0)
    return pl.pallas_call(
        _outproj_body,
        grid=(rows // tm,),
        in_specs=[pl.BlockSpec((tm, d), row),
                  pl.BlockSpec((tm, C_A), row),
                  pl.BlockSpec((tm, C_A), lambda i: (i, 2)),
                  pl.BlockSpec((tm, D_B), row),
                  pl.BlockSpec((tm, D_C), row),
                  pl.BlockSpec((C_A, C_A), fixed),
                  pl.BlockSpec((C_A + D_B + D_C, d), fixed),
                  pl.BlockSpec((1, d), fixed)],
        out_specs=pl.BlockSpec((tm, d), row),
        out_shape=jax.ShapeDtypeStruct((rows, d), F32),
        compiler_params=_cparams(1),
        name="outproj",
    )(x2d, ya, a2d, yb, yc, conv_pw.astype(BF16), w_out.astype(BF16), ln_post.reshape(1, d))


def _layer(x, lidx, conv_state, mstate, attn, flash_operands, weights, mlstm_chunk):
    (ln_pre, w_in, b_in, conv_w, conv_b, conv_ln_g, conv_ln_b, conv_pw, lam_qk, attn_g, f_bias,
     mlstm_g, w_out, ln_post) = weights
    bsz, t_len, d = x.shape
    rows = bsz * t_len
    lam_init = 0.8 - 0.6 * math.exp(-0.3 * lidx)
    x2d = x.reshape(rows, d)
    a, q, k, v, z, c, gcol, grow, *flash = _inproj(x2d, ln_pre, w_in, b_in, flash_operands)

    ya, conv_new = _conv(a.reshape(bsz, t_len, 3 * C_A), conv_state, conv_w, conv_b, conv_ln_g,
                         conv_ln_b)
    yb = attn(q, k, v, z, lam_qk, attn_g, lam_init, *flash)
    grow3 = jnp.swapaxes(grow.reshape(N_GATE, bsz, t_len), 0, 1)
    yc, c_new, n_new, m_new = _mlstm(c.reshape(bsz, t_len, 5 * D_C), gcol.reshape(bsz, t_len, N_GATE),
                                     grow3, f_bias, mlstm_g, *mstate, mlstm_chunk)
    x_new = _outproj(x2d, ya.reshape(rows, C_A), a, yb.reshape(rows, D_B), yc.reshape(rows, D_C),
                     conv_pw, w_out, ln_post)
    return (x_new.reshape(bsz, t_len, d), k.reshape(bsz, t_len, H_B, 2 * DK_B),
            v.reshape(bsz, t_len, H_B, DV_B), conv_new, (c_new, n_new, m_new))


def kernel(x_prompt, x_sample, cache_k, cache_v, state_conv, state_C, state_n, state_m, page_table,
           ln_pre, w_in, b_in, conv_w, conv_b, conv_ln_g, conv_ln_b, conv_pw, lam_qk, attn_g, f_bias,
           mlstm_g, w_out, ln_post):
    depth = w_in.shape[0]
    bp, tp, _ = x_prompt.shape
    bs, ts, _ = x_sample.shape
    assert bp == 1, "prompt attention sweep is written for a single prompt sequence"
    n_pool, page = cache_k.shape[1], cache_k.shape[2]
    ck = cache_k.reshape(depth, n_pool, page * H_B, DV_B)
    cv = cache_v.reshape(depth, n_pool, page * H_B, DV_B)

    def weights(l):
        return (ln_pre[l], w_in[l], b_in[l], conv_w[l], conv_b[l], conv_ln_g[l], conv_ln_b[l],
                conv_pw[l], lam_qk[l], attn_g[l], f_bias[l], mlstm_g[l], w_out[l], ln_post[l])

    def prompt_attn(q, k, v, z, lq, g, lam_init, kb, vt):
        return _attn_prompt(q, kb, vt, z, lq, g, lam_init)

    prompt_chunk = _pick(tp, (128, 64, 32, 16, 8))
    xp = x_prompt
    outs_p = [[] for _ in range(6)]
    for l in range(depth):
        buf0 = jnp.zeros((bp, CONV_W - 1, C_A), F32)
        m0 = (jnp.zeros((bp, H_C, DK_C, DV_C), F32), jnp.zeros((bp, H_C, DK_C), F32),
              jnp.zeros((bp, H_C), F32))
        xp, kr, vr, cb, (cm, nv, mv) = _layer(xp, l, buf0, m0, prompt_attn, True, weights(l),
                                              prompt_chunk)
        for lst, val in zip(outs_p, (kr, vr, cb, cm, nv, mv)):
            lst.append(val)

    xs = x_sample
    outs_s = [[] for _ in range(6)]
    for l in range(depth):
        def sample_attn(q, k, v, z, lq, g, lam_init, l=l):
            return _attn_decode(q.reshape(bs, ts, D_B), k.reshape(bs, ts, D_B), v.reshape(bs, ts, D_B),
                                z.reshape(bs, ts, D_B), ck, cv, page_table, lq, g, lam_init, l)

        st = (state_C[l], state_n[l], state_m[l])
        xs, kr, vr, cb, (cm, nv, mv) = _layer(xs, l, state_conv[l], st, sample_attn, False,
                                              weights(l), 8)
        for lst, val in zip(outs_s, (kr, vr, cb, cm, nv, mv)):
            lst.append(val)

    return (xp, xs, *[jnp.stack(o) for o in outs_p], *[jnp.stack(o) for o in outs_s])
```

```python
import functools
import math

import jax
import jax.numpy as jnp
from jax import lax
from jax.experimental import pallas as pl
from jax.experimental.pallas import tpu as pltpu

F32 = jnp.float32
BF16 = jnp.bfloat16

C_A = 256
CONV_W = 31
H_B = 4
DK_B = 64
DV_B = 128
D_B = H_B * DV_B
H_C = 4
DK_C = 64
DV_C = 64
D_C = H_C * DV_C

OFF_A = 0
OFF_Q = 3 * C_A
OFF_K = OFF_Q + 2 * H_B * DK_B
OFF_V = OFF_K + 2 * H_B * DK_B
OFF_Z = OFF_V + D_B
OFF_C = OFF_Z + D_B
OFF_G = OFF_C + 5 * D_C
N_MAIN = OFF_G
N_GATE = 2 * H_C

HIST = 32
NEG = -1e30
QK_SCALE = DK_B ** -0.5
KC_SCALE = DK_C ** -0.5
LOG2E = math.log2(math.e)

VMEM_LIMIT_BYTES = 48 * 1024 * 1024


def _cparams(n_axes):
    return pltpu.CompilerParams(dimension_semantics=("arbitrary",) * n_axes,
                                vmem_limit_bytes=VMEM_LIMIT_BYTES)


def _pick(n, candidates):
    for c in candidates:
        if n % c == 0:
            return c
    return n


def _sigmoid(x):
    return 1.0 / (1.0 + jnp.exp(-x))


def _silu(x):
    return x * _sigmoid(x)


def _log_sigmoid(x):
    return jnp.minimum(x, 0.0) - jnp.log(1.0 + jnp.exp(-jnp.abs(x)))


def _inproj_body(x_ref, g_ref, w_ref, b_ref, wg_ref, wgt_ref, bgc_ref, bgr_ref, *rest, flash_operands):
    if flash_operands:
        wvt_ref, bvt_ref = rest[:2]
        rest = rest[2:]
    a_ref, q_ref, k_ref, v_ref, z_ref, c_ref, gc_ref, gr_ref = rest[:8]
    xf = x_ref[...]
    h = xf * lax.rsqrt(jnp.mean(xf * xf, axis=-1, keepdims=True) + 1e-6) * g_ref[...]
    hb = h.astype(BF16)
    rows_last = (((1,), (1,)), ((), ()))

    def seg(lo, hi):
        return jnp.dot(hb, w_ref[:, lo:hi], preferred_element_type=F32) + b_ref[:, lo:hi]

    a_ref[...] = seg(OFF_A, OFF_Q)
    q_scale = QK_SCALE * LOG2E if flash_operands else QK_SCALE
    q_ref[...] = (seg(OFF_Q, OFF_K) * q_scale).astype(BF16)
    kk = seg(OFF_K, OFF_V)
    vv = seg(OFF_V, OFF_Z)
    tm = kk.shape[0]
    for h in range(H_B):
        k_ref[pl.ds(h, tm, stride=H_B), :] = kk[:, h * DV_B:(h + 1) * DV_B]
        v_ref[pl.ds(h, tm, stride=H_B), :] = vv[:, h * DV_B:(h + 1) * DV_B]
    z_ref[...] = seg(OFF_Z, OFF_C)
    c_ref[...] = seg(OFF_C, OFF_G)
    gc_ref[...] = jnp.dot(hb, wg_ref[...], preferred_element_type=F32) + bgc_ref[...]
    gr_ref[...] = lax.dot_general(wgt_ref[...], hb, rows_last, preferred_element_type=F32) + bgr_ref[...]
    if flash_operands:
        kb_ref, vt_ref = rest[8:]
        kb_ref[...] = kk.astype(BF16)
        vt = lax.dot_general(wvt_ref[...], hb, rows_last, preferred_element_type=F32) + bvt_ref[...]
        vt_ref[...] = vt.astype(BF16)


def _inproj(x2d, ln_pre, w_in, b_in, flash_operands):
    rows, d = x2d.shape
    tm = _pick(rows, (256, 128))
    w_main = w_in[:, :N_MAIN].astype(BF16)
    w_gate = w_in[:, N_MAIN:].astype(BF16)
    b_main = b_in[:N_MAIN].reshape(1, N_MAIN)
    b_gate = b_in[N_MAIN:]
    row = lambda i: (i, 0)
    col = lambda i: (0, i)
    fixed = lambda i: (0, 0)
    head_rows = jax.ShapeDtypeStruct((rows * H_B, DV_B), F32)
    head_spec = pl.BlockSpec((tm * H_B, DV_B), row)
    out_shape = [jax.ShapeDtypeStruct((rows, OFF_Q - OFF_A), F32),
                 jax.ShapeDtypeStruct((rows, OFF_K - OFF_Q), BF16),
                 head_rows, head_rows,
                 jax.ShapeDtypeStruct((rows, OFF_C - OFF_Z), F32),
                 jax.ShapeDtypeStruct((rows, OFF_G - OFF_C), F32)]
    out_specs = [pl.BlockSpec((tm, OFF_Q - OFF_A), row), pl.BlockSpec((tm, OFF_K - OFF_Q), row),
                 head_spec, head_spec,
                 pl.BlockSpec((tm, OFF_C - OFF_Z), row), pl.BlockSpec((tm, OFF_G - OFF_C), row)]
    out_shape += [jax.ShapeDtypeStruct((rows, N_GATE), F32), jax.ShapeDtypeStruct((N_GATE, rows), F32)]
    out_specs += [pl.BlockSpec((tm, N_GATE), row), pl.BlockSpec((N_GATE, tm), col)]
    in_specs = [pl.BlockSpec((tm, d), row),
                pl.BlockSpec((1, d), fixed),
                pl.BlockSpec((d, N_MAIN), fixed),
                pl.BlockSpec((1, N_MAIN), fixed),
                pl.BlockSpec((d, N_GATE), fixed),
                pl.BlockSpec((N_GATE, d), fixed),
                pl.BlockSpec((1, N_GATE), fixed),
                pl.BlockSpec((N_GATE, 1), fixed)]
    args = [x2d, ln_pre.reshape(1, d), w_main, b_main, w_gate, w_gate.T,
            b_gate.reshape(1, N_GATE), b_gate.reshape(N_GATE, 1)]
    if flash_operands:
        in_specs += [pl.BlockSpec((D_B, d), fixed), pl.BlockSpec((D_B, 1), fixed)]
        args += [w_main[:, OFF_V:OFF_Z].T, b_in[OFF_V:OFF_Z].reshape(D_B, 1)]
        out_shape += [jax.ShapeDtypeStruct((rows, D_B), BF16), jax.ShapeDtypeStruct((D_B, rows), BF16)]
        out_specs += [pl.BlockSpec((tm, D_B), row), pl.BlockSpec((D_B, tm), col)]
    return pl.pallas_call(
        functools.partial(_inproj_body, flash_operands=flash_operands),
        grid=(rows // tm,),
        in_specs=in_specs,
        out_specs=out_specs,
        out_shape=out_shape,
        compiler_params=_cparams(1),
        name="inproj",
    )(*args)


SUBLANES = 8


def _conv_body(a_ref, st_ref, cw_ref, cb_ref, lg_ref, lb_ref, y_ref, hist_ref, fbuf, shifted,
               *, tt, rc):
    t = pl.program_id(1)

    @pl.when(t == 0)
    def _():
        fbuf[0:HIST, :] = st_ref[0]

    a = a_ref[0]
    fbuf[HIST:HIST + tt, :] = a[:, 0:C_A] * _sigmoid(a[:, C_A:2 * C_A])
    n_rows = shifted.shape[1]
    for k in range(1, SUBLANES):
        shifted[k] = fbuf[pl.ds(k, n_rows), :]
    first = HIST - (CONV_W - 1)
    for r0 in range(0, tt, rc):
        acc = jnp.zeros((rc, C_A), F32)
        for j in range(CONV_W):
            k = (first + j) % SUBLANES
            base = first + j - k + r0
            win = fbuf[base:base + rc, :] if k == 0 else shifted[k, base:base + rc, :]
            acc = acc + cw_ref[j:j + 1, :] * win
        y = acc + cb_ref[...]
        mu = jnp.mean(y, axis=-1, keepdims=True)
        yc = y - mu
        yn = yc * lax.rsqrt(jnp.mean(yc * yc, axis=-1, keepdims=True) + 1e-5)
        y_ref[0, r0:r0 + rc, :] = _silu(yn * lg_ref[...] + lb_ref[...]).astype(y_ref.dtype)
    new_hist = fbuf[pl.ds(tt, HIST), :]
    hist_ref[0] = new_hist
    fbuf[0:HIST, :] = new_hist


def _conv(a3d, state, conv_w, conv_b, ln_g, ln_b):
    bsz, t_len, _ = a3d.shape
    tt = _pick(t_len, (256, 128, 64, 32, 16, 8))
    rc = min(tt, 64)
    buf_rows = HIST + max(tt, SUBLANES)
    st = jnp.pad(state, ((0, 0), (HIST - (CONV_W - 1), 0), (0, 0)))
    fixed = lambda b, t: (0, 0)
    y, hist = pl.pallas_call(
        functools.partial(_conv_body, tt=tt, rc=rc),
        grid=(bsz, t_len // tt),
        in_specs=[pl.BlockSpec((1, tt, 3 * C_A), lambda b, t: (b, t, 0)),
                  pl.BlockSpec((1, HIST, C_A), lambda b, t: (b, 0, 0)),
                  pl.BlockSpec((CONV_W, C_A), fixed),
                  pl.BlockSpec((1, C_A), fixed),
                  pl.BlockSpec((1, C_A), fixed),
                  pl.BlockSpec((1, C_A), fixed)],
        out_specs=[pl.BlockSpec((1, tt, C_A), lambda b, t: (b, t, 0)),
                   pl.BlockSpec((1, HIST, C_A), lambda b, t: (b, 0, 0))],
        out_shape=[jax.ShapeDtypeStruct((bsz, t_len, C_A), BF16),
                   jax.ShapeDtypeStruct((bsz, HIST, C_A), F32)],
        scratch_shapes=[pltpu.VMEM((buf_rows, C_A), F32),
                        pltpu.VMEM((SUBLANES, buf_rows - SUBLANES, C_A), F32)],
        compiler_params=_cparams(2),
        name="conv",
    )(a3d, st, conv_w, conv_b.reshape(1, C_A), ln_g.reshape(1, C_A), ln_b.reshape(1, C_A))
    return y, hist[:, HIST - (CONV_W - 1):, :]


def _lambda(lq, lam_init):
    s01 = jnp.sum(lq[0:1, :] * lq[1:2, :], axis=-1, keepdims=True)
    s23 = jnp.sum(lq[2:3, :] * lq[3:4, :], axis=-1, keepdims=True)
    return jnp.exp(s01) - jnp.exp(s23) + lam_init


def _attn_finish(o, g, z, lam_init):
    y = o * lax.rsqrt(jnp.mean(o * o, axis=-1, keepdims=True) + 1e-6) * g * (1.0 - lam_init)
    return y * _silu(z)


N_MAPS = 2 * H_B
SUM_ROWS = 16


def _attn_body(it_ref, jt_ref, q_ref, k_ref, vt_ref, z_ref, lq_ref, g_ref, o_ref,
               m_scr, acc_scr, *, blk, lam_init):
    step_id = pl.program_id(0)
    i = it_ref[step_id]
    j = jt_ref[step_id]

    @pl.when(j == 0)
    def _():
        m_scr[...] = jnp.full(m_scr.shape, NEG, F32)
        acc_scr[...] = jnp.zeros(acc_scr.shape, F32)

    def sweep(diagonal):
        if diagonal:
            kpos = lax.broadcasted_iota(jnp.int32, (blk, blk), 0)
            qpos = lax.broadcasted_iota(jnp.int32, (blk, blk), 1)
            keep = kpos <= qpos
        lane = lax.broadcasted_iota(jnp.int32, (blk, DV_B), 1)
        ones = jnp.ones((SUM_ROWS, blk), BF16)

        def scores(r):
            h, c = divmod(r, 2)
            hs = slice(h * DV_B, (h + 1) * DV_B)
            qh = q_ref[:, hs]
            qc = jnp.where((lane < DK_B) == (c == 0), qh, jnp.zeros_like(qh))
            st = lax.dot_general(k_ref[:, hs], qc, (((1,), (1,)), ((), ())),
                                 preferred_element_type=F32)
            if diagonal:
                st = jnp.where(keep, st, NEG)
            m_prev = m_scr[r:r + 1, :]
            m_new = jnp.maximum(m_prev, jnp.max(st, axis=0, keepdims=True))
            return st, m_prev, m_new

        def values(r, pt, alpha):
            h = r // 2
            vth = jnp.concatenate([vt_ref[h * DV_B:(h + 1) * DV_B, :], ones], axis=0)
            acc_scr[r] = alpha * acc_scr[r] + jnp.dot(vth, pt, preferred_element_type=F32)

        nxt = scores(0)
        pending = None
        for r in range(N_MAPS):
            st, m_prev, m_new = nxt
            if r + 1 < N_MAPS:
                nxt = scores(r + 1)
            alpha = jnp.exp2(m_prev - m_new)
            pt = jnp.exp2(st - m_new).astype(BF16)
            m_scr[r:r + 1, :] = m_new
            if pending is not None:
                values(*pending)
            pending = (r, pt, alpha)
        values(*pending)

    @pl.when(j < i)
    def _():
        sweep(False)

    @pl.when(j == i)
    def _():
        sweep(True)
        lam = _lambda(lq_ref[...], lam_init)
        for h in range(H_B):
            hs = slice(h * DV_B, (h + 1) * DV_B)
            r = 2 * h
            a1 = acc_scr[r, 0:DV_B, :] * (1.0 / acc_scr[r, DV_B:DV_B + 1, :])
            a2 = acc_scr[r + 1, 0:DV_B, :] * (1.0 / acc_scr[r + 1, DV_B:DV_B + 1, :])
            ot = a1 - lam * a2
            yt = ot * lax.rsqrt(jnp.mean(ot * ot, axis=0, keepdims=True) + 1e-6) * g_ref[...]
            o_ref[:, hs] = (yt.T * (1.0 - lam_init) * _silu(z_ref[:, hs])).astype(o_ref.dtype)


def _attn_prompt(q, kb, vt, z, lam_qk, attn_g, lam_init):
    t_len = q.shape[0]
    blk = _pick(t_len, (512, 256, 128))
    nb = t_len // blk
    pairs = [(i, j) for i in range(nb) for j in range(i + 1)]
    i_tab = jnp.asarray([p[0] for p in pairs], jnp.int32)
    j_tab = jnp.asarray([p[1] for p in pairs], jnp.int32)
    q_map = lambda s, it, jt: (it[s], 0)
    k_map = lambda s, it, jt: (jt[s], 0)
    vt_map = lambda s, it, jt: (0, jt[s])
    fixed = lambda s, it, jt: (0, 0)
    return pl.pallas_call(
        functools.partial(_attn_body, blk=blk, lam_init=lam_init),
        grid_spec=pltpu.PrefetchScalarGridSpec(
            num_scalar_prefetch=2,
            grid=(len(pairs),),
            in_specs=[pl.BlockSpec((blk, D_B), q_map),
                      pl.BlockSpec((blk, D_B), k_map),
                      pl.BlockSpec((D_B, blk), vt_map),
                      pl.BlockSpec((blk, D_B), q_map),
                      pl.BlockSpec((4, DK_B), fixed),
                      pl.BlockSpec((DV_B, 1), fixed)],
            out_specs=pl.BlockSpec((blk, D_B), q_map),
            scratch_shapes=[pltpu.VMEM((N_MAPS, blk), F32),
                            pltpu.VMEM((N_MAPS, DV_B + SUM_ROWS, blk), F32)]),
        out_shape=jax.ShapeDtypeStruct((t_len, D_B), BF16),
        compiler_params=_cparams(1),
        name="attn_prompt",
    )(i_tab, j_tab, q, kb, vt, z, lam_qk, attn_g.reshape(DV_B, 1))


QROWS = 16


QR = H_B * QROWS


def _decode_body(pt_ref, q_ref, kn_ref, vn_ref, z_ref, lq_ref, g_ref, *rest,
                 n_pages_step, t_new, lam_init):
    k_refs = rest[:n_pages_step]
    v_refs = rest[n_pages_step:2 * n_pages_step]
    o_ref, m_scr, l_scr, acc_scr = rest[2 * n_pages_step:]
    g = pl.program_id(1)
    n_groups = pl.num_programs(1)

    @pl.when(g == 0)
    def _():
        m_scr[...] = jnp.full(m_scr.shape, NEG, F32)
        l_scr[...] = jnp.zeros(l_scr.shape, F32)
        acc_scr[...] = jnp.zeros(acc_scr.shape, F32)

    q64 = q_ref[0]
    row = lax.broadcasted_iota(jnp.int32, q64.shape, 0)
    lane = lax.broadcasted_iota(jnp.int32, q64.shape, 1)
    qd = jnp.where((lane < DK_B) == ((row % QROWS) < QROWS // 2), q64, jnp.zeros_like(q64))

    page = k_refs[0].shape[0] // H_B

    def head_rows(refs, h):
        return jnp.concatenate([r[pl.ds(h, page, stride=H_B), :] for r in refs], axis=0).astype(BF16)

    heads = range(H_B)
    rows = [slice(h * QROWS, (h + 1) * QROWS) for h in heads]
    s = [lax.dot_general(qd[rows[h], :], head_rows(k_refs, h), (((1,), (1,)), ((), ())),
                         preferred_element_type=F32) for h in heads]
    m_prev = [m_scr[rows[h], :] for h in heads]
    m_new = [jnp.maximum(m_prev[h], jnp.max(s[h], axis=-1, keepdims=True)) for h in heads]
    alpha = [jnp.exp(m_prev[h] - m_new[h]) for h in heads]
    p = [jnp.exp(s[h] - m_new[h]) for h in heads]
    pv = [jnp.dot(p[h].astype(BF16), head_rows(v_refs, h), preferred_element_type=F32) for h in heads]
    for h in heads:
        l_scr[rows[h], :] = alpha[h] * l_scr[rows[h], :] + jnp.sum(p[h], axis=-1, keepdims=True)
        acc_scr[rows[h], :] = alpha[h] * acc_scr[rows[h], :] + pv[h]
        m_scr[rows[h], :] = m_new[h]

    @pl.when(g == n_groups - 1)
    def _():
        lam = _lambda(lq_ref[...], lam_init)
        qf = qd.astype(F32)
        tok = lax.broadcasted_iota(jnp.int32, (QR, 1), 0) % (QROWS // 2)
        s_new = []
        for jn in range(t_new):
            sj = jnp.sum(qf * kn_ref[0, jn].astype(BF16).astype(F32), axis=-1, keepdims=True)
            s_new.append(jnp.where(tok >= jn, sj, NEG))
        m_prev = m_scr[...]
        m_fin = m_prev
        for sj in s_new:
            m_fin = jnp.maximum(m_fin, sj)
        alpha = jnp.exp(m_prev - m_fin)
        l_fin = alpha * l_scr[...]
        acc = alpha * acc_scr[...]
        for jn, sj in enumerate(s_new):
            pj = jnp.exp(sj - m_fin)
            l_fin = l_fin + pj
            acc = acc + pj.astype(BF16).astype(F32) * vn_ref[0, jn].astype(BF16).astype(F32)
        a = acc / l_fin
        half = QROWS // 2
        for h in range(H_B):
            hs = slice(h * DV_B, (h + 1) * DV_B)
            o = a[h * QROWS:h * QROWS + half, :] - lam * a[h * QROWS + half:(h + 1) * QROWS, :]
            o_ref[0, :, hs] = _attn_finish(o, g_ref[...], z_ref[0, :, hs], lam_init)


def _attn_decode(q, k_new, v_new, z, cache_k_l, cache_v_l, page_table, lam_qk, attn_g, lam_init,
                 layer):
    bsz, t_new, _ = q.shape
    n_pages = page_table.shape[1]
    page_rows = cache_k_l.shape[2]
    pps = _pick(n_pages, (16, 8, 4, 2, 1))
    half = QROWS // 2
    qh = jnp.swapaxes(q.reshape(bsz, t_new, H_B, DV_B), 1, 2)
    q64 = jnp.zeros((bsz, H_B, QROWS, DV_B), BF16)
    q64 = q64.at[:, :, 0:t_new].set(qh).at[:, :, half:half + t_new].set(qh).reshape(bsz, QR, DV_B)
    on_rows = lambda x: jnp.repeat(x.reshape(bsz, t_new, H_B, DV_B), QROWS, axis=2)
    z8 = jnp.pad(z, ((0, 0), (0, half - t_new), (0, 0)))
    fixed = lambda b, g, pt: (0, 0)
    per_b = lambda b, g, pt: (b, 0, 0)
    per_b4 = lambda b, g, pt: (b, 0, 0, 0)

    def page_spec(jp):
        return pl.BlockSpec((None, None, page_rows, DV_B),
                            lambda b, g, pt: (layer, pt[b, g * pps + jp], 0, 0))

    in_specs = [pl.BlockSpec((1, QR, DV_B), per_b),
                pl.BlockSpec((1, t_new, QR, DV_B), per_b4),
                pl.BlockSpec((1, t_new, QR, DV_B), per_b4),
                pl.BlockSpec((1, half, D_B), per_b),
                pl.BlockSpec((4, DK_B), fixed),
                pl.BlockSpec((1, DV_B), fixed)]
    in_specs += [page_spec(jp) for jp in range(pps)] * 2
    out = pl.pallas_call(
        functools.partial(_decode_body, n_pages_step=pps, t_new=t_new, lam_init=lam_init),
        grid_spec=pltpu.PrefetchScalarGridSpec(
            num_scalar_prefetch=1,
            grid=(bsz, n_pages // pps),
            in_specs=in_specs,
            out_specs=pl.BlockSpec((1, half, D_B), per_b),
            scratch_shapes=[pltpu.VMEM((QR, 1), F32), pltpu.VMEM((QR, 1), F32),
                            pltpu.VMEM((QR, DV_B), F32)]),
        out_shape=jax.ShapeDtypeStruct((bsz, half, D_B), F32),
        compiler_params=_cparams(2),
        name="attn_decode",
    )(page_table, q64, on_rows(k_new), on_rows(v_new), z8, lam_qk, attn_g.reshape(1, DV_B),
      *([cache_k_l] * pps), *([cache_v_l] * pps))
    return out[:, :t_new, :]


def _mlstm_body(c_ref, gc_ref, gr_ref, fbc_ref, fbr_ref, mg_ref, c0_ref, n0_ref, m0_ref,
                y_ref, cn_ref, nn_ref, mn_ref, c_scr, n_scr, m_scr, *, ln, t_valid, t_pad):
    t = pl.program_id(1)
    n_t = pl.num_programs(1)

    @pl.when(t == 0)
    def _():
        c_scr[...] = c0_ref[0]
        n_scr[...] = n0_ref[0]
        m_scr[...] = m0_ref[0]

    cblk = c_ref[0]
    gcol = gc_ref[0]
    grow = gr_ref[0]
    is_f_c = lax.broadcasted_iota(jnp.int32, (ln, N_GATE), 1) >= H_C
    is_f_r = lax.broadcasted_iota(jnp.int32, (N_GATE, ln), 0) >= H_C
    g_c = jnp.where(is_f_c, _log_sigmoid(gcol + fbc_ref[...]), gcol)
    g_r = jnp.where(is_f_r, _log_sigmoid(grow + fbr_ref[...]), grow)
    if t_pad != t_valid:
        ok_c = (t * ln + lax.broadcasted_iota(jnp.int32, (ln, N_GATE), 0)) < t_valid
        ok_r = (t * ln + lax.broadcasted_iota(jnp.int32, (N_GATE, ln), 1)) < t_valid
        g_c = jnp.where(ok_c, g_c, jnp.where(is_f_c, 0.0, NEG))
        g_r = jnp.where(ok_r, g_r, jnp.where(is_f_r, 0.0, NEG))
    rr = lax.broadcasted_iota(jnp.int32, (ln, ln), 0)
    cc = lax.broadcasted_iota(jnp.int32, (ln, ln), 1)
    causal = rr >= cc
    tri = causal.astype(F32)
    cum_c = jnp.dot(tri, jnp.where(is_f_c, g_c, 0.0), preferred_element_type=F32,
                    precision=lax.Precision.HIGHEST)
    cum_r = lax.dot_general(jnp.where(is_f_r, g_r, 0.0), tri, (((1,), (1,)), ((), ())),
                            preferred_element_type=F32, precision=lax.Precision.HIGHEST)
    m_all = m_scr[...]
    lane = lax.broadcasted_iota(jnp.int32, (1, 128), 1)
    m_out = jnp.zeros((1, 128), F32)

    for h in range(H_C):
        qf = cblk[:, h * DK_C:(h + 1) * DK_C]
        kf = cblk[:, D_C + h * DK_C:D_C + (h + 1) * DK_C] * KC_SCALE
        vf = cblk[:, 2 * D_C + h * DV_C:2 * D_C + (h + 1) * DV_C]
        of = cblk[:, 3 * D_C + h * DV_C:3 * D_C + (h + 1) * DV_C]
        zf = cblk[:, 4 * D_C + h * DV_C:4 * D_C + (h + 1) * DV_C]
        qb, kb, vb = qf.astype(BF16), kf.astype(BF16), vf.astype(BF16)
        bc = cum_c[:, H_C + h:H_C + h + 1]
        igc = g_c[:, h:h + 1]
        br = cum_r[H_C + h:H_C + h + 1, :]
        igr = g_r[h:h + 1, :]
        m_prev = m_all[:, h:h + 1]
        c_mat = c_scr[h]
        n_vec = n_scr[h]

        log_d = jnp.where(causal, bc - br + igr, NEG)
        inter = bc + m_prev
        m_t = jnp.maximum(inter, jnp.max(log_d, axis=-1, keepdims=True))
        d_w = jnp.where(causal, jnp.exp(log_d - m_t), 0.0)
        i_w = jnp.exp(inter - m_t)
        qk = lax.dot_general(qb, kb, (((1,), (1,)), ((), ())), preferred_element_type=F32)
        w = d_w * qk
        num = (jnp.dot(w.astype(BF16), vb, preferred_element_type=F32)
               + i_w * jnp.dot(qb, c_mat.astype(BF16), preferred_element_type=F32))
        den = jnp.sum(w, axis=-1, keepdims=True) + i_w * jnp.sum(qf * n_vec, axis=-1, keepdims=True)
        hh = num / jnp.maximum(jnp.abs(den), jnp.exp(-m_t))

        b_last = bc[ln - 1:ln, :]
        m_last = m_t[ln - 1:ln, :]
        i_last = i_w[ln - 1:ln, :]
        w_last = jnp.exp(b_last - bc + igc - m_last)
        kw = w_last * kf
        c_scr[h] = i_last * c_mat + lax.dot_general(kw.astype(BF16), vb, (((0,), (0,)), ((), ())),
                                                    preferred_element_type=F32)
        n_scr[h] = i_last * n_vec + jnp.sum(kw, axis=0, keepdims=True)
        m_out = jnp.where(lane == h, m_last, m_out)

        hg = _sigmoid(of) * hh
        yh = hg * lax.rsqrt(jnp.mean(hg * hg, axis=-1, keepdims=True) + 1e-6)
        yh = yh * mg_ref[:, h * DV_C:(h + 1) * DV_C]
        y_ref[0, :, h * DV_C:(h + 1) * DV_C] = (yh * _silu(zf)).astype(y_ref.dtype)

    m_scr[...] = m_out

    @pl.when(t == n_t - 1)
    def _():
        cn_ref[0] = c_scr[...]
        nn_ref[0] = n_scr[...]
        mn_ref[0] = m_scr[...]


def _mlstm(c3d, gcol, grow, f_bias, mlstm_g, c0, n0, m0, ln):
    bsz, t_valid, _ = c3d.shape
    t_pad = -(-t_valid // ln) * ln
    if t_pad != t_valid:
        extra = t_pad - t_valid
        c3d = jnp.pad(c3d, ((0, 0), (0, extra), (0, 0)))
        gcol = jnp.pad(gcol, ((0, 0), (0, extra), (0, 0)))
        grow = jnp.pad(grow, ((0, 0), (0, 0), (0, extra)))
    m0p = jnp.pad(m0, ((0, 0), (0, 128 - H_C))).reshape(bsz, 1, 128)
    n0r = n0.reshape(bsz, H_C, 1, DK_C)
    fb8 = jnp.concatenate([jnp.zeros((H_C,), F32), f_bias])
    fixed = lambda b, t: (0, 0)
    y, cn, nn, mn = pl.pallas_call(
        functools.partial(_mlstm_body, ln=ln, t_valid=t_valid, t_pad=t_pad),
        grid=(bsz, t_pad // ln),
        in_specs=[pl.BlockSpec((1, ln, 5 * D_C), lambda b, t: (b, t, 0)),
                  pl.BlockSpec((1, ln, N_GATE), lambda b, t: (b, t, 0)),
                  pl.BlockSpec((1, N_GATE, ln), lambda b, t: (b, 0, t)),
                  pl.BlockSpec((1, N_GATE), fixed),
                  pl.BlockSpec((N_GATE, 1), fixed),
                  pl.BlockSpec((1, D_C), fixed),
                  pl.BlockSpec((1, H_C, DK_C, DV_C), lambda b, t: (b, 0, 0, 0)),
                  pl.BlockSpec((1, H_C, 1, DK_C), lambda b, t: (b, 0, 0, 0)),
                  pl.BlockSpec((1, 1, 128), lambda b, t: (b, 0, 0))],
        out_specs=[pl.BlockSpec((1, ln, D_C), lambda b, t: (b, t, 0)),
                   pl.BlockSpec((1, H_C, DK_C, DV_C), lambda b, t: (b, 0, 0, 0)),
                   pl.BlockSpec((1, H_C, 1, DK_C), lambda b, t: (b, 0, 0, 0)),
                   pl.BlockSpec((1, 1, 128), lambda b, t: (b, 0, 0))],
        out_shape=[jax.ShapeDtypeStruct((bsz, t_pad, D_C), BF16),
                   jax.ShapeDtypeStruct((bsz, H_C, DK_C, DV_C), F32),
                   jax.ShapeDtypeStruct((bsz, H_C, 1, DK_C), F32),
                   jax.ShapeDtypeStruct((bsz, 1, 128), F32)],
        scratch_shapes=[pltpu.VMEM((H_C, DK_C, DV_C), F32), pltpu.VMEM((H_C, 1, DK_C), F32),
                        pltpu.VMEM((1, 128), F32)],
        compiler_params=_cparams(2),
        name="mlstm",
    )(c3d, gcol, grow, fb8.reshape(1, N_GATE), fb8.reshape(N_GATE, 1), mlstm_g.reshape(1, D_C),
      c0, n0r, m0p)
    return y[:, :t_valid], cn, nn.reshape(bsz, H_C, DK_C), mn[:, 0, :H_C]


def _outproj_body(x_ref, ya_ref, az_ref, yb_ref, yc_ref, pw_ref, w_ref, g_ref, o_ref):
    ya = jnp.dot(ya_ref[...].astype(BF16), pw_ref[...], preferred_element_type=F32)
    ya = ya * _silu(az_ref[...])
    out = jnp.dot(ya.astype(BF16), w_ref[0:C_A, :], preferred_element_type=F32)
    out = out + jnp.dot(yb_ref[...].astype(BF16), w_ref[C_A:C_A + D_B, :], preferred_element_type=F32)
    out = out + jnp.dot(yc_ref[...].astype(BF16), w_ref[C_A + D_B:, :], preferred_element_type=F32)
    y = out * lax.rsqrt(jnp.mean(out * out, axis=-1, keepdims=True) + 1e-6) * g_ref[...]
    o_ref[...] = x_ref[...] + y


def _outproj(x2d, ya, a2d, yb, yc, conv_pw, w_out, ln_post):
    rows, d = x2d.shape
    tm = _pick(rows, (256, 128))
    row = lambda i: (i, 0)
    fixed = lambda i: (0, 0)
    return pl.pallas_call(
        _outproj_body,
        grid=(rows // tm,),
        in_specs=[pl.BlockSpec((tm, d), row),
                  pl.BlockSpec((tm, C_A), row),
                  pl.BlockSpec((tm, C_A), lambda i: (i, 2)),
                  pl.BlockSpec((tm, D_B), row),
                  pl.BlockSpec((tm, D_C), row),
                  pl.BlockSpec((C_A, C_A), fixed),
                  pl.BlockSpec((C_A + D_B + D_C, d), fixed),
                  pl.BlockSpec((1, d), fixed)],
        out_specs=pl.BlockSpec((tm, d), row),
        out_shape=jax.ShapeDtypeStruct((rows, d), F32),
        compiler_params=_cparams(1),
        name="outproj",
    )(x2d, ya, a2d, yb, yc, conv_pw.astype(BF16), w_out.astype(BF16), ln_post.reshape(1, d))


def _layer(x, lidx, conv_state, mstate, attn, flash_operands, weights, mlstm_chunk):
    (ln_pre, w_in, b_in, conv_w, conv_b, conv_ln_g, conv_ln_b, conv_pw, lam_qk, attn_g, f_bias,
     mlstm_g, w_out, ln_post) = weights
    bsz, t_len, d = x.shape
    rows = bsz * t_len
    lam_init = 0.8 - 0.6 * math.exp(-0.3 * lidx)
    x2d = x.reshape(rows, d)
    a, q, k, v, z, c, gcol, grow, *flash = _inproj(x2d, ln_pre, w_in, b_in, flash_operands)

    ya, conv_new = _conv(a.reshape(bsz, t_len, 3 * C_A), conv_state, conv_w, conv_b, conv_ln_g,
                         conv_ln_b)
    yb = attn(q, k, v, z, lam_qk, attn_g, lam_init, *flash)
    grow3 = jnp.swapaxes(grow.reshape(N_GATE, bsz, t_len), 0, 1)
    yc, c_new, n_new, m_new = _mlstm(c.reshape(bsz, t_len, 5 * D_C), gcol.reshape(bsz, t_len, N_GATE),
                                     grow3, f_bias, mlstm_g, *mstate, mlstm_chunk)
    x_new = _outproj(x2d, ya.reshape(rows, C_A), a, yb.reshape(rows, D_B), yc.reshape(rows, D_C),
                     conv_pw, w_out, ln_post)
    return (x_new.reshape(bsz, t_len, d), k.reshape(bsz, t_len, H_B, 2 * DK_B),
            v.reshape(bsz, t_len, H_B, DV_B), conv_new, (c_new, n_new, m_new))


def kernel(x_prompt, x_sample, cache_k, cache_v, state_conv, state_C, state_n, state_m, page_table,
           ln_pre, w_in, b_in, conv_w, conv_b, conv_ln_g, conv_ln_b, conv_pw, lam_qk, attn_g, f_bias,
           mlstm_g, w_out, ln_post):
    depth = w_in.shape[0]
    bp, tp, _ = x_prompt.shape
    bs, ts, _ = x_sample.shape
    assert bp == 1, "prompt attention sweep is written for a single prompt sequence"
    n_pool, page = cache_k.shape[1], cache_k.shape[2]
    ck = cache_k.reshape(depth, n_pool, page * H_B, DV_B)
    cv = cache_v.reshape(depth, n_pool, page * H_B, DV_B)

    def weights(l):
        return (ln_pre[l], w_in[l], b_in[l], conv_w[l], conv_b[l], conv_ln_g[l], conv_ln_b[l],
                conv_pw[l], lam_qk[l], attn_g[l], f_bias[l], mlstm_g[l], w_out[l], ln_post[l])

    def prompt_attn(q, k, v, z, lq, g, lam_init, kb, vt):
        return _attn_prompt(q, kb, vt, z, lq, g, lam_init)

    prompt_chunk = _pick(tp, (128, 64, 32, 16, 8))
    xp = x_prompt
    outs_p = [[] for _ in range(6)]
    for l in range(depth):
        buf0 = jnp.zeros((bp, CONV_W - 1, C_A), F32)
        m0 = (jnp.zeros((bp, H_C, DK_C, DV_C), F32), jnp.zeros((bp, H_C, DK_C), F32),
              jnp.zeros((bp, H_C), F32))
        xp, kr, vr, cb, (cm, nv, mv) = _layer(xp, l, buf0, m0, prompt_attn, True, weights(l),
                                              prompt_chunk)
        for lst, val in zip(outs_p, (kr, vr, cb, cm, nv, mv)):
            lst.append(val)

    xs = x_sample
    outs_s = [[] for _ in range(6)]
    for l in range(depth):
        def sample_attn(q, k, v, z, lq, g, lam_init, l=l):
            return _attn_decode(q.reshape(bs, ts, D_B), k.reshape(bs, ts, D_B), v.reshape(bs, ts, D_B),
                                z.reshape(bs, ts, D_B), ck, cv, page_table, lq, g, lam_init, l)

        st = (state_C[l], state_n[l], state_m[l])
        xs, kr, vr, cb, (cm, nv, mv) = _layer(xs, l, state_conv[l], st, sample_attn, False,
                                              weights(l), 8)
        for lst, val in zip(outs_s, (kr, vr, cb, cm, nv, mv)):
            lst.append(val)

    return (xp, xs, *[jnp.stack(o) for o in outs_p], *[jnp.stack(o) for o in outs_s])
```

```python
import functools
import math

import jax
import jax.numpy as jnp
from jax import lax
from jax.experimental import pallas as pl
from jax.experimental.pallas import tpu as pltpu

F32 = jnp.float32
BF16 = jnp.bfloat16

C_A = 256
CONV_W = 31
H_B = 4
DK_B = 64
DV_B = 128
D_B = H_B * DV_B
H_C = 4
DK_C = 64
DV_C = 64
D_C = H_C * DV_C

OFF_A = 0
OFF_Q = 3 * C_A
OFF_K = OFF_Q + 2 * H_B * DK_B
OFF_V = OFF_K + 2 * H_B * DK_B
OFF_Z = OFF_V + D_B
OFF_C = OFF_Z + D_B
OFF_G = OFF_C + 5 * D_C
N_MAIN = OFF_G
N_GATE = 2 * H_C

HIST = 32
NEG = -1e30
QK_SCALE = DK_B ** -0.5
KC_SCALE = DK_C ** -0.5
LOG2E = math.log2(math.e)

VMEM_LIMIT_BYTES = 48 * 1024 * 1024


def _cparams(n_axes):
    return pltpu.CompilerParams(dimension_semantics=("arbitrary",) * n_axes,
                                vmem_limit_bytes=VMEM_LIMIT_BYTES)


def _pick(n, candidates):
    for c in candidates:
        if n % c == 0:
            return c
    return n


def _sigmoid(x):
    return 1.0 / (1.0 + jnp.exp(-x))


def _silu(x):
    return x * _sigmoid(x)


def _log_sigmoid(x):
    return jnp.minimum(x, 0.0) - jnp.log(1.0 + jnp.exp(-jnp.abs(x)))


def _inproj_body(x_ref, g_ref, w_ref, b_ref, wg_ref, wgt_ref, bgc_ref, bgr_ref, *rest, flash_operands):
    if flash_operands:
        wvt_ref, bvt_ref = rest[:2]
        rest = rest[2:]
    a_ref, q_ref, k_ref, v_ref, z_ref, c_ref, gc_ref, gr_ref = rest[:8]
    xf = x_ref[...]
    h = xf * lax.rsqrt(jnp.mean(xf * xf, axis=-1, keepdims=True) + 1e-6) * g_ref[...]
    hb = h.astype(BF16)
    rows_last = (((1,), (1,)), ((), ()))

    def seg(lo, hi):
        return jnp.dot(hb, w_ref[:, lo:hi], preferred_element_type=F32) + b_ref[:, lo:hi]

    a_ref[...] = seg(OFF_A, OFF_Q)
    q_scale = QK_SCALE * LOG2E if flash_operands else QK_SCALE
    q_ref[...] = (seg(OFF_Q, OFF_K) * q_scale).astype(BF16)
    kk = seg(OFF_K, OFF_V)
    vv = seg(OFF_V, OFF_Z)
    tm = kk.shape[0]
    for h in range(H_B):
        k_ref[pl.ds(h, tm, stride=H_B), :] = kk[:, h * DV_B:(h + 1) * DV_B]
        v_ref[pl.ds(h, tm, stride=H_B), :] = vv[:, h * DV_B:(h + 1) * DV_B]
    z_ref[...] = seg(OFF_Z, OFF_C)
    c_ref[...] = seg(OFF_C, OFF_G)
    gc_ref[...] = jnp.dot(hb, wg_ref[...], preferred_element_type=F32) + bgc_ref[...]
    gr_ref[...] = lax.dot_general(wgt_ref[...], hb, rows_last, preferred_element_type=F32) + bgr_ref[...]
    if flash_operands:
        kb_ref, vt_ref = rest[8:]
        kb_ref[...] = kk.astype(BF16)
        vt = lax.dot_general(wvt_ref[...], hb, rows_last, preferred_element_type=F32) + bvt_ref[...]
        vt_ref[...] = vt.astype(BF16)


def _inproj(x2d, ln_pre, w_in, b_in, flash_operands):
    rows, d = x2d.shape
    tm = _pick(rows, (256, 128))
    w_main = w_in[:, :N_MAIN].astype(BF16)
    w_gate = w_in[:, N_MAIN:].astype(BF16)
    b_main = b_in[:N_MAIN].reshape(1, N_MAIN)
    b_gate = b_in[N_MAIN:]
    row = lambda i: (i, 0)
    col = lambda i: (0, i)
    fixed = lambda i: (0, 0)
    head_rows = jax.ShapeDtypeStruct((rows * H_B, DV_B), F32)
    head_spec = pl.BlockSpec((tm * H_B, DV_B), row)
    out_shape = [jax.ShapeDtypeStruct((rows, OFF_Q - OFF_A), F32),
                 jax.ShapeDtypeStruct((rows, OFF_K - OFF_Q), BF16),
                 head_rows, head_rows,
                 jax.ShapeDtypeStruct((rows, OFF_C - OFF_Z), F32),
                 jax.ShapeDtypeStruct((rows, OFF_G - OFF_C), F32)]
    out_specs = [pl.BlockSpec((tm, OFF_Q - OFF_A), row), pl.BlockSpec((tm, OFF_K - OFF_Q), row),
                 head_spec, head_spec,
                 pl.BlockSpec((tm, OFF_C - OFF_Z), row), pl.BlockSpec((tm, OFF_G - OFF_C), row)]
    out_shape += [jax.ShapeDtypeStruct((rows, N_GATE), F32), jax.ShapeDtypeStruct((N_GATE, rows), F32)]
    out_specs += [pl.BlockSpec((tm, N_GATE), row), pl.BlockSpec((N_GATE, tm), col)]
    in_specs = [pl.BlockSpec((tm, d), row),
                pl.BlockSpec((1, d), fixed),
                pl.BlockSpec((d, N_MAIN), fixed),
                pl.BlockSpec((1, N_MAIN), fixed),
                pl.BlockSpec((d, N_GATE), fixed),
                pl.BlockSpec((N_GATE, d), fixed),
                pl.BlockSpec((1, N_GATE), fixed),
                pl.BlockSpec((N_GATE, 1), fixed)]
    args = [x2d, ln_pre.reshape(1, d), w_main, b_main, w_gate, w_gate.T,
            b_gate.reshape(1, N_GATE), b_gate.reshape(N_GATE, 1)]
    if flash_operands:
        in_specs += [pl.BlockSpec((D_B, d), fixed), pl.BlockSpec((D_B, 1), fixed)]
        args += [w_main[:, OFF_V:OFF_Z].T, b_in[OFF_V:OFF_Z].reshape(D_B, 1)]
        out_shape += [jax.ShapeDtypeStruct((rows, D_B), BF16), jax.ShapeDtypeStruct((D_B, rows), BF16)]
        out_specs += [pl.BlockSpec((tm, D_B), row), pl.BlockSpec((D_B, tm), col)]
    return pl.pallas_call(
        functools.partial(_inproj_body, flash_operands=flash_operands),
        grid=(rows // tm,),
        in_specs=in_specs,
        out_specs=out_specs,
        out_shape=out_shape,
        compiler_params=_cparams(1),
        name="inproj",
    )(*args)


SUBLANES = 8


def _conv_body(a_ref, st_ref, cw_ref, cb_ref, lg_ref, lb_ref, y_ref, hist_ref, fbuf, shifted,
               *, tt, rc):
    t = pl.program_id(1)

    @pl.when(t == 0)
    def _():
        fbuf[0:HIST, :] = st_ref[0]

    a = a_ref[0]
    fbuf[HIST:HIST + tt, :] = a[:, 0:C_A] * _sigmoid(a[:, C_A:2 * C_A])
    n_rows = shifted.shape[1]
    for k in range(1, SUBLANES):
        shifted[k] = fbuf[pl.ds(k, n_rows), :]
    first = HIST - (CONV_W - 1)
    for r0 in range(0, tt, rc):
        acc = jnp.zeros((rc, C_A), F32)
        for j in range(CONV_W):
            k = (first + j) % SUBLANES
            base = first + j - k + r0
            win = fbuf[base:base + rc, :] if k == 0 else shifted[k, base:base + rc, :]
            acc = acc + cw_ref[j:j + 1, :] * win
        y = acc + cb_ref[...]
        mu = jnp.mean(y, axis=-1, keepdims=True)
        yc = y - mu
        yn = yc * lax.rsqrt(jnp.mean(yc * yc, axis=-1, keepdims=True) + 1e-5)
        y_ref[0, r0:r0 + rc, :] = _silu(yn * lg_ref[...] + lb_ref[...]).astype(y_ref.dtype)
    new_hist = fbuf[pl.ds(tt, HIST), :]
    hist_ref[0] = new_hist
    fbuf[0:HIST, :] = new_hist


def _conv(a3d, state, conv_w, conv_b, ln_g, ln_b):
    bsz, t_len, _ = a3d.shape
    tt = _pick(t_len, (256, 128, 64, 32, 16, 8))
    rc = min(tt, 64)
    buf_rows = HIST + max(tt, SUBLANES)
    st = jnp.pad(state, ((0, 0), (HIST - (CONV_W - 1), 0), (0, 0)))
    fixed = lambda b, t: (0, 0)
    y, hist = pl.pallas_call(
        functools.partial(_conv_body, tt=tt, rc=rc),
        grid=(bsz, t_len // tt),
        in_specs=[pl.BlockSpec((1, tt, 3 * C_A), lambda b, t: (b, t, 0)),
                  pl.BlockSpec((1, HIST, C_A), lambda b, t: (b, 0, 0)),
                  pl.BlockSpec((CONV_W, C_A), fixed),
                  pl.BlockSpec((1, C_A), fixed),
                  pl.BlockSpec((1, C_A), fixed),
                  pl.BlockSpec((1, C_A), fixed)],
        out_specs=[pl.BlockSpec((1, tt, C_A), lambda b, t: (b, t, 0)),
                   pl.BlockSpec((1, HIST, C_A), lambda b, t: (b, 0, 0))],
        out_shape=[jax.ShapeDtypeStruct((bsz, t_len, C_A), BF16),
                   jax.ShapeDtypeStruct((bsz, HIST, C_A), F32)],
        scratch_shapes=[pltpu.VMEM((buf_rows, C_A), F32),
                        pltpu.VMEM((SUBLANES, buf_rows - SUBLANES, C_A), F32)],
        compiler_params=_cparams(2),
        name="conv",
    )(a3d, st, conv_w, conv_b.reshape(1, C_A), ln_g.reshape(1, C_A), ln_b.reshape(1, C_A))
    return y, hist[:, HIST - (CONV_W - 1):, :]


def _lambda(lq, lam_init):
    s01 = jnp.sum(lq[0:1, :] * lq[1:2, :], axis=-1, keepdims=True)
    s23 = jnp.sum(lq[2:3, :] * lq[3:4, :], axis=-1, keepdims=True)
    return jnp.exp(s01) - jnp.exp(s23) + lam_init


def _attn_finish(o, g, z, lam_init):
    y = o * lax.rsqrt(jnp.mean(o * o, axis=-1, keepdims=True) + 1e-6) * g * (1.0 - lam_init)
    return y * _silu(z)


N_MAPS = 2 * H_B
SUM_ROWS = 16


def _attn_body(it_ref, jt_ref, q_ref, k_ref, vt_ref, z_ref, lq_ref, g_ref, o_ref,
               m_scr, acc_scr, *, blk, kb_step, lam_init):
    step_id = pl.program_id(0)
    i = it_ref[step_id]
    jj = jt_ref[step_id]

    @pl.when(jj == 0)
    def _():
        m_scr[...] = jnp.full(m_scr.shape, NEG, F32)
        acc_scr[...] = jnp.zeros(acc_scr.shape, F32)

    def sweep(diagonal, sub):
        keys = pl.ds(pl.multiple_of(sub * blk, blk), blk)
        if diagonal:
            kpos = lax.broadcasted_iota(jnp.int32, (blk, blk), 0)
            qpos = lax.broadcasted_iota(jnp.int32, (blk, blk), 1)
            keep = kpos <= qpos
        lane = lax.broadcasted_iota(jnp.int32, (blk, DV_B), 1)
        ones = jnp.ones((SUM_ROWS, blk), BF16)

        def scores(r):
            h, c = divmod(r, 2)
            hs = slice(h * DV_B, (h + 1) * DV_B)
            qh = q_ref[:, hs]
            qc = jnp.where((lane < DK_B) == (c == 0), qh, jnp.zeros_like(qh))
            st = lax.dot_general(k_ref[keys, hs], qc, (((1,), (1,)), ((), ())),
                                 preferred_element_type=F32)
            if diagonal:
                st = jnp.where(keep, st, NEG)
            m_prev = m_scr[r:r + 1, :]
            m_new = jnp.maximum(m_prev, jnp.max(st, axis=0, keepdims=True))
            return st, m_prev, m_new

        def values(r, pt, alpha):
            h = r // 2
            vth = jnp.concatenate([vt_ref[h * DV_B:(h + 1) * DV_B, keys], ones], axis=0)
            acc_scr[r] = alpha * acc_scr[r] + jnp.dot(vth, pt, preferred_element_type=F32)

        nxt = scores(0)
        pending = None
        for r in range(N_MAPS):
            st, m_prev, m_new = nxt
            if r + 1 < N_MAPS:
                nxt = scores(r + 1)
            alpha = jnp.exp2(m_prev - m_new)
            pt = jnp.exp2(st - m_new).astype(BF16)
            m_scr[r:r + 1, :] = m_new
            if pending is not None:
                values(*pending)
            pending = (r, pt, alpha)
        values(*pending)

    first = jj * kb_step
    n_below = jnp.clip(i - first, 0, kb_step)

    def below(sub, carry):
        sweep(False, sub)
        return carry

    lax.fori_loop(0, n_below, below, 0)

    @pl.when(i - first < kb_step)
    def _():
        sweep(True, i - first)
        lam = _lambda(lq_ref[...], lam_init)
        for h in range(H_B):
            hs = slice(h * DV_B, (h + 1) * DV_B)
            r = 2 * h
            a1 = acc_scr[r, 0:DV_B, :] * (1.0 / acc_scr[r, DV_B:DV_B + 1, :])
            a2 = acc_scr[r + 1, 0:DV_B, :] * (1.0 / acc_scr[r + 1, DV_B:DV_B + 1, :])
            ot = a1 - lam * a2
            yt = ot * lax.rsqrt(jnp.mean(ot * ot, axis=0, keepdims=True) + 1e-6) * g_ref[...]
            o_ref[:, hs] = (yt.T * (1.0 - lam_init) * _silu(z_ref[:, hs])).astype(o_ref.dtype)


def _attn_prompt(q, kb, vt, z, lam_qk, attn_g, lam_init):
    t_len = q.shape[0]
    blk = _pick(t_len, (512, 256, 128))
    nb = t_len // blk
    kb_step = _pick(nb, (4, 2, 1))
    pairs = [(i, jj) for i in range(nb) for jj in range(i // kb_step + 1)]
    i_tab = jnp.asarray([p[0] for p in pairs], jnp.int32)
    j_tab = jnp.asarray([p[1] for p in pairs], jnp.int32)
    q_map = lambda s, it, jt: (it[s], 0)
    k_map = lambda s, it, jt: (jt[s], 0)
    vt_map = lambda s, it, jt: (0, jt[s])
    fixed = lambda s, it, jt: (0, 0)
    return pl.pallas_call(
        functools.partial(_attn_body, blk=blk, kb_step=kb_step, lam_init=lam_init),
        grid_spec=pltpu.PrefetchScalarGridSpec(
            num_scalar_prefetch=2,
            grid=(len(pairs),),
            in_specs=[pl.BlockSpec((blk, D_B), q_map),
                      pl.BlockSpec((kb_step * blk, D_B), k_map),
                      pl.BlockSpec((D_B, kb_step * blk), vt_map),
                      pl.BlockSpec((blk, D_B), q_map),
                      pl.BlockSpec((4, DK_B), fixed),
                      pl.BlockSpec((DV_B, 1), fixed)],
            out_specs=pl.BlockSpec((blk, D_B), q_map),
            scratch_shapes=[pltpu.VMEM((N_MAPS, blk), F32),
                            pltpu.VMEM((N_MAPS, DV_B + SUM_ROWS, blk), F32)]),
        out_shape=jax.ShapeDtypeStruct((t_len, D_B), BF16),
        compiler_params=_cparams(1),
        name="attn_prompt",
    )(i_tab, j_tab, q, kb, vt, z, lam_qk, attn_g.reshape(DV_B, 1))


QROWS = 16


QR = H_B * QROWS


def _decode_body(pt_ref, q_ref, kn_ref, vn_ref, z_ref, lq_ref, g_ref, *rest,
                 n_pages_step, t_new, lam_init):
    k_refs = rest[:n_pages_step]
    v_refs = rest[n_pages_step:2 * n_pages_step]
    o_ref, m_scr, l_scr, acc_scr = rest[2 * n_pages_step:]
    g = pl.program_id(1)
    n_groups = pl.num_programs(1)

    @pl.when(g == 0)
    def _():
        m_scr[...] = jnp.full(m_scr.shape, NEG, F32)
        l_scr[...] = jnp.zeros(l_scr.shape, F32)
        acc_scr[...] = jnp.zeros(acc_scr.shape, F32)

    q64 = q_ref[0]
    row = lax.broadcasted_iota(jnp.int32, q64.shape, 0)
    lane = lax.broadcasted_iota(jnp.int32, q64.shape, 1)
    qd = jnp.where((lane < DK_B) == ((row % QROWS) < QROWS // 2), q64, jnp.zeros_like(q64))

    page = k_refs[0].shape[0] // H_B

    def head_rows(refs, h):
        return jnp.concatenate([r[pl.ds(h, page, stride=H_B), :] for r in refs], axis=0).astype(BF16)

    heads = range(H_B)
    rows = [slice(h * QROWS, (h + 1) * QROWS) for h in heads]
    s = [lax.dot_general(qd[rows[h], :], head_rows(k_refs, h), (((1,), (1,)), ((), ())),
                         preferred_element_type=F32) for h in heads]
    m_prev = [m_scr[rows[h], :] for h in heads]
    m_new = [jnp.maximum(m_prev[h], jnp.max(s[h], axis=-1, keepdims=True)) for h in heads]
    alpha = [jnp.exp(m_prev[h] - m_new[h]) for h in heads]
    p = [jnp.exp(s[h] - m_new[h]) for h in heads]
    pv = [jnp.dot(p[h].astype(BF16), head_rows(v_refs, h), preferred_element_type=F32) for h in heads]
    for h in heads:
        l_scr[rows[h], :] = alpha[h] * l_scr[rows[h], :] + jnp.sum(p[h], axis=-1, keepdims=True)
        acc_scr[rows[h], :] = alpha[h] * acc_scr[rows[h], :] + pv[h]
        m_scr[rows[h], :] = m_new[h]

    @pl.when(g == n_groups - 1)
    def _():
        lam = _lambda(lq_ref[...], lam_init)
        qf = qd.astype(F32)
        tok = lax.broadcasted_iota(jnp.int32, (QR, 1), 0) % (QROWS // 2)
        s_new = []
        for jn in range(t_new):
            sj = jnp.sum(qf * kn_ref[0, jn].astype(BF16).astype(F32), axis=-1, keepdims=True)
            s_new.append(jnp.where(tok >= jn, sj, NEG))
        m_prev = m_scr[...]
        m_fin = m_prev
        for sj in s_new:
            m_fin = jnp.maximum(m_fin, sj)
        alpha = jnp.exp(m_prev - m_fin)
        l_fin = alpha * l_scr[...]
        acc = alpha * acc_scr[...]
        for jn, sj in enumerate(s_new):
            pj = jnp.exp(sj - m_fin)
            l_fin = l_fin + pj
            acc = acc + pj.astype(BF16).astype(F32) * vn_ref[0, jn].astype(BF16).astype(F32)
        a = acc / l_fin
        half = QROWS // 2
        for h in range(H_B):
            hs = slice(h * DV_B, (h + 1) * DV_B)
            o = a[h * QROWS:h * QROWS + half, :] - lam * a[h * QROWS + half:(h + 1) * QROWS, :]
            o_ref[0, :, hs] = _attn_finish(o, g_ref[...], z_ref[0, :, hs], lam_init)


def _attn_decode(q, k_new, v_new, z, cache_k_l, cache_v_l, page_table, lam_qk, attn_g, lam_init,
                 layer):
    bsz, t_new, _ = q.shape
    n_pages = page_table.shape[1]
    page_rows = cache_k_l.shape[2]
    pps = _pick(n_pages, (16, 8, 4, 2, 1))
    half = QROWS // 2
    qh = jnp.swapaxes(q.reshape(bsz, t_new, H_B, DV_B), 1, 2)
    q64 = jnp.zeros((bsz, H_B, QROWS, DV_B), BF16)
    q64 = q64.at[:, :, 0:t_new].set(qh).at[:, :, half:half + t_new].set(qh).reshape(bsz, QR, DV_B)
    on_rows = lambda x: jnp.repeat(x.reshape(bsz, t_new, H_B, DV_B), QROWS, axis=2)
    z8 = jnp.pad(z, ((0, 0), (0, half - t_new), (0, 0)))
    fixed = lambda b, g, pt: (0, 0)
    per_b = lambda b, g, pt: (b, 0, 0)
    per_b4 = lambda b, g, pt: (b, 0, 0, 0)

    def page_spec(jp):
        return pl.BlockSpec((None, None, page_rows, DV_B),
                            lambda b, g, pt: (layer, pt[b, g * pps + jp], 0, 0))

    in_specs = [pl.BlockSpec((1, QR, DV_B), per_b),
                pl.BlockSpec((1, t_new, QR, DV_B), per_b4),
                pl.BlockSpec((1, t_new, QR, DV_B), per_b4),
                pl.BlockSpec((1, half, D_B), per_b),
                pl.BlockSpec((4, DK_B), fixed),
                pl.BlockSpec((1, DV_B), fixed)]
    in_specs += [page_spec(jp) for jp in range(pps)] * 2
    out = pl.pallas_call(
        functools.partial(_decode_body, n_pages_step=pps, t_new=t_new, lam_init=lam_init),
        grid_spec=pltpu.PrefetchScalarGridSpec(
            num_scalar_prefetch=1,
            grid=(bsz, n_pages // pps),
            in_specs=in_specs,
            out_specs=pl.BlockSpec((1, half, D_B), per_b),
            scratch_shapes=[pltpu.VMEM((QR, 1), F32), pltpu.VMEM((QR, 1), F32),
                            pltpu.VMEM((QR, DV_B), F32)]),
        out_shape=jax.ShapeDtypeStruct((bsz, half, D_B), F32),
        compiler_params=_cparams(2),
        name="attn_decode",
    )(page_table, q64, on_rows(k_new), on_rows(v_new), z8, lam_qk, attn_g.reshape(1, DV_B),
      *([cache_k_l] * pps), *([cache_v_l] * pps))
    return out[:, :t_new, :]


def _mlstm_body(c_ref, gc_ref, gr_ref, fbc_ref, fbr_ref, mg_ref, c0_ref, n0_ref, m0_ref,
                y_ref, cn_ref, nn_ref, mn_ref, c_scr, n_scr, m_scr, *, ln, t_valid, t_pad):
    t = pl.program_id(1)
    n_t = pl.num_programs(1)

    @pl.when(t == 0)
    def _():
        c_scr[...] = c0_ref[0]
        n_scr[...] = n0_ref[0]
        m_scr[...] = m0_ref[0]

    cblk = c_ref[0]
    gcol = gc_ref[0]
    grow = gr_ref[0]
    is_f_c = lax.broadcasted_iota(jnp.int32, (ln, N_GATE), 1) >= H_C
    is_f_r = lax.broadcasted_iota(jnp.int32, (N_GATE, ln), 0) >= H_C
    g_c = jnp.where(is_f_c, _log_sigmoid(gcol + fbc_ref[...]), gcol)
    g_r = jnp.where(is_f_r, _log_sigmoid(grow + fbr_ref[...]), grow)
    if t_pad != t_valid:
        ok_c = (t * ln + lax.broadcasted_iota(jnp.int32, (ln, N_GATE), 0)) < t_valid
        ok_r = (t * ln + lax.broadcasted_iota(jnp.int32, (N_GATE, ln), 1)) < t_valid
        g_c = jnp.where(ok_c, g_c, jnp.where(is_f_c, 0.0, NEG))
        g_r = jnp.where(ok_r, g_r, jnp.where(is_f_r, 0.0, NEG))
    rr = lax.broadcasted_iota(jnp.int32, (ln, ln), 0)
    cc = lax.broadcasted_iota(jnp.int32, (ln, ln), 1)
    causal = rr >= cc
    tri = causal.astype(F32)
    cum_c = jnp.dot(tri, jnp.where(is_f_c, g_c, 0.0), preferred_element_type=F32,
                    precision=lax.Precision.HIGHEST)
    cum_r = lax.dot_general(jnp.where(is_f_r, g_r, 0.0), tri, (((1,), (1,)), ((), ())),
                            preferred_element_type=F32, precision=lax.Precision.HIGHEST)
    m_all = m_scr[...]
    lane = lax.broadcasted_iota(jnp.int32, (1, 128), 1)
    m_out = jnp.zeros((1, 128), F32)

    for h in range(H_C):
        qf = cblk[:, h * DK_C:(h + 1) * DK_C]
        kf = cblk[:, D_C + h * DK_C:D_C + (h + 1) * DK_C] * KC_SCALE
        vf = cblk[:, 2 * D_C + h * DV_C:2 * D_C + (h + 1) * DV_C]
        of = cblk[:, 3 * D_C + h * DV_C:3 * D_C + (h + 1) * DV_C]
        zf = cblk[:, 4 * D_C + h * DV_C:4 * D_C + (h + 1) * DV_C]
        qb, kb, vb = qf.astype(BF16), kf.astype(BF16), vf.astype(BF16)
        bc = cum_c[:, H_C + h:H_C + h + 1]
        igc = g_c[:, h:h + 1]
        br = cum_r[H_C + h:H_C + h + 1, :]
        igr = g_r[h:h + 1, :]
        m_prev = m_all[:, h:h + 1]
        c_mat = c_scr[h]
        n_vec = n_scr[h]

        log_d = jnp.where(causal, bc - br + igr, NEG)
        inter = bc + m_prev
        m_t = jnp.maximum(inter, jnp.max(log_d, axis=-1, keepdims=True))
        d_w = jnp.where(causal, jnp.exp(log_d - m_t), 0.0)
        i_w = jnp.exp(inter - m_t)
        qk = lax.dot_general(qb, kb, (((1,), (1,)), ((), ())), preferred_element_type=F32)
        w = d_w * qk
        num = (jnp.dot(w.astype(BF16), vb, preferred_element_type=F32)
               + i_w * jnp.dot(qb, c_mat.astype(BF16), preferred_element_type=F32))
        den = jnp.sum(w, axis=-1, keepdims=True) + i_w * jnp.sum(qf * n_vec, axis=-1, keepdims=True)
        hh = num / jnp.maximum(jnp.abs(den), jnp.exp(-m_t))

        b_last = bc[ln - 1:ln, :]
        m_last = m_t[ln - 1:ln, :]
        i_last = i_w[ln - 1:ln, :]
        w_last = jnp.exp(b_last - bc + igc - m_last)
        kw = w_last * kf
        c_scr[h] = i_last * c_mat + lax.dot_general(kw.astype(BF16), vb, (((0,), (0,)), ((), ())),
                                                    preferred_element_type=F32)
        n_scr[h] = i_last * n_vec + jnp.sum(kw, axis=0, keepdims=True)
        m_out = jnp.where(lane == h, m_last, m_out)

        hg = _sigmoid(of) * hh
        yh = hg * lax.rsqrt(jnp.mean(hg * hg, axis=-1, keepdims=True) + 1e-6)
        yh = yh * mg_ref[:, h * DV_C:(h + 1) * DV_C]
        y_ref[0, :, h * DV_C:(h + 1) * DV_C] = (yh * _silu(zf)).astype(y_ref.dtype)

    m_scr[...] = m_out

    @pl.when(t == n_t - 1)
    def _():
        cn_ref[0] = c_scr[...]
        nn_ref[0] = n_scr[...]
        mn_ref[0] = m_scr[...]


def _mlstm(c3d, gcol, grow, f_bias, mlstm_g, c0, n0, m0, ln):
    bsz, t_valid, _ = c3d.shape
    t_pad = -(-t_valid // ln) * ln
    if t_pad != t_valid:
        extra = t_pad - t_valid
        c3d = jnp.pad(c3d, ((0, 0), (0, extra), (0, 0)))
        gcol = jnp.pad(gcol, ((0, 0), (0, extra), (0, 0)))
        grow = jnp.pad(grow, ((0, 0), (0, 0), (0, extra)))
    m0p = jnp.pad(m0, ((0, 0), (0, 128 - H_C))).reshape(bsz, 1, 128)
    n0r = n0.reshape(bsz, H_C, 1, DK_C)
    fb8 = jnp.concatenate([jnp.zeros((H_C,), F32), f_bias])
    fixed = lambda b, t: (0, 0)
    y, cn, nn, mn = pl.pallas_call(
        functools.partial(_mlstm_body, ln=ln, t_valid=t_valid, t_pad=t_pad),
        grid=(bsz, t_pad // ln),
        in_specs=[pl.BlockSpec((1, ln, 5 * D_C), lambda b, t: (b, t, 0)),
                  pl.BlockSpec((1, ln, N_GATE), lambda b, t: (b, t, 0)),
                  pl.BlockSpec((1, N_GATE, ln), lambda b, t: (b, 0, t)),
                  pl.BlockSpec((1, N_GATE), fixed),
                  pl.BlockSpec((N_GATE, 1), fixed),
                  pl.BlockSpec((1, D_C), fixed),
                  pl.BlockSpec((1, H_C, DK_C, DV_C), lambda b, t: (b, 0, 0, 0)),
                  pl.BlockSpec((1, H_C, 1, DK_C), lambda b, t: (b, 0, 0, 0)),
                  pl.BlockSpec((1, 1, 128), lambda b, t: (b, 0, 0))],
        out_specs=[pl.BlockSpec((1, ln, D_C), lambda b, t: (b, t, 0)),
                   pl.BlockSpec((1, H_C, DK_C, DV_C), lambda b, t: (b, 0, 0, 0)),
                   pl.BlockSpec((1, H_C, 1, DK_C), lambda b, t: (b, 0, 0, 0)),
                   pl.BlockSpec((1, 1, 128), lambda b, t: (b, 0, 0))],
        out_shape=[jax.ShapeDtypeStruct((bsz, t_pad, D_C), BF16),
                   jax.ShapeDtypeStruct((bsz, H_C, DK_C, DV_C), F32),
                   jax.ShapeDtypeStruct((bsz, H_C, 1, DK_C), F32),
                   jax.ShapeDtypeStruct((bsz, 1, 128), F32)],
        scratch_shapes=[pltpu.VMEM((H_C, DK_C, DV_C), F32), pltpu.VMEM((H_C, 1, DK_C), F32),
                        pltpu.VMEM((1, 128), F32)],
        compiler_params=_cparams(2),
        name="mlstm",
    )(c3d, gcol, grow, fb8.reshape(1, N_GATE), fb8.reshape(N_GATE, 1), mlstm_g.reshape(1, D_C),
      c0, n0r, m0p)
    return y[:, :t_valid], cn, nn.reshape(bsz, H_C, DK_C), mn[:, 0, :H_C]


def _outproj_body(x_ref, ya_ref, az_ref, yb_ref, yc_ref, pw_ref, w_ref, g_ref, o_ref):
    ya = jnp.dot(ya_ref[...].astype(BF16), pw_ref[...], preferred_element_type=F32)
    ya = ya * _silu(az_ref[...])
    out = jnp.dot(ya.astype(BF16), w_ref[0:C_A, :], preferred_element_type=F32)
    out = out + jnp.dot(yb_ref[...].astype(BF16), w_ref[C_A:C_A + D_B, :], preferred_element_type=F32)
    out = out + jnp.dot(yc_ref[...].astype(BF16), w_ref[C_A + D_B:, :], preferred_element_type=F32)
    y = out * lax.rsqrt(jnp.mean(out * out, axis=-1, keepdims=True) + 1e-6) * g_ref[...]
    o_ref[...] = x_ref[...] + y


def _outproj(x2d, ya, a2d, yb, yc, conv_pw, w_out, ln_post):
    rows, d = x2d.shape
    tm = _pick(rows, (256, 128))
    row = lambda i: (i, 0)
    fixed = lambda i: (0, 0)
    return pl.pallas_call(
        _outproj_body,
        grid=(rows // tm,),
        in_specs=[pl.BlockSpec((tm, d), row),
                  pl.BlockSpec((tm, C_A), row),
                  pl.BlockSpec((tm, C_A), lambda i: (i, 2)),
                  pl.BlockSpec((tm, D_B), row),
                  pl.BlockSpec((tm, D_C), row),
                  pl.BlockSpec((C_A, C_A), fixed),
                  pl.BlockSpec((C_A + D_B + D_C, d), fixed),
                  pl.BlockSpec((1, d), fixed)],
        out_specs=pl.BlockSpec((tm, d), row),
        out_shape=jax.ShapeDtypeStruct((rows, d), F32),
        compiler_params=_cparams(1),
        name="outproj",
    )(x2d, ya, a2d, yb, yc, conv_pw.astype(BF16), w_out.astype(BF16), ln_post.reshape(1, d))


def _layer(x, lidx, conv_state, mstate, attn, flash_operands, weights, mlstm_chunk):
    (ln_pre, w_in, b_in, conv_w, conv_b, conv_ln_g, conv_ln_b, conv_pw, lam_qk, attn_g, f_bias,
     mlstm_g, w_out, ln_post) = weights
    bsz, t_len, d = x.shape
    rows = bsz * t_len
    lam_init = 0.8 - 0.6 * math.exp(-0.3 * lidx)
    x2d = x.reshape(rows, d)
    a, q, k, v, z, c, gcol, grow, *flash = _inproj(x2d, ln_pre, w_in, b_in, flash_operands)

    ya, conv_new = _conv(a.reshape(bsz, t_len, 3 * C_A), conv_state, conv_w, conv_b, conv_ln_g,
                         conv_ln_b)
    yb = attn(q, k, v, z, lam_qk, attn_g, lam_init, *flash)
    grow3 = jnp.swapaxes(grow.reshape(N_GATE, bsz, t_len), 0, 1)
    yc, c_new, n_new, m_new = _mlstm(c.reshape(bsz, t_len, 5 * D_C), gcol.reshape(bsz, t_len, N_GATE),
                                     grow3, f_bias, mlstm_g, *mstate, mlstm_chunk)
    x_new = _outproj(x2d, ya.reshape(rows, C_A), a, yb.reshape(rows, D_B), yc.reshape(rows, D_C),
                     conv_pw, w_out, ln_post)
    return (x_new.reshape(bsz, t_len, d), k.reshape(bsz, t_len, H_B, 2 * DK_B),
            v.reshape(bsz, t_len, H_B, DV_B), conv_new, (c_new, n_new, m_new))


def kernel(x_prompt, x_sample, cache_k, cache_v, state_conv, state_C, state_n, state_m, page_table,
           ln_pre, w_in, b_in, conv_w, conv_b, conv_ln_g, conv_ln_b, conv_pw, lam_qk, attn_g, f_bias,
           mlstm_g, w_out, ln_post):
    depth = w_in.shape[0]
    bp, tp, _ = x_prompt.shape
    bs, ts, _ = x_sample.shape
    assert bp == 1, "prompt attention sweep is written for a single prompt sequence"
    n_pool, page = cache_k.shape[1], cache_k.shape[2]
    ck = cache_k.reshape(depth, n_pool, page * H_B, DV_B)
    cv = cache_v.reshape(depth, n_pool, page * H_B, DV_B)

    def weights(l):
        return (ln_pre[l], w_in[l], b_in[l], conv_w[l], conv_b[l], conv_ln_g[l], conv_ln_b[l],
                conv_pw[l], lam_qk[l], attn_g[l], f_bias[l], mlstm_g[l], w_out[l], ln_post[l])

    def prompt_attn(q, k, v, z, lq, g, lam_init, kb, vt):
        return _attn_prompt(q, kb, vt, z, lq, g, lam_init)

    prompt_chunk = _pick(tp, (128, 64, 32, 16, 8))
    xp = x_prompt
    outs_p = [[] for _ in range(6)]
    for l in range(depth):
        buf0 = jnp.zeros((bp, CONV_W - 1, C_A), F32)
        m0 = (jnp.zeros((bp, H_C, DK_C, DV_C), F32), jnp.zeros((bp, H_C, DK_C), F32),
              jnp.zeros((bp, H_C), F32))
        xp, kr, vr, cb, (cm, nv, mv) = _layer(xp, l, buf0, m0, prompt_attn, True, weights(l),
                                              prompt_chunk)
        for lst, val in zip(outs_p, (kr, vr, cb, cm, nv, mv)):
            lst.append(val)

    xs = x_sample
    outs_s = [[] for _ in range(6)]
    for l in range(depth):
        def sample_attn(q, k, v, z, lq, g, lam_init, l=l):
            return _attn_decode(q.reshape(bs, ts, D_B), k.reshape(bs, ts, D_B), v.reshape(bs, ts, D_B),
                                z.reshape(bs, ts, D_B), ck, cv, page_table, lq, g, lam_init, l)

        st = (state_C[l], state_n[l], state_m[l])
        xs, kr, vr, cb, (cm, nv, mv) = _layer(xs, l, state_conv[l], st, sample_attn, False,
                                              weights(l), 8)
        for lst, val in zip(outs_s, (kr, vr, cb, cm, nv, mv)):
            lst.append(val)

    return (xp, xs, *[jnp.stack(o) for o in outs_p], *[jnp.stack(o) for o in outs_s])
```

```python
import functools
import math

import jax
import jax.numpy as jnp
from jax import lax
from jax.experimental import pallas as pl
from jax.experimental.pallas import tpu as pltpu

F32 = jnp.float32
BF16 = jnp.bfloat16

C_A = 256
CONV_W = 31
H_B = 4
DK_B = 64
DV_B = 128
D_B = H_B * DV_B
H_C = 4
DK_C = 64
DV_C = 64
D_C = H_C * DV_C

OFF_A = 0
OFF_Q = 3 * C_A
OFF_K = OFF_Q + 2 * H_B * DK_B
OFF_V = OFF_K + 2 * H_B * DK_B
OFF_Z = OFF_V + D_B
OFF_C = OFF_Z + D_B
OFF_G = OFF_C + 5 * D_C
N_MAIN = OFF_G
N_GATE = 2 * H_C

HIST = 32
NEG = -1e30
QK_SCALE = DK_B ** -0.5
KC_SCALE = DK_C ** -0.5
LOG2E = math.log2(math.e)

VMEM_LIMIT_BYTES = 48 * 1024 * 1024


def _cparams(n_axes):
    return pltpu.CompilerParams(dimension_semantics=("arbitrary",) * n_axes,
                                vmem_limit_bytes=VMEM_LIMIT_BYTES)


def _pick(n, candidates):
    for c in candidates:
        if n % c == 0:
            return c
    return n


def _sigmoid(x):
    return 1.0 / (1.0 + jnp.exp(-x))


def _silu(x):
    return x * _sigmoid(x)


def _log_sigmoid(x):
    return jnp.minimum(x, 0.0) - jnp.log(1.0 + jnp.exp(-jnp.abs(x)))


def _inproj_body(x_ref, g_ref, w_ref, b_ref, wg_ref, wgt_ref, bgc_ref, bgr_ref, *rest, flash_operands):
    a_ref, q_ref, k_ref, v_ref, z_ref, c_ref, gc_ref, gr_ref = rest[:8]
    xf = x_ref[...]
    h = xf * lax.rsqrt(jnp.mean(xf * xf, axis=-1, keepdims=True) + 1e-6) * g_ref[...]
    hb = h.astype(BF16)
    rows_last = (((1,), (1,)), ((), ()))

    def seg(lo, hi):
        return jnp.dot(hb, w_ref[:, lo:hi], preferred_element_type=F32) + b_ref[:, lo:hi]

    a_ref[...] = seg(OFF_A, OFF_Q)
    q_scale = QK_SCALE * LOG2E if flash_operands else QK_SCALE
    q_ref[...] = (seg(OFF_Q, OFF_K) * q_scale).astype(BF16)
    kk = seg(OFF_K, OFF_V)
    vv = seg(OFF_V, OFF_Z)
    tm = kk.shape[0]
    for h in range(H_B):
        k_ref[pl.ds(h, tm, stride=H_B), :] = kk[:, h * DV_B:(h + 1) * DV_B]
        v_ref[pl.ds(h, tm, stride=H_B), :] = vv[:, h * DV_B:(h + 1) * DV_B]
    z_ref[...] = seg(OFF_Z, OFF_C)
    c_ref[...] = seg(OFF_C, OFF_G)
    gc_ref[...] = jnp.dot(hb, wg_ref[...], preferred_element_type=F32) + bgc_ref[...]
    gr_ref[...] = lax.dot_general(wgt_ref[...], hb, rows_last, preferred_element_type=F32) + bgr_ref[...]
    if flash_operands:
        kb_ref, vt_ref = rest[8:]
        kb_ref[...] = kk.astype(BF16)
        vt_ref[...] = vv.T.astype(BF16)


def _inproj(x2d, ln_pre, w_in, b_in, flash_operands):
    rows, d = x2d.shape
    tm = _pick(rows, (256, 128))
    w_main = w_in[:, :N_MAIN].astype(BF16)
    w_gate = w_in[:, N_MAIN:].astype(BF16)
    b_main = b_in[:N_MAIN].reshape(1, N_MAIN)
    b_gate = b_in[N_MAIN:]
    row = lambda i: (i, 0)
    col = lambda i: (0, i)
    fixed = lambda i: (0, 0)
    head_rows = jax.ShapeDtypeStruct((rows * H_B, DV_B), F32)
    head_spec = pl.BlockSpec((tm * H_B, DV_B), row)
    out_shape = [jax.ShapeDtypeStruct((rows, OFF_Q - OFF_A), F32),
                 jax.ShapeDtypeStruct((rows, OFF_K - OFF_Q), BF16),
                 head_rows, head_rows,
                 jax.ShapeDtypeStruct((rows, OFF_C - OFF_Z), F32),
                 jax.ShapeDtypeStruct((rows, OFF_G - OFF_C), F32)]
    out_specs = [pl.BlockSpec((tm, OFF_Q - OFF_A), row), pl.BlockSpec((tm, OFF_K - OFF_Q), row),
                 head_spec, head_spec,
                 pl.BlockSpec((tm, OFF_C - OFF_Z), row), pl.BlockSpec((tm, OFF_G - OFF_C), row)]
    out_shape += [jax.ShapeDtypeStruct((rows, N_GATE), F32), jax.ShapeDtypeStruct((N_GATE, rows), F32)]
    out_specs += [pl.BlockSpec((tm, N_GATE), row), pl.BlockSpec((N_GATE, tm), col)]
    in_specs = [pl.BlockSpec((tm, d), row),
                pl.BlockSpec((1, d), fixed),
                pl.BlockSpec((d, N_MAIN), fixed),
                pl.BlockSpec((1, N_MAIN), fixed),
                pl.BlockSpec((d, N_GATE), fixed),
                pl.BlockSpec((N_GATE, d), fixed),
                pl.BlockSpec((1, N_GATE), fixed),
                pl.BlockSpec((N_GATE, 1), fixed)]
    args = [x2d, ln_pre.reshape(1, d), w_main, b_main, w_gate, w_gate.T,
            b_gate.reshape(1, N_GATE), b_gate.reshape(N_GATE, 1)]
    if flash_operands:
        out_shape += [jax.ShapeDtypeStruct((rows, D_B), BF16), jax.ShapeDtypeStruct((D_B, rows), BF16)]
        out_specs += [pl.BlockSpec((tm, D_B), row), pl.BlockSpec((D_B, tm), col)]
    return pl.pallas_call(
        functools.partial(_inproj_body, flash_operands=flash_operands),
        grid=(rows // tm,),
        in_specs=in_specs,
        out_specs=out_specs,
        out_shape=out_shape,
        compiler_params=_cparams(1),
        name="inproj",
    )(*args)


SUBLANES = 8


def _conv_body(a_ref, st_ref, cw_ref, cb_ref, lg_ref, lb_ref, y_ref, hist_ref, fbuf, shifted,
               *, tt, rc):
    t = pl.program_id(1)

    @pl.when(t == 0)
    def _():
        fbuf[0:HIST, :] = st_ref[0]

    a = a_ref[0]
    fbuf[HIST:HIST + tt, :] = a[:, 0:C_A] * _sigmoid(a[:, C_A:2 * C_A])
    n_rows = shifted.shape[1]
    for k in range(1, SUBLANES):
        shifted[k] = fbuf[pl.ds(k, n_rows), :]
    first = HIST - (CONV_W - 1)
    for r0 in range(0, tt, rc):
        acc = jnp.zeros((rc, C_A), F32)
        for j in range(CONV_W):
            k = (first + j) % SUBLANES
            base = first + j - k + r0
            win = fbuf[base:base + rc, :] if k == 0 else shifted[k, base:base + rc, :]
            acc = acc + cw_ref[j:j + 1, :] * win
        y = acc + cb_ref[...]
        mu = jnp.mean(y, axis=-1, keepdims=True)
        yc = y - mu
        yn = yc * lax.rsqrt(jnp.mean(yc * yc, axis=-1, keepdims=True) + 1e-5)
        y_ref[0, r0:r0 + rc, :] = _silu(yn * lg_ref[...] + lb_ref[...]).astype(y_ref.dtype)
    new_hist = fbuf[pl.ds(tt, HIST), :]
    hist_ref[0] = new_hist
    fbuf[0:HIST, :] = new_hist


def _conv(a3d, state, conv_w, conv_b, ln_g, ln_b):
    bsz, t_len, _ = a3d.shape
    tt = _pick(t_len, (256, 128, 64, 32, 16, 8))
    rc = min(tt, 64)
    buf_rows = HIST + max(tt, SUBLANES)
    st = jnp.pad(state, ((0, 0), (HIST - (CONV_W - 1), 0), (0, 0)))
    fixed = lambda b, t: (0, 0)
    y, hist = pl.pallas_call(
        functools.partial(_conv_body, tt=tt, rc=rc),
        grid=(bsz, t_len // tt),
        in_specs=[pl.BlockSpec((1, tt, 3 * C_A), lambda b, t: (b, t, 0)),
                  pl.BlockSpec((1, HIST, C_A), lambda b, t: (b, 0, 0)),
                  pl.BlockSpec((CONV_W, C_A), fixed),
                  pl.BlockSpec((1, C_A), fixed),
                  pl.BlockSpec((1, C_A), fixed),
                  pl.BlockSpec((1, C_A), fixed)],
        out_specs=[pl.BlockSpec((1, tt, C_A), lambda b, t: (b, t, 0)),
                   pl.BlockSpec((1, HIST, C_A), lambda b, t: (b, 0, 0))],
        out_shape=[jax.ShapeDtypeStruct((bsz, t_len, C_A), BF16),
                   jax.ShapeDtypeStruct((bsz, HIST, C_A), F32)],
        scratch_shapes=[pltpu.VMEM((buf_rows, C_A), F32),
                        pltpu.VMEM((SUBLANES, buf_rows - SUBLANES, C_A), F32)],
        compiler_params=_cparams(2),
        name="conv",
    )(a3d, st, conv_w, conv_b.reshape(1, C_A), ln_g.reshape(1, C_A), ln_b.reshape(1, C_A))
    return y, hist[:, HIST - (CONV_W - 1):, :]


def _lambda(lq, lam_init):
    s01 = jnp.sum(lq[0:1, :] * lq[1:2, :], axis=-1, keepdims=True)
    s23 = jnp.sum(lq[2:3, :] * lq[3:4, :], axis=-1, keepdims=True)
    return jnp.exp(s01) - jnp.exp(s23) + lam_init


def _attn_finish(o, g, z, lam_init):
    y = o * lax.rsqrt(jnp.mean(o * o, axis=-1, keepdims=True) + 1e-6) * g * (1.0 - lam_init)
    return y * _silu(z)


N_MAPS = 2 * H_B
SUM_ROWS = 16


def _attn_body(it_ref, jt_ref, q_ref, k_ref, vt_ref, z_ref, lq_ref, g_ref, o_ref,
               m_scr, acc_scr, *, blk, kb_step, lam_init):
    step_id = pl.program_id(0)
    i = it_ref[step_id]
    jj = jt_ref[step_id]

    @pl.when(jj == 0)
    def _():
        m_scr[...] = jnp.full(m_scr.shape, NEG, F32)
        acc_scr[...] = jnp.zeros(acc_scr.shape, F32)

    def sweep(diagonal, sub):
        keys = pl.ds(pl.multiple_of(sub * blk, blk), blk)
        if diagonal:
            kpos = lax.broadcasted_iota(jnp.int32, (blk, blk), 0)
            qpos = lax.broadcasted_iota(jnp.int32, (blk, blk), 1)
            keep = kpos <= qpos
        lane = lax.broadcasted_iota(jnp.int32, (blk, DV_B), 1)
        ones = jnp.ones((SUM_ROWS, blk), BF16)

        def scores(r):
            h, c = divmod(r, 2)
            hs = slice(h * DV_B, (h + 1) * DV_B)
            qh = q_ref[:, hs]
            qc = jnp.where((lane < DK_B) == (c == 0), qh, jnp.zeros_like(qh))
            st = lax.dot_general(k_ref[keys, hs], qc, (((1,), (1,)), ((), ())),
                                 preferred_element_type=F32)
            if diagonal:
                st = jnp.where(keep, st, NEG)
            m_prev = m_scr[r:r + 1, :]
            m_new = jnp.maximum(m_prev, jnp.max(st, axis=0, keepdims=True))
            return st, m_prev, m_new

        def values(r, pt, alpha):
            h = r // 2
            vth = jnp.concatenate([vt_ref[h * DV_B:(h + 1) * DV_B, keys], ones], axis=0)
            acc_scr[r] = alpha * acc_scr[r] + jnp.dot(vth, pt, preferred_element_type=F32)

        nxt = scores(0)
        pending = None
        for r in range(N_MAPS):
            st, m_prev, m_new = nxt
            if r + 1 < N_MAPS:
                nxt = scores(r + 1)
            alpha = jnp.exp2(m_prev - m_new)
            pt = jnp.exp2(st - m_new).astype(BF16)
            m_scr[r:r + 1, :] = m_new
            if pending is not None:
                values(*pending)
            pending = (r, pt, alpha)
        values(*pending)

    first = jj * kb_step
    n_below = jnp.clip(i - first, 0, kb_step)

    def below(sub, carry):
        sweep(False, sub)
        return carry

    lax.fori_loop(0, n_below, below, 0)

    @pl.when(i - first < kb_step)
    def _():
        sweep(True, i - first)
        lam = _lambda(lq_ref[...], lam_init)
        for h in range(H_B):
            hs = slice(h * DV_B, (h + 1) * DV_B)
            r = 2 * h
            a1 = acc_scr[r, 0:DV_B, :] * (1.0 / acc_scr[r, DV_B:DV_B + 1, :])
            a2 = acc_scr[r + 1, 0:DV_B, :] * (1.0 / acc_scr[r + 1, DV_B:DV_B + 1, :])
            ot = a1 - lam * a2
            yt = ot * lax.rsqrt(jnp.mean(ot * ot, axis=0, keepdims=True) + 1e-6) * g_ref[...]
            o_ref[:, hs] = (yt.T * (1.0 - lam_init) * _silu(z_ref[:, hs])).astype(o_ref.dtype)


def _attn_prompt(q, kb, vt, z, lam_qk, attn_g, lam_init):
    t_len = q.shape[0]
    blk = _pick(t_len, (1024, 512, 256, 128))
    nb = t_len // blk
    kb_step = _pick(nb, (2, 1))
    pairs = [(i, jj) for i in range(nb) for jj in range(i // kb_step + 1)]
    i_tab = jnp.asarray([p[0] for p in pairs], jnp.int32)
    j_tab = jnp.asarray([p[1] for p in pairs], jnp.int32)
    q_map = lambda s, it, jt: (it[s], 0)
    k_map = lambda s, it, jt: (jt[s], 0)
    vt_map = lambda s, it, jt: (0, jt[s])
    fixed = lambda s, it, jt: (0, 0)
    return pl.pallas_call(
        functools.partial(_attn_body, blk=blk, kb_step=kb_step, lam_init=lam_init),
        grid_spec=pltpu.PrefetchScalarGridSpec(
            num_scalar_prefetch=2,
            grid=(len(pairs),),
            in_specs=[pl.BlockSpec((blk, D_B), q_map),
                      pl.BlockSpec((kb_step * blk, D_B), k_map),
                      pl.BlockSpec((D_B, kb_step * blk), vt_map),
                      pl.BlockSpec((blk, D_B), q_map),
                      pl.BlockSpec((4, DK_B), fixed),
                      pl.BlockSpec((DV_B, 1), fixed)],
            out_specs=pl.BlockSpec((blk, D_B), q_map),
            scratch_shapes=[pltpu.VMEM((N_MAPS, blk), F32),
                            pltpu.VMEM((N_MAPS, DV_B + SUM_ROWS, blk), F32)]),
        out_shape=jax.ShapeDtypeStruct((t_len, D_B), BF16),
        compiler_params=_cparams(1),
        name="attn_prompt",
    )(i_tab, j_tab, q, kb, vt, z, lam_qk, attn_g.reshape(DV_B, 1))


QROWS = 16


QR = H_B * QROWS


def _decode_body(pt_ref, q_ref, kn_ref, vn_ref, z_ref, lq_ref, g_ref, *rest,
                 n_pages_step, t_new, lam_init):
    k_refs = rest[:n_pages_step]
    v_refs = rest[n_pages_step:2 * n_pages_step]
    o_ref, m_scr, l_scr, acc_scr = rest[2 * n_pages_step:]
    g = pl.program_id(1)
    n_groups = pl.num_programs(1)

    @pl.when(g == 0)
    def _():
        m_scr[...] = jnp.full(m_scr.shape, NEG, F32)
        l_scr[...] = jnp.zeros(l_scr.shape, F32)
        acc_scr[...] = jnp.zeros(acc_scr.shape, F32)

    q64 = q_ref[0]
    row = lax.broadcasted_iota(jnp.int32, q64.shape, 0)
    lane = lax.broadcasted_iota(jnp.int32, q64.shape, 1)
    qd = jnp.where((lane < DK_B) == ((row % QROWS) < QROWS // 2), q64, jnp.zeros_like(q64))

    page = k_refs[0].shape[0] // H_B

    def head_rows(refs, h):
        return jnp.concatenate([r[pl.ds(h, page, stride=H_B), :] for r in refs], axis=0).astype(BF16)

    heads = range(H_B)
    rows = [slice(h * QROWS, (h + 1) * QROWS) for h in heads]
    s = [lax.dot_general(qd[rows[h], :], head_rows(k_refs, h), (((1,), (1,)), ((), ())),
                         preferred_element_type=F32) for h in heads]
    m_prev = [m_scr[rows[h], :] for h in heads]
    m_new = [jnp.maximum(m_prev[h], jnp.max(s[h], axis=-1, keepdims=True)) for h in heads]
    alpha = [jnp.exp(m_prev[h] - m_new[h]) for h in heads]
    p = [jnp.exp(s[h] - m_new[h]) for h in heads]
    pv = [jnp.dot(p[h].astype(BF16), head_rows(v_refs, h), preferred_element_type=F32) for h in heads]
    for h in heads:
        l_scr[rows[h], :] = alpha[h] * l_scr[rows[h], :] + jnp.sum(p[h], axis=-1, keepdims=True)
        acc_scr[rows[h], :] = alpha[h] * acc_scr[rows[h], :] + pv[h]
        m_scr[rows[h], :] = m_new[h]

    @pl.when(g == n_groups - 1)
    def _():
        lam = _lambda(lq_ref[...], lam_init)
        qf = qd.astype(F32)
        tok = lax.broadcasted_iota(jnp.int32, (QR, 1), 0) % (QROWS // 2)
        s_new = []
        for jn in range(t_new):
            sj = jnp.sum(qf * kn_ref[0, jn].astype(BF16).astype(F32), axis=-1, keepdims=True)
            s_new.append(jnp.where(tok >= jn, sj, NEG))
        m_prev = m_scr[...]
        m_fin = m_prev
        for sj in s_new:
            m_fin = jnp.maximum(m_fin, sj)
        alpha = jnp.exp(m_prev - m_fin)
        l_fin = alpha * l_scr[...]
        acc = alpha * acc_scr[...]
        for jn, sj in enumerate(s_new):
            pj = jnp.exp(sj - m_fin)
            l_fin = l_fin + pj
            acc = acc + pj.astype(BF16).astype(F32) * vn_ref[0, jn].astype(BF16).astype(F32)
        a = acc / l_fin
        half = QROWS // 2
        for h in range(H_B):
            hs = slice(h * DV_B, (h + 1) * DV_B)
            o = a[h * QROWS:h * QROWS + half, :] - lam * a[h * QROWS + half:(h + 1) * QROWS, :]
            o_ref[0, :, hs] = _attn_finish(o, g_ref[...], z_ref[0, :, hs], lam_init)


def _attn_decode(q, k_new, v_new, z, cache_k_l, cache_v_l, page_table, lam_qk, attn_g, lam_init,
                 layer):
    bsz, t_new, _ = q.shape
    n_pages = page_table.shape[1]
    page_rows = cache_k_l.shape[2]
    pps = _pick(n_pages, (16, 8, 4, 2, 1))
    half = QROWS // 2
    qh = jnp.swapaxes(q.reshape(bsz, t_new, H_B, DV_B), 1, 2)
    q64 = jnp.zeros((bsz, H_B, QROWS, DV_B), BF16)
    q64 = q64.at[:, :, 0:t_new].set(qh).at[:, :, half:half + t_new].set(qh).reshape(bsz, QR, DV_B)
    on_rows = lambda x: jnp.repeat(x.reshape(bsz, t_new, H_B, DV_B), QROWS, axis=2)
    z8 = jnp.pad(z, ((0, 0), (0, half - t_new), (0, 0)))
    fixed = lambda b, g, pt: (0, 0)
    per_b = lambda b, g, pt: (b, 0, 0)
    per_b4 = lambda b, g, pt: (b, 0, 0, 0)

    def page_spec(jp):
        return pl.BlockSpec((None, None, page_rows, DV_B),
                            lambda b, g, pt: (layer, pt[b, g * pps + jp], 0, 0))

    in_specs = [pl.BlockSpec((1, QR, DV_B), per_b),
                pl.BlockSpec((1, t_new, QR, DV_B), per_b4),
                pl.BlockSpec((1, t_new, QR, DV_B), per_b4),
                pl.BlockSpec((1, half, D_B), per_b),
                pl.BlockSpec((4, DK_B), fixed),
                pl.BlockSpec((1, DV_B), fixed)]
    in_specs += [page_spec(jp) for jp in range(pps)] * 2
    out = pl.pallas_call(
        functools.partial(_decode_body, n_pages_step=pps, t_new=t_new, lam_init=lam_init),
        grid_spec=pltpu.PrefetchScalarGridSpec(
            num_scalar_prefetch=1,
            grid=(bsz, n_pages // pps),
            in_specs=in_specs,
            out_specs=pl.BlockSpec((1, half, D_B), per_b),
            scratch_shapes=[pltpu.VMEM((QR, 1), F32), pltpu.VMEM((QR, 1), F32),
                            pltpu.VMEM((QR, DV_B), F32)]),
        out_shape=jax.ShapeDtypeStruct((bsz, half, D_B), F32),
        compiler_params=_cparams(2),
        name="attn_decode",
    )(page_table, q64, on_rows(k_new), on_rows(v_new), z8, lam_qk, attn_g.reshape(1, DV_B),
      *([cache_k_l] * pps), *([cache_v_l] * pps))
    return out[:, :t_new, :]


def _mlstm_body(c_ref, gc_ref, gr_ref, fbc_ref, fbr_ref, mg_ref, c0_ref, n0_ref, m0_ref,
                y_ref, cn_ref, nn_ref, mn_ref, c_scr, n_scr, m_scr, *, ln, t_valid, t_pad):
    t = pl.program_id(1)
    n_t = pl.num_programs(1)

    @pl.when(t == 0)
    def _():
        c_scr[...] = c0_ref[0]
        n_scr[...] = n0_ref[0]
        m_scr[...] = m0_ref[0]

    cblk = c_ref[0]
    gcol = gc_ref[0]
    grow = gr_ref[0]
    is_f_c = lax.broadcasted_iota(jnp.int32, (ln, N_GATE), 1) >= H_C
    is_f_r = lax.broadcasted_iota(jnp.int32, (N_GATE, ln), 0) >= H_C
    g_c = jnp.where(is_f_c, _log_sigmoid(gcol + fbc_ref[...]), gcol)
    g_r = jnp.where(is_f_r, _log_sigmoid(grow + fbr_ref[...]), grow)
    if t_pad != t_valid:
        ok_c = (t * ln + lax.broadcasted_iota(jnp.int32, (ln, N_GATE), 0)) < t_valid
        ok_r = (t * ln + lax.broadcasted_iota(jnp.int32, (N_GATE, ln), 1)) < t_valid
        g_c = jnp.where(ok_c, g_c, jnp.where(is_f_c, 0.0, NEG))
        g_r = jnp.where(ok_r, g_r, jnp.where(is_f_r, 0.0, NEG))
    rr = lax.broadcasted_iota(jnp.int32, (ln, ln), 0)
    cc = lax.broadcasted_iota(jnp.int32, (ln, ln), 1)
    causal = rr >= cc
    tri = causal.astype(F32)
    cum_c = jnp.dot(tri, jnp.where(is_f_c, g_c, 0.0), preferred_element_type=F32,
                    precision=lax.Precision.HIGHEST)
    cum_r = lax.dot_general(jnp.where(is_f_r, g_r, 0.0), tri, (((1,), (1,)), ((), ())),
                            preferred_element_type=F32, precision=lax.Precision.HIGHEST)
    m_all = m_scr[...]
    lane = lax.broadcasted_iota(jnp.int32, (1, 128), 1)
    m_out = jnp.zeros((1, 128), F32)

    for h in range(H_C):
        qf = cblk[:, h * DK_C:(h + 1) * DK_C]
        kf = cblk[:, D_C + h * DK_C:D_C + (h + 1) * DK_C] * KC_SCALE
        vf = cblk[:, 2 * D_C + h * DV_C:2 * D_C + (h + 1) * DV_C]
        of = cblk[:, 3 * D_C + h * DV_C:3 * D_C + (h + 1) * DV_C]
        zf = cblk[:, 4 * D_C + h * DV_C:4 * D_C + (h + 1) * DV_C]
        qb, kb, vb = qf.astype(BF16), kf.astype(BF16), vf.astype(BF16)
        bc = cum_c[:, H_C + h:H_C + h + 1]
        igc = g_c[:, h:h + 1]
        br = cum_r[H_C + h:H_C + h + 1, :]
        igr = g_r[h:h + 1, :]
        m_prev = m_all[:, h:h + 1]
        c_mat = c_scr[h]
        n_vec = n_scr[h]

        log_d = jnp.where(causal, bc - br + igr, NEG)
        inter = bc + m_prev
        m_t = jnp.maximum(inter, jnp.max(log_d, axis=-1, keepdims=True))
        d_w = jnp.where(causal, jnp.exp(log_d - m_t), 0.0)
        i_w = jnp.exp(inter - m_t)
        qk = lax.dot_general(qb, kb, (((1,), (1,)), ((), ())), preferred_element_type=F32)
        w = d_w * qk
        num = (jnp.dot(w.astype(BF16), vb, preferred_element_type=F32)
               + i_w * jnp.dot(qb, c_mat.astype(BF16), preferred_element_type=F32))
        den = jnp.sum(w, axis=-1, keepdims=True) + i_w * jnp.sum(qf * n_vec, axis=-1, keepdims=True)
        hh = num / jnp.maximum(jnp.abs(den), jnp.exp(-m_t))

        b_last = bc[ln - 1:ln, :]
        m_last = m_t[ln - 1:ln, :]
        i_last = i_w[ln - 1:ln, :]
        w_last = jnp.exp(b_last - bc + igc - m_last)
        kw = w_last * kf
        c_scr[h] = i_last * c_mat + lax.dot_general(kw.astype(BF16), vb, (((0,), (0,)), ((), ())),
                                                    preferred_element_type=F32)
        n_scr[h] = i_last * n_vec + jnp.sum(kw, axis=0, keepdims=True)
        m_out = jnp.where(lane == h, m_last, m_out)

        hg = _sigmoid(of) * hh
        yh = hg * lax.rsqrt(jnp.mean(hg * hg, axis=-1, keepdims=True) + 1e-6)
        yh = yh * mg_ref[:, h * DV_C:(h + 1) * DV_C]
        y_ref[0, :, h * DV_C:(h + 1) * DV_C] = (yh * _silu(zf)).astype(y_ref.dtype)

    m_scr[...] = m_out

    @pl.when(t == n_t - 1)
    def _():
        cn_ref[0] = c_scr[...]
        nn_ref[0] = n_scr[...]
        mn_ref[0] = m_scr[...]


def _mlstm(c3d, gcol, grow, f_bias, mlstm_g, c0, n0, m0, ln):
    bsz, t_valid, _ = c3d.shape
    t_pad = -(-t_valid // ln) * ln
    if t_pad != t_valid:
        extra = t_pad - t_valid
        c3d = jnp.pad(c3d, ((0, 0), (0, extra), (0, 0)))
        gcol = jnp.pad(gcol, ((0, 0), (0, extra), (0, 0)))
        grow = jnp.pad(grow, ((0, 0), (0, 0), (0, extra)))
    m0p = jnp.pad(m0, ((0, 0), (0, 128 - H_C))).reshape(bsz, 1, 128)
    n0r = n0.reshape(bsz, H_C, 1, DK_C)
    fb8 = jnp.concatenate([jnp.zeros((H_C,), F32), f_bias])
    fixed = lambda b, t: (0, 0)
    y, cn, nn, mn = pl.pallas_call(
        functools.partial(_mlstm_body, ln=ln, t_valid=t_valid, t_pad=t_pad),
        grid=(bsz, t_pad // ln),
        in_specs=[pl.BlockSpec((1, ln, 5 * D_C), lambda b, t: (b, t, 0)),
                  pl.BlockSpec((1, ln, N_GATE), lambda b, t: (b, t, 0)),
                  pl.BlockSpec((1, N_GATE, ln), lambda b, t: (b, 0, t)),
                  pl.BlockSpec((1, N_GATE), fixed),
                  pl.BlockSpec((N_GATE, 1), fixed),
                  pl.BlockSpec((1, D_C), fixed),
                  pl.BlockSpec((1, H_C, DK_C, DV_C), lambda b, t: (b, 0, 0, 0)),
                  pl.BlockSpec((1, H_C, 1, DK_C), lambda b, t: (b, 0, 0, 0)),
                  pl.BlockSpec((1, 1, 128), lambda b, t: (b, 0, 0))],
        out_specs=[pl.BlockSpec((1, ln, D_C), lambda b, t: (b, t, 0)),
                   pl.BlockSpec((1, H_C, DK_C, DV_C), lambda b, t: (b, 0, 0, 0)),
                   pl.BlockSpec((1, H_C, 1, DK_C), lambda b, t: (b, 0, 0, 0)),
                   pl.BlockSpec((1, 1, 128), lambda b, t: (b, 0, 0))],
        out_shape=[jax.ShapeDtypeStruct((bsz, t_pad, D_C), BF16),
                   jax.ShapeDtypeStruct((bsz, H_C, DK_C, DV_C), F32),
                   jax.ShapeDtypeStruct((bsz, H_C, 1, DK_C), F32),
                   jax.ShapeDtypeStruct((bsz, 1, 128), F32)],
        scratch_shapes=[pltpu.VMEM((H_C, DK_C, DV_C), F32), pltpu.VMEM((H_C, 1, DK_C), F32),
                        pltpu.VMEM((1, 128), F32)],
        compiler_params=_cparams(2),
        name="mlstm",
    )(c3d, gcol, grow, fb8.reshape(1, N_GATE), fb8.reshape(N_GATE, 1), mlstm_g.reshape(1, D_C),
      c0, n0r, m0p)
    return y[:, :t_valid], cn, nn.reshape(bsz, H_C, DK_C), mn[:, 0, :H_C]


def _outproj_body(x_ref, ya_ref, az_ref, yb_ref, yc_ref, pw_ref, w_ref, g_ref, o_ref):
    ya = jnp.dot(ya_ref[...].astype(BF16), pw_ref[...], preferred_element_type=F32)
    ya = ya * _silu(az_ref[...])
    out = jnp.dot(ya.astype(BF16), w_ref[0:C_A, :], preferred_element_type=F32)
    out = out + jnp.dot(yb_ref[...].astype(BF16), w_ref[C_A:C_A + D_B, :], preferred_element_type=F32)
    out = out + jnp.dot(yc_ref[...].astype(BF16), w_ref[C_A + D_B:, :], preferred_element_type=F32)
    y = out * lax.rsqrt(jnp.mean(out * out, axis=-1, keepdims=True) + 1e-6) * g_ref[...]
    o_ref[...] = x_ref[...] + y


def _outproj(x2d, ya, a2d, yb, yc, conv_pw, w_out, ln_post):
    rows, d = x2d.shape
    tm = _pick(rows, (512, 256, 128))
    row = lambda i: (i, 0)
    fixed = lambda i: (0, 0)
    return pl.pallas_call(
        _outproj_body,
        grid=(rows // tm,),
        in_specs=[pl.BlockSpec((tm, d), row),
                  pl.BlockSpec((tm, C_A), row),
                  pl.BlockSpec((tm, C_A), lambda i: (i, 2)),
                  pl.BlockSpec((tm, D_B), row),
                  pl.BlockSpec((tm, D_C), row),
                  pl.BlockSpec((C_A, C_A), fixed),
                  pl.BlockSpec((C_A + D_B + D_C, d), fixed),
                  pl.BlockSpec((1, d), fixed)],
        out_specs=pl.BlockSpec((tm, d), row),
        out_shape=jax.ShapeDtypeStruct((rows, d), F32),
        compiler_params=_cparams(1),
        name="outproj",
    )(x2d, ya, a2d, yb, yc, conv_pw.astype(BF16), w_out.astype(BF16), ln_post.reshape(1, d))


def _layer(x, lidx, conv_state, mstate, attn, flash_operands, weights, mlstm_chunk):
    (ln_pre, w_in, b_in, conv_w, conv_b, conv_ln_g, conv_ln_b, conv_pw, lam_qk, attn_g, f_bias,
     mlstm_g, w_out, ln_post) = weights
    bsz, t_len, d = x.shape
    rows = bsz * t_len
    lam_init = 0.8 - 0.6 * math.exp(-0.3 * lidx)
    x2d = x.reshape(rows, d)
    a, q, k, v, z, c, gcol, grow, *flash = _inproj(x2d, ln_pre, w_in, b_in, flash_operands)

    ya, conv_new = _conv(a.reshape(bsz, t_len, 3 * C_A), conv_state, conv_w, conv_b, conv_ln_g,
                         conv_ln_b)
    yb = attn(q, k, v, z, lam_qk, attn_g, lam_init, *flash)
    grow3 = jnp.swapaxes(grow.reshape(N_GATE, bsz, t_len), 0, 1)
    yc, c_new, n_new, m_new = _mlstm(c.reshape(bsz, t_len, 5 * D_C), gcol.reshape(bsz, t_len, N_GATE),
                                     grow3, f_bias, mlstm_g, *mstate, mlstm_chunk)
    x_new = _outproj(x2d, ya.reshape(rows, C_A), a, yb.reshape(rows, D_B), yc.reshape(rows, D_C),
                     conv_pw, w_out, ln_post)
    return (x_new.reshape(bsz, t_len, d), k.reshape(bsz, t_len, H_B, 2 * DK_B),
            v.reshape(bsz, t_len, H_B, DV_B), conv_new, (c_new, n_new, m_new))


def kernel(x_prompt, x_sample, cache_k, cache_v, state_conv, state_C, state_n, state_m, page_table,
           ln_pre, w_in, b_in, conv_w, conv_b, conv_ln_g, conv_ln_b, conv_pw, lam_qk, attn_g, f_bias,
           mlstm_g, w_out, ln_post):
    depth = w_in.shape[0]
    bp, tp, _ = x_prompt.shape
    bs, ts, _ = x_sample.shape
    assert bp == 1, "prompt attention sweep is written for a single prompt sequence"
    n_pool, page = cache_k.shape[1], cache_k.shape[2]
    ck = cache_k.reshape(depth, n_pool, page * H_B, DV_B)
    cv = cache_v.reshape(depth, n_pool, page * H_B, DV_B)

    def weights(l):
        return (ln_pre[l], w_in[l], b_in[l], conv_w[l], conv_b[l], conv_ln_g[l], conv_ln_b[l],
                conv_pw[l], lam_qk[l], attn_g[l], f_bias[l], mlstm_g[l], w_out[l], ln_post[l])

    def prompt_attn(q, k, v, z, lq, g, lam_init, kb, vt):
        return _attn_prompt(q, kb, vt, z, lq, g, lam_init)

    prompt_chunk = _pick(tp, (128, 64, 32, 16, 8))
    xp = x_prompt
    outs_p = [[] for _ in range(6)]
    for l in range(depth):
        buf0 = jnp.zeros((bp, CONV_W - 1, C_A), F32)
        m0 = (jnp.zeros((bp, H_C, DK_C, DV_C), F32), jnp.zeros((bp, H_C, DK_C), F32),
              jnp.zeros((bp, H_C), F32))
        xp, kr, vr, cb, (cm, nv, mv) = _layer(xp, l, buf0, m0, prompt_attn, True, weights(l),
                                              prompt_chunk)
        for lst, val in zip(outs_p, (kr, vr, cb, cm, nv, mv)):
            lst.append(val)

    xs = x_sample
    outs_s = [[] for _ in range(6)]
    for l in range(depth):
        def sample_attn(q, k, v, z, lq, g, lam_init, l=l):
            return _attn_decode(q.reshape(bs, ts, D_B), k.reshape(bs, ts, D_B), v.reshape(bs, ts, D_B),
                                z.reshape(bs, ts, D_B), ck, cv, page_table, lq, g, lam_init, l)

        st = (state_C[l], state_n[l], state_m[l])
        xs, kr, vr, cb, (cm, nv, mv) = _layer(xs, l, state_conv[l], st, sample_attn, False,
                                              weights(l), 8)
        for lst, val in zip(outs_s, (kr, vr, cb, cm, nv, mv)):
            lst.append(val)

    return (xp, xs, *[jnp.stack(o) for o in outs_p], *[jnp.stack(o) for o in outs_s])
```

```python
import functools
import math

import jax
import jax.numpy as jnp
from jax import lax
from jax.experimental import pallas as pl
from jax.experimental.pallas import tpu as pltpu

F32 = jnp.float32
BF16 = jnp.bfloat16

C_A = 256
CONV_W = 31
H_B = 4
DK_B = 64
DV_B = 128
D_B = H_B * DV_B
H_C = 4
DK_C = 64
DV_C = 64
D_C = H_C * DV_C

OFF_A = 0
OFF_Q = 3 * C_A
OFF_K = OFF_Q + 2 * H_B * DK_B
OFF_V = OFF_K + 2 * H_B * DK_B
OFF_Z = OFF_V + D_B
OFF_C = OFF_Z + D_B
OFF_G = OFF_C + 5 * D_C
N_MAIN = OFF_G
N_GATE = 2 * H_C

HIST = 32
NEG = -1e30
QK_SCALE = DK_B ** -0.5
KC_SCALE = DK_C ** -0.5
LOG2E = math.log2(math.e)

VMEM_LIMIT_BYTES = 48 * 1024 * 1024


def _cparams(n_axes):
    return pltpu.CompilerParams(dimension_semantics=("arbitrary",) * n_axes,
                                vmem_limit_bytes=VMEM_LIMIT_BYTES)


def _pick(n, candidates):
    for c in candidates:
        if n % c == 0:
            return c
    return n


def _sigmoid(x):
    return 1.0 / (1.0 + jnp.exp(-x))


def _silu(x):
    return x * _sigmoid(x)


def _log_sigmoid(x):
    return jnp.minimum(x, 0.0) - jnp.log(1.0 + jnp.exp(-jnp.abs(x)))


def _inproj_body(x_ref, g_ref, w_ref, b_ref, wg_ref, wgt_ref, bgc_ref, bgr_ref, *rest, flash_operands):
    a_ref, q_ref, k_ref, v_ref, z_ref, c_ref, gc_ref, gr_ref = rest[:8]
    xf = x_ref[...]
    h = xf * lax.rsqrt(jnp.mean(xf * xf, axis=-1, keepdims=True) + 1e-6) * g_ref[...]
    hb = h.astype(BF16)
    rows_last = (((1,), (1,)), ((), ()))

    def seg(lo, hi):
        return jnp.dot(hb, w_ref[:, lo:hi], preferred_element_type=F32) + b_ref[:, lo:hi]

    a_ref[...] = seg(OFF_A, OFF_Q)
    q_scale = QK_SCALE * LOG2E if flash_operands else QK_SCALE
    q_ref[...] = (seg(OFF_Q, OFF_K) * q_scale).astype(BF16)
    kk = seg(OFF_K, OFF_V)
    vv = seg(OFF_V, OFF_Z)
    tm = kk.shape[0]
    for h in range(H_B):
        k_ref[pl.ds(h, tm, stride=H_B), :] = kk[:, h * DV_B:(h + 1) * DV_B]
        v_ref[pl.ds(h, tm, stride=H_B), :] = vv[:, h * DV_B:(h + 1) * DV_B]
    z_ref[...] = seg(OFF_Z, OFF_C)
    c_ref[...] = seg(OFF_C, OFF_G)
    gc_ref[...] = jnp.dot(hb, wg_ref[...], preferred_element_type=F32) + bgc_ref[...]
    gr_ref[...] = lax.dot_general(wgt_ref[...], hb, rows_last, preferred_element_type=F32) + bgr_ref[...]
    if flash_operands:
        kb_ref, vt_ref = rest[8:]
        kb_ref[...] = kk.astype(BF16)
        vt_ref[...] = vv.T.astype(BF16)


def _inproj(x2d, ln_pre, w_in, b_in, flash_operands):
    rows, d = x2d.shape
    tm = _pick(rows, (512, 256, 128))
    w_main = w_in[:, :N_MAIN].astype(BF16)
    w_gate = w_in[:, N_MAIN:].astype(BF16)
    b_main = b_in[:N_MAIN].reshape(1, N_MAIN)
    b_gate = b_in[N_MAIN:]
    row = lambda i: (i, 0)
    col = lambda i: (0, i)
    fixed = lambda i: (0, 0)
    head_rows = jax.ShapeDtypeStruct((rows * H_B, DV_B), F32)
    head_spec = pl.BlockSpec((tm * H_B, DV_B), row)
    out_shape = [jax.ShapeDtypeStruct((rows, OFF_Q - OFF_A), F32),
                 jax.ShapeDtypeStruct((rows, OFF_K - OFF_Q), BF16),
                 head_rows, head_rows,
                 jax.ShapeDtypeStruct((rows, OFF_C - OFF_Z), F32),
                 jax.ShapeDtypeStruct((rows, OFF_G - OFF_C), F32)]
    out_specs = [pl.BlockSpec((tm, OFF_Q - OFF_A), row), pl.BlockSpec((tm, OFF_K - OFF_Q), row),
                 head_spec, head_spec,
                 pl.BlockSpec((tm, OFF_C - OFF_Z), row), pl.BlockSpec((tm, OFF_G - OFF_C), row)]
    out_shape += [jax.ShapeDtypeStruct((rows, N_GATE), F32), jax.ShapeDtypeStruct((N_GATE, rows), F32)]
    out_specs += [pl.BlockSpec((tm, N_GATE), row), pl.BlockSpec((N_GATE, tm), col)]
    in_specs = [pl.BlockSpec((tm, d), row),
                pl.BlockSpec((1, d), fixed),
                pl.BlockSpec((d, N_MAIN), fixed),
                pl.BlockSpec((1, N_MAIN), fixed),
                pl.BlockSpec((d, N_GATE), fixed),
                pl.BlockSpec((N_GATE, d), fixed),
                pl.BlockSpec((1, N_GATE), fixed),
                pl.BlockSpec((N_GATE, 1), fixed)]
    args = [x2d, ln_pre.reshape(1, d), w_main, b_main, w_gate, w_gate.T,
            b_gate.reshape(1, N_GATE), b_gate.reshape(N_GATE, 1)]
    if flash_operands:
        out_shape += [jax.ShapeDtypeStruct((rows, D_B), BF16), jax.ShapeDtypeStruct((D_B, rows), BF16)]
        out_specs += [pl.BlockSpec((tm, D_B), row), pl.BlockSpec((D_B, tm), col)]
    return pl.pallas_call(
        functools.partial(_inproj_body, flash_operands=flash_operands),
        grid=(rows // tm,),
        in_specs=in_specs,
        out_specs=out_specs,
        out_shape=out_shape,
        compiler_params=_cparams(1),
        name="inproj",
    )(*args)


SUBLANES = 8


def _conv_body(a_ref, st_ref, cw_ref, cb_ref, lg_ref, lb_ref, y_ref, hist_ref, fbuf, shifted,
               *, tt, rc):
    t = pl.program_id(1)

    @pl.when(t == 0)
    def _():
        fbuf[0:HIST, :] = st_ref[0]

    a = a_ref[0]
    fbuf[HIST:HIST + tt, :] = a[:, 0:C_A] * _sigmoid(a[:, C_A:2 * C_A])
    n_rows = shifted.shape[1]
    for k in range(1, SUBLANES):
        shifted[k] = fbuf[pl.ds(k, n_rows), :]
    first = HIST - (CONV_W - 1)
    for r0 in range(0, tt, rc):
        acc = jnp.zeros((rc, C_A), F32)
        for j in range(CONV_W):
            k = (first + j) % SUBLANES
            base = first + j - k + r0
            win = fbuf[base:base + rc, :] if k == 0 else shifted[k, base:base + rc, :]
            acc = acc + cw_ref[j:j + 1, :] * win
        y = acc + cb_ref[...]
        mu = jnp.mean(y, axis=-1, keepdims=True)
        yc = y - mu
        yn = yc * lax.rsqrt(jnp.mean(yc * yc, axis=-1, keepdims=True) + 1e-5)
        y_ref[0, r0:r0 + rc, :] = _silu(yn * lg_ref[...] + lb_ref[...]).astype(y_ref.dtype)
    new_hist = fbuf[pl.ds(tt, HIST), :]
    hist_ref[0] = new_hist
    fbuf[0:HIST, :] = new_hist


def _conv(a3d, state, conv_w, conv_b, ln_g, ln_b):
    bsz, t_len, _ = a3d.shape
    tt = _pick(t_len, (256, 128, 64, 32, 16, 8))
    rc = min(tt, 64)
    buf_rows = HIST + max(tt, SUBLANES)
    st = jnp.pad(state, ((0, 0), (HIST - (CONV_W - 1), 0), (0, 0)))
    fixed = lambda b, t: (0, 0)
    y, hist = pl.pallas_call(
        functools.partial(_conv_body, tt=tt, rc=rc),
        grid=(bsz, t_len // tt),
        in_specs=[pl.BlockSpec((1, tt, 3 * C_A), lambda b, t: (b, t, 0)),
                  pl.BlockSpec((1, HIST, C_A), lambda b, t: (b, 0, 0)),
                  pl.BlockSpec((CONV_W, C_A), fixed),
                  pl.BlockSpec((1, C_A), fixed),
                  pl.BlockSpec((1, C_A), fixed),
                  pl.BlockSpec((1, C_A), fixed)],
        out_specs=[pl.BlockSpec((1, tt, C_A), lambda b, t: (b, t, 0)),
                   pl.BlockSpec((1, HIST, C_A), lambda b, t: (b, 0, 0))],
        out_shape=[jax.ShapeDtypeStruct((bsz, t_len, C_A), BF16),
                   jax.ShapeDtypeStruct((bsz, HIST, C_A), F32)],
        scratch_shapes=[pltpu.VMEM((buf_rows, C_A), F32),
                        pltpu.VMEM((SUBLANES, buf_rows - SUBLANES, C_A), F32)],
        compiler_params=_cparams(2),
        name="conv",
    )(a3d, st, conv_w, conv_b.reshape(1, C_A), ln_g.reshape(1, C_A), ln_b.reshape(1, C_A))
    return y, hist[:, HIST - (CONV_W - 1):, :]


def _lambda(lq, lam_init):
    s01 = jnp.sum(lq[0:1, :] * lq[1:2, :], axis=-1, keepdims=True)
    s23 = jnp.sum(lq[2:3, :] * lq[3:4, :], axis=-1, keepdims=True)
    return jnp.exp(s01) - jnp.exp(s23) + lam_init


def _attn_finish(o, g, z, lam_init):
    y = o * lax.rsqrt(jnp.mean(o * o, axis=-1, keepdims=True) + 1e-6) * g * (1.0 - lam_init)
    return y * _silu(z)


SUM_ROWS = 16


def _attn_body(it_ref, jt_ref, q_ref, k_ref, vt_ref, z_ref, lq_ref, g_ref, o_ref,
               m_scr, acc_scr, *, blk, kb_step, lam_init):
    step_id = pl.program_id(0)
    i = it_ref[step_id]
    jj = jt_ref[step_id]

    @pl.when(jj == 0)
    def _():
        m_scr[...] = jnp.full(m_scr.shape, NEG, F32)
        acc_scr[...] = jnp.zeros(acc_scr.shape, F32)

    def sweep(diagonal, sub):
        keys = pl.ds(pl.multiple_of(sub * blk, blk), blk)
        if diagonal:
            kpos = lax.broadcasted_iota(jnp.int32, (blk, 2 * blk), 0)
            qpos = lax.broadcasted_iota(jnp.int32, (blk, 2 * blk), 1) % blk
            keep = kpos <= qpos
        lane = lax.broadcasted_iota(jnp.int32, (blk, DV_B), 1)
        ones = jnp.ones((SUM_ROWS, blk), BF16)

        def scores(h):
            hs = slice(h * DV_B, (h + 1) * DV_B)
            qh = q_ref[:, hs]
            zero = jnp.zeros_like(qh)
            q12 = jnp.concatenate([jnp.where(lane < DK_B, qh, zero), jnp.where(lane < DK_B, zero, qh)],
                                  axis=0)
            st = lax.dot_general(k_ref[keys, hs], q12, (((1,), (1,)), ((), ())),
                                 preferred_element_type=F32)
            if diagonal:
                st = jnp.where(keep, st, NEG)
            m_prev = m_scr[h:h + 1, :]
            m_new = jnp.maximum(m_prev, jnp.max(st, axis=0, keepdims=True))
            return st, m_prev, m_new

        def values(h, pt, alpha):
            vth = jnp.concatenate([vt_ref[h * DV_B:(h + 1) * DV_B, keys], ones], axis=0)
            acc_scr[h] = alpha * acc_scr[h] + jnp.dot(vth, pt, preferred_element_type=F32)

        nxt = scores(0)
        pending = None
        for h in range(H_B):
            st, m_prev, m_new = nxt
            if h + 1 < H_B:
                nxt = scores(h + 1)
            alpha = jnp.exp2(m_prev - m_new)
            pt = jnp.exp2(st - m_new).astype(BF16)
            m_scr[h:h + 1, :] = m_new
            if pending is not None:
                values(*pending)
            pending = (h, pt, alpha)
        values(*pending)

    first = jj * kb_step
    n_below = jnp.clip(i - first, 0, kb_step)

    def below(sub, carry):
        sweep(False, sub)
        return carry

    lax.fori_loop(0, n_below, below, 0)

    @pl.when(i - first < kb_step)
    def _():
        sweep(True, i - first)
        lam = _lambda(lq_ref[...], lam_init)
        for h in range(H_B):
            hs = slice(h * DV_B, (h + 1) * DV_B)
            a1 = acc_scr[h, 0:DV_B, 0:blk] * (1.0 / acc_scr[h, DV_B:DV_B + 1, 0:blk])
            a2 = acc_scr[h, 0:DV_B, blk:2 * blk] * (1.0 / acc_scr[h, DV_B:DV_B + 1, blk:2 * blk])
            ot = a1 - lam * a2
            yt = ot * lax.rsqrt(jnp.mean(ot * ot, axis=0, keepdims=True) + 1e-6) * g_ref[...]
            o_ref[:, hs] = (yt.T * (1.0 - lam_init) * _silu(z_ref[:, hs])).astype(o_ref.dtype)


def _attn_prompt(q, kb, vt, z, lam_qk, attn_g, lam_init):
    t_len = q.shape[0]
    blk = _pick(t_len, (1024, 512, 256, 128))
    nb = t_len // blk
    kb_step = _pick(nb, (2, 1))
    pairs = [(i, jj) for i in range(nb) for jj in range(i // kb_step + 1)]
    i_tab = jnp.asarray([p[0] for p in pairs], jnp.int32)
    j_tab = jnp.asarray([p[1] for p in pairs], jnp.int32)
    q_map = lambda s, it, jt: (it[s], 0)
    k_map = lambda s, it, jt: (jt[s], 0)
    vt_map = lambda s, it, jt: (0, jt[s])
    fixed = lambda s, it, jt: (0, 0)
    return pl.pallas_call(
        functools.partial(_attn_body, blk=blk, kb_step=kb_step, lam_init=lam_init),
        grid_spec=pltpu.PrefetchScalarGridSpec(
            num_scalar_prefetch=2,
            grid=(len(pairs),),
            in_specs=[pl.BlockSpec((blk, D_B), q_map),
                      pl.BlockSpec((kb_step * blk, D_B), k_map),
                      pl.BlockSpec((D_B, kb_step * blk), vt_map),
                      pl.BlockSpec((blk, D_B), q_map),
                      pl.BlockSpec((4, DK_B), fixed),
                      pl.BlockSpec((DV_B, 1), fixed)],
            out_specs=pl.BlockSpec((blk, D_B), q_map),
            scratch_shapes=[pltpu.VMEM((H_B, 2 * blk), F32),
                            pltpu.VMEM((H_B, DV_B + SUM_ROWS, 2 * blk), F32)]),
        out_shape=jax.ShapeDtypeStruct((t_len, D_B), BF16),
        compiler_params=_cparams(1),
        name="attn_prompt",
    )(i_tab, j_tab, q, kb, vt, z, lam_qk, attn_g.reshape(DV_B, 1))


QROWS = 16


QR = H_B * QROWS


def _decode_body(pt_ref, q_ref, kn_ref, vn_ref, z_ref, lq_ref, g_ref, *rest,
                 n_pages_step, t_new, lam_init):
    k_refs = rest[:n_pages_step]
    v_refs = rest[n_pages_step:2 * n_pages_step]
    o_ref, m_scr, l_scr, acc_scr = rest[2 * n_pages_step:]
    g = pl.program_id(1)
    n_groups = pl.num_programs(1)

    @pl.when(g == 0)
    def _():
        m_scr[...] = jnp.full(m_scr.shape, NEG, F32)
        l_scr[...] = jnp.zeros(l_scr.shape, F32)
        acc_scr[...] = jnp.zeros(acc_scr.shape, F32)

    q64 = q_ref[0]
    row = lax.broadcasted_iota(jnp.int32, q64.shape, 0)
    lane = lax.broadcasted_iota(jnp.int32, q64.shape, 1)
    qd = jnp.where((lane < DK_B) == ((row % QROWS) < QROWS // 2), q64, jnp.zeros_like(q64))

    page = k_refs[0].shape[0] // H_B

    def head_rows(refs, h):
        return jnp.concatenate([r[pl.ds(h, page, stride=H_B), :] for r in refs], axis=0).astype(BF16)

    heads = range(H_B)
    rows = [slice(h * QROWS, (h + 1) * QROWS) for h in heads]
    s = [lax.dot_general(qd[rows[h], :], head_rows(k_refs, h), (((1,), (1,)), ((), ())),
                         preferred_element_type=F32) for h in heads]
    m_prev = [m_scr[rows[h], :] for h in heads]
    m_new = [jnp.maximum(m_prev[h], jnp.max(s[h], axis=-1, keepdims=True)) for h in heads]
    alpha = [jnp.exp(m_prev[h] - m_new[h]) for h in heads]
    p = [jnp.exp(s[h] - m_new[h]) for h in heads]
    pv = [jnp.dot(p[h].astype(BF16), head_rows(v_refs, h), preferred_element_type=F32) for h in heads]
    for h in heads:
        l_scr[rows[h], :] = alpha[h] * l_scr[rows[h], :] + jnp.sum(p[h], axis=-1, keepdims=True)
        acc_scr[rows[h], :] = alpha[h] * acc_scr[rows[h], :] + pv[h]
        m_scr[rows[h], :] = m_new[h]

    @pl.when(g == n_groups - 1)
    def _():
        lam = _lambda(lq_ref[...], lam_init)
        qf = qd.astype(F32)
        tok = lax.broadcasted_iota(jnp.int32, (QR, 1), 0) % (QROWS // 2)
        s_new = []
        for jn in range(t_new):
            sj = jnp.sum(qf * kn_ref[0, jn].astype(BF16).astype(F32), axis=-1, keepdims=True)
            s_new.append(jnp.where(tok >= jn, sj, NEG))
        m_prev = m_scr[...]
        m_fin = m_prev
        for sj in s_new:
            m_fin = jnp.maximum(m_fin, sj)
        alpha = jnp.exp(m_prev - m_fin)
        l_fin = alpha * l_scr[...]
        acc = alpha * acc_scr[...]
        for jn, sj in enumerate(s_new):
            pj = jnp.exp(sj - m_fin)
            l_fin = l_fin + pj
            acc = acc + pj.astype(BF16).astype(F32) * vn_ref[0, jn].astype(BF16).astype(F32)
        a = acc / l_fin
        half = QROWS // 2
        for h in range(H_B):
            hs = slice(h * DV_B, (h + 1) * DV_B)
            o = a[h * QROWS:h * QROWS + half, :] - lam * a[h * QROWS + half:(h + 1) * QROWS, :]
            o_ref[0, :, hs] = _attn_finish(o, g_ref[...], z_ref[0, :, hs], lam_init)


def _attn_decode(q, k_new, v_new, z, cache_k_l, cache_v_l, page_table, lam_qk, attn_g, lam_init,
                 layer):
    bsz, t_new, _ = q.shape
    n_pages = page_table.shape[1]
    page_rows = cache_k_l.shape[2]
    pps = _pick(n_pages, (16, 8, 4, 2, 1))
    half = QROWS // 2
    qh = jnp.swapaxes(q.reshape(bsz, t_new, H_B, DV_B), 1, 2)
    q64 = jnp.zeros((bsz, H_B, QROWS, DV_B), BF16)
    q64 = q64.at[:, :, 0:t_new].set(qh).at[:, :, half:half + t_new].set(qh).reshape(bsz, QR, DV_B)
    on_rows = lambda x: jnp.repeat(x.reshape(bsz, t_new, H_B, DV_B), QROWS, axis=2)
    z8 = jnp.pad(z, ((0, 0), (0, half - t_new), (0, 0)))
    fixed = lambda b, g, pt: (0, 0)
    per_b = lambda b, g, pt: (b, 0, 0)
    per_b4 = lambda b, g, pt: (b, 0, 0, 0)

    def page_spec(jp):
        return pl.BlockSpec((None, None, page_rows, DV_B),
                            lambda b, g, pt: (layer, pt[b, g * pps + jp], 0, 0))

    in_specs = [pl.BlockSpec((1, QR, DV_B), per_b),
                pl.BlockSpec((1, t_new, QR, DV_B), per_b4),
                pl.BlockSpec((1, t_new, QR, DV_B), per_b4),
                pl.BlockSpec((1, half, D_B), per_b),
                pl.BlockSpec((4, DK_B), fixed),
                pl.BlockSpec((1, DV_B), fixed)]
    in_specs += [page_spec(jp) for jp in range(pps)] * 2
    out = pl.pallas_call(
        functools.partial(_decode_body, n_pages_step=pps, t_new=t_new, lam_init=lam_init),
        grid_spec=pltpu.PrefetchScalarGridSpec(
            num_scalar_prefetch=1,
            grid=(bsz, n_pages // pps),
            in_specs=in_specs,
            out_specs=pl.BlockSpec((1, half, D_B), per_b),
            scratch_shapes=[pltpu.VMEM((QR, 1), F32), pltpu.VMEM((QR, 1), F32),
                            pltpu.VMEM((QR, DV_B), F32)]),
        out_shape=jax.ShapeDtypeStruct((bsz, half, D_B), F32),
        compiler_params=_cparams(2),
        name="attn_decode",
    )(page_table, q64, on_rows(k_new), on_rows(v_new), z8, lam_qk, attn_g.reshape(1, DV_B),
      *([cache_k_l] * pps), *([cache_v_l] * pps))
    return out[:, :t_new, :]


def _mlstm_body(c_ref, gc_ref, gr_ref, fbc_ref, fbr_ref, mg_ref, c0_ref, n0_ref, m0_ref,
                y_ref, cn_ref, nn_ref, mn_ref, c_scr, n_scr, m_scr, *, ln, t_valid, t_pad):
    t = pl.program_id(1)
    n_t = pl.num_programs(1)

    @pl.when(t == 0)
    def _():
        c_scr[...] = c0_ref[0]
        n_scr[...] = n0_ref[0]
        m_scr[...] = m0_ref[0]

    cblk = c_ref[0]
    gcol = gc_ref[0]
    grow = gr_ref[0]
    is_f_c = lax.broadcasted_iota(jnp.int32, (ln, N_GATE), 1) >= H_C
    is_f_r = lax.broadcasted_iota(jnp.int32, (N_GATE, ln), 0) >= H_C
    g_c = jnp.where(is_f_c, _log_sigmoid(gcol + fbc_ref[...]), gcol)
    g_r = jnp.where(is_f_r, _log_sigmoid(grow + fbr_ref[...]), grow)
    if t_pad != t_valid:
        ok_c = (t * ln + lax.broadcasted_iota(jnp.int32, (ln, N_GATE), 0)) < t_valid
        ok_r = (t * ln + lax.broadcasted_iota(jnp.int32, (N_GATE, ln), 1)) < t_valid
        g_c = jnp.where(ok_c, g_c, jnp.where(is_f_c, 0.0, NEG))
        g_r = jnp.where(ok_r, g_r, jnp.where(is_f_r, 0.0, NEG))
    rr = lax.broadcasted_iota(jnp.int32, (ln, ln), 0)
    cc = lax.broadcasted_iota(jnp.int32, (ln, ln), 1)
    causal = rr >= cc
    tri = causal.astype(F32)
    cum_c = jnp.dot(tri, jnp.where(is_f_c, g_c, 0.0), preferred_element_type=F32,
                    precision=lax.Precision.HIGHEST)
    cum_r = lax.dot_general(jnp.where(is_f_r, g_r, 0.0), tri, (((1,), (1,)), ((), ())),
                            preferred_element_type=F32, precision=lax.Precision.HIGHEST)
    m_all = m_scr[...]
    lane = lax.broadcasted_iota(jnp.int32, (1, 128), 1)
    m_out = jnp.zeros((1, 128), F32)

    for h in range(H_C):
        qf = cblk[:, h * DK_C:(h + 1) * DK_C]
        kf = cblk[:, D_C + h * DK_C:D_C + (h + 1) * DK_C] * KC_SCALE
        vf = cblk[:, 2 * D_C + h * DV_C:2 * D_C + (h + 1) * DV_C]
        of = cblk[:, 3 * D_C + h * DV_C:3 * D_C + (h + 1) * DV_C]
        zf = cblk[:, 4 * D_C + h * DV_C:4 * D_C + (h + 1) * DV_C]
        qb, kb, vb = qf.astype(BF16), kf.astype(BF16), vf.astype(BF16)
        bc = cum_c[:, H_C + h:H_C + h + 1]
        igc = g_c[:, h:h + 1]
        br = cum_r[H_C + h:H_C + h + 1, :]
        igr = g_r[h:h + 1, :]
        m_prev = m_all[:, h:h + 1]
        c_mat = c_scr[h]
        n_vec = n_scr[h]

        log_d = jnp.where(causal, bc - br + igr, NEG)
        inter = bc + m_prev
        m_t = jnp.maximum(inter, jnp.max(log_d, axis=-1, keepdims=True))
        d_w = jnp.where(causal, jnp.exp(log_d - m_t), 0.0)
        i_w = jnp.exp(inter - m_t)
        qk = lax.dot_general(qb, kb, (((1,), (1,)), ((), ())), preferred_element_type=F32)
        w = d_w * qk
        num = (jnp.dot(w.astype(BF16), vb, preferred_element_type=F32)
               + i_w * jnp.dot(qb, c_mat.astype(BF16), preferred_element_type=F32))
        den = jnp.sum(w, axis=-1, keepdims=True) + i_w * jnp.sum(qf * n_vec, axis=-1, keepdims=True)
        hh = num / jnp.maximum(jnp.abs(den), jnp.exp(-m_t))

        b_last = bc[ln - 1:ln, :]
        m_last = m_t[ln - 1:ln, :]
        i_last = i_w[ln - 1:ln, :]
        w_last = jnp.exp(b_last - bc + igc - m_last)
        kw = w_last * kf
        c_scr[h] = i_last * c_mat + lax.dot_general(kw.astype(BF16), vb, (((0,), (0,)), ((), ())),
                                                    preferred_element_type=F32)
        n_scr[h] = i_last * n_vec + jnp.sum(kw, axis=0, keepdims=True)
        m_out = jnp.where(lane == h, m_last, m_out)

        hg = _sigmoid(of) * hh
        yh = hg * lax.rsqrt(jnp.mean(hg * hg, axis=-1, keepdims=True) + 1e-6)
        yh = yh * mg_ref[:, h * DV_C:(h + 1) * DV_C]
        y_ref[0, :, h * DV_C:(h + 1) * DV_C] = (yh * _silu(zf)).astype(y_ref.dtype)

    m_scr[...] = m_out

    @pl.when(t == n_t - 1)
    def _():
        cn_ref[0] = c_scr[...]
        nn_ref[0] = n_scr[...]
        mn_ref[0] = m_scr[...]


def _mlstm(c3d, gcol, grow, f_bias, mlstm_g, c0, n0, m0, ln):
    bsz, t_valid, _ = c3d.shape
    t_pad = -(-t_valid // ln) * ln
    if t_pad != t_valid:
        extra = t_pad - t_valid
        c3d = jnp.pad(c3d, ((0, 0), (0, extra), (0, 0)))
        gcol = jnp.pad(gcol, ((0, 0), (0, extra), (0, 0)))
        grow = jnp.pad(grow, ((0, 0), (0, 0), (0, extra)))
    m0p = jnp.pad(m0, ((0, 0), (0, 128 - H_C))).reshape(bsz, 1, 128)
    n0r = n0.reshape(bsz, H_C, 1, DK_C)
    fb8 = jnp.concatenate([jnp.zeros((H_C,), F32), f_bias])
    fixed = lambda b, t: (0, 0)
    y, cn, nn, mn = pl.pallas_call(
        functools.partial(_mlstm_body, ln=ln, t_valid=t_valid, t_pad=t_pad),
        grid=(bsz, t_pad // ln),
        in_specs=[pl.BlockSpec((1, ln, 5 * D_C), lambda b, t: (b, t, 0)),
                  pl.BlockSpec((1, ln, N_GATE), lambda b, t: (b, t, 0)),
                  pl.BlockSpec((1, N_GATE, ln), lambda b, t: (b, 0, t)),
                  pl.BlockSpec((1, N_GATE), fixed),
                  pl.BlockSpec((N_GATE, 1), fixed),
                  pl.BlockSpec((1, D_C), fixed),
                  pl.BlockSpec((1, H_C, DK_C, DV_C), lambda b, t: (b, 0, 0, 0)),
                  pl.BlockSpec((1, H_C, 1, DK_C), lambda b, t: (b, 0, 0, 0)),
                  pl.BlockSpec((1, 1, 128), lambda b, t: (b, 0, 0))],
        out_specs=[pl.BlockSpec((1, ln, D_C), lambda b, t: (b, t, 0)),
                   pl.BlockSpec((1, H_C, DK_C, DV_C), lambda b, t: (b, 0, 0, 0)),
                   pl.BlockSpec((1, H_C, 1, DK_C), lambda b, t: (b, 0, 0, 0)),
                   pl.BlockSpec((1, 1, 128), lambda b, t: (b, 0, 0))],
        out_shape=[jax.ShapeDtypeStruct((bsz, t_pad, D_C), BF16),
                   jax.ShapeDtypeStruct((bsz, H_C, DK_C, DV_C), F32),
                   jax.ShapeDtypeStruct((bsz, H_C, 1, DK_C), F32),
                   jax.ShapeDtypeStruct((bsz, 1, 128), F32)],
        scratch_shapes=[pltpu.VMEM((H_C, DK_C, DV_C), F32), pltpu.VMEM((H_C, 1, DK_C), F32),
                        pltpu.VMEM((1, 128), F32)],
        compiler_params=_cparams(2),
        name="mlstm",
    )(c3d, gcol, grow, fb8.reshape(1, N_GATE), fb8.reshape(N_GATE, 1), mlstm_g.reshape(1, D_C),
      c0, n0r, m0p)
    return y[:, :t_valid], cn, nn.reshape(bsz, H_C, DK_C), mn[:, 0, :H_C]


def _outproj_body(x_ref, ya_ref, az_ref, yb_ref, yc_ref, pw_ref, w_ref, g_ref, o_ref):
    ya = jnp.dot(ya_ref[...].astype(BF16), pw_ref[...], preferred_element_type=F32)
    ya = ya * _silu(az_ref[...])
    out = jnp.dot(ya.astype(BF16), w_ref[0:C_A, :], preferred_element_type=F32)
    out = out + jnp.dot(yb_ref[...].astype(BF16), w_ref[C_A:C_A + D_B, :], preferred_element_type=F32)
    out = out + jnp.dot(yc_ref[...].astype(BF16), w_ref[C_A + D_B:, :], preferred_element_type=F32)
    y = out * lax.rsqrt(jnp.mean(out * out, axis=-1, keepdims=True) + 1e-6) * g_ref[...]
    o_ref[...] = x_ref[...] + y


def _outproj(x2d, ya, a2d, yb, yc, conv_pw, w_out, ln_post):
    rows, d = x2d.shape
    tm = _pick(rows, (1024, 512, 256, 128))
    row = lambda i: (i, 0)
    fixed = lambda i: (0, 0)
    return pl.pallas_call(
        _outproj_body,
        grid=(rows // tm,),
        in_specs=[pl.BlockSpec((tm, d), row),
                  pl.BlockSpec((tm, C_A), row),
                  pl.BlockSpec((tm, C_A), lambda i: (i, 2)),
                  pl.BlockSpec((tm, D_B), row),
                  pl.BlockSpec((tm, D_C), row),
                  pl.BlockSpec((C_A, C_A), fixed),
                  pl.BlockSpec((C_A + D_B + D_C, d), fixed),
                  pl.BlockSpec((1, d), fixed)],
        out_specs=pl.BlockSpec((tm, d), row),
        out_shape=jax.ShapeDtypeStruct((rows, d), F32),
        compiler_params=_cparams(1),
        name="outproj",
    )(x2d, ya, a2d, yb, yc, conv_pw.astype(BF16), w_out.astype(BF16), ln_post.reshape(1, d))


def _layer(x, lidx, conv_state, mstate, attn, flash_operands, weights, mlstm_chunk):
    (ln_pre, w_in, b_in, conv_w, conv_b, conv_ln_g, conv_ln_b, conv_pw, lam_qk, attn_g, f_bias,
     mlstm_g, w_out, ln_post) = weights
    bsz, t_len, d = x.shape
    rows = bsz * t_len
    lam_init = 0.8 - 0.6 * math.exp(-0.3 * lidx)
    x2d = x.reshape(rows, d)
    a, q, k, v, z, c, gcol, grow, *flash = _inproj(x2d, ln_pre, w_in, b_in, flash_operands)

    ya, conv_new = _conv(a.reshape(bsz, t_len, 3 * C_A), conv_state, conv_w, conv_b, conv_ln_g,
                         conv_ln_b)
    yb = attn(q, k, v, z, lam_qk, attn_g, lam_init, *flash)
    grow3 = jnp.swapaxes(grow.reshape(N_GATE, bsz, t_len), 0, 1)
    yc, c_new, n_new, m_new = _mlstm(c.reshape(bsz, t_len, 5 * D_C), gcol.reshape(bsz, t_len, N_GATE),
                                     grow3, f_bias, mlstm_g, *mstate, mlstm_chunk)
    x_new = _outproj(x2d, ya.reshape(rows, C_A), a, yb.reshape(rows, D_B), yc.reshape(rows, D_C),
                     conv_pw, w_out, ln_post)
    return (x_new.reshape(bsz, t_len, d), k.reshape(bsz, t_len, H_B, 2 * DK_B),
            v.reshape(bsz, t_len, H_B, DV_B), conv_new, (c_new, n_new, m_new))


def kernel(x_prompt, x_sample, cache_k, cache_v, state_conv, state_C, state_n, state_m, page_table,
           ln_pre, w_in, b_in, conv_w, conv_b, conv_ln_g, conv_ln_b, conv_pw, lam_qk, attn_g, f_bias,
           mlstm_g, w_out, ln_post):
    depth = w_in.shape[0]
    bp, tp, _ = x_prompt.shape
    bs, ts, _ = x_sample.shape
    assert bp == 1, "prompt attention sweep is written for a single prompt sequence"
    n_pool, page = cache_k.shape[1], cache_k.shape[2]
    ck = cache_k.reshape(depth, n_pool, page * H_B, DV_B)
    cv = cache_v.reshape(depth, n_pool, page * H_B, DV_B)

    def weights(l):
        return (ln_pre[l], w_in[l], b_in[l], conv_w[l], conv_b[l], conv_ln_g[l], conv_ln_b[l],
                conv_pw[l], lam_qk[l], attn_g[l], f_bias[l], mlstm_g[l], w_out[l], ln_post[l])

    def prompt_attn(q, k, v, z, lq, g, lam_init, kb, vt):
        return _attn_prompt(q, kb, vt, z, lq, g, lam_init)

    prompt_chunk = _pick(tp, (128, 64, 32, 16, 8))
    xp = x_prompt
    outs_p = [[] for _ in range(6)]
    for l in range(depth):
        buf0 = jnp.zeros((bp, CONV_W - 1, C_A), F32)
        m0 = (jnp.zeros((bp, H_C, DK_C, DV_C), F32), jnp.zeros((bp, H_C, DK_C), F32),
              jnp.zeros((bp, H_C), F32))
        xp, kr, vr, cb, (cm, nv, mv) = _layer(xp, l, buf0, m0, prompt_attn, True, weights(l),
                                              prompt_chunk)
        for lst, val in zip(outs_p, (kr, vr, cb, cm, nv, mv)):
            lst.append(val)

    xs = x_sample
    outs_s = [[] for _ in range(6)]
    for l in range(depth):
        def sample_attn(q, k, v, z, lq, g, lam_init, l=l):
            return _attn_decode(q.reshape(bs, ts, D_B), k.reshape(bs, ts, D_B), v.reshape(bs, ts, D_B),
                                z.reshape(bs, ts, D_B), ck, cv, page_table, lq, g, lam_init, l)

        st = (state_C[l], state_n[l], state_m[l])
        xs, kr, vr, cb, (cm, nv, mv) = _layer(xs, l, state_conv[l], st, sample_attn, False,
                                              weights(l), 8)
        for lst, val in zip(outs_s, (kr, vr, cb, cm, nv, mv)):
            lst.append(val)

    return (xp, xs, *[jnp.stack(o) for o in outs_p], *[jnp.stack(o) for o in outs_s])
```

```python
import functools
import math

import jax
import jax.numpy as jnp
from jax import lax
from jax.experimental import pallas as pl
from jax.experimental.pallas import tpu as pltpu

F32 = jnp.float32
BF16 = jnp.bfloat16

C_A = 256
CONV_W = 31
H_B = 4
DK_B = 64
DV_B = 128
D_B = H_B * DV_B
H_C = 4
DK_C = 64
DV_C = 64
D_C = H_C * DV_C

OFF_A = 0
OFF_Q = 3 * C_A
OFF_K = OFF_Q + 2 * H_B * DK_B
OFF_V = OFF_K + 2 * H_B * DK_B
OFF_Z = OFF_V + D_B
OFF_C = OFF_Z + D_B
OFF_G = OFF_C + 5 * D_C
N_MAIN = OFF_G
N_GATE = 2 * H_C

HIST = 32
NEG = -1e30
QK_SCALE = DK_B ** -0.5
KC_SCALE = DK_C ** -0.5
LOG2E = math.log2(math.e)

VMEM_LIMIT_BYTES = 48 * 1024 * 1024


def _cparams(n_axes):
    return pltpu.CompilerParams(dimension_semantics=("arbitrary",) * n_axes,
                                vmem_limit_bytes=VMEM_LIMIT_BYTES)


def _pick(n, candidates):
    for c in candidates:
        if n % c == 0:
            return c
    return n


def _sigmoid(x):
    return 1.0 / (1.0 + jnp.exp(-x))


def _silu(x):
    return x * _sigmoid(x)


def _log_sigmoid(x):
    return jnp.minimum(x, 0.0) - jnp.log(1.0 + jnp.exp(-jnp.abs(x)))


def _inproj_body(x_ref, g_ref, w_ref, b_ref, wg_ref, wgt_ref, bgc_ref, bgr_ref, *rest, flash_operands):
    a_ref, q_ref, k_ref, v_ref, z_ref, c_ref, gc_ref, gr_ref = rest[:8]
    xf = x_ref[...]
    h = xf * lax.rsqrt(jnp.mean(xf * xf, axis=-1, keepdims=True) + 1e-6) * g_ref[...]
    hb = h.astype(BF16)
    rows_last = (((1,), (1,)), ((), ()))

    def seg(lo, hi):
        return jnp.dot(hb, w_ref[:, lo:hi], preferred_element_type=F32) + b_ref[:, lo:hi]

    a_ref[...] = seg(OFF_A, OFF_Q)
    q_scale = QK_SCALE * LOG2E if flash_operands else QK_SCALE
    q_ref[...] = (seg(OFF_Q, OFF_K) * q_scale).astype(BF16)
    kk = seg(OFF_K, OFF_V)
    vv = seg(OFF_V, OFF_Z)
    tm = kk.shape[0]
    for h in range(H_B):
        k_ref[pl.ds(h, tm, stride=H_B), :] = kk[:, h * DV_B:(h + 1) * DV_B]
        v_ref[pl.ds(h, tm, stride=H_B), :] = vv[:, h * DV_B:(h + 1) * DV_B]
    z_ref[...] = seg(OFF_Z, OFF_C)
    c_ref[...] = seg(OFF_C, OFF_G)
    gc_ref[...] = jnp.dot(hb, wg_ref[...], preferred_element_type=F32) + bgc_ref[...]
    gr_ref[...] = lax.dot_general(wgt_ref[...], hb, rows_last, preferred_element_type=F32) + bgr_ref[...]
    if flash_operands:
        kb_ref, vt_ref = rest[8:]
        kb_ref[...] = kk.astype(BF16)
        vt_ref[...] = vv.T.astype(BF16)


def _inproj(x2d, ln_pre, w_in, b_in, flash_operands):
    rows, d = x2d.shape
    tm = _pick(rows, (512, 256, 128))
    w_main = w_in[:, :N_MAIN].astype(BF16)
    w_gate = w_in[:, N_MAIN:].astype(BF16)
    b_main = b_in[:N_MAIN].reshape(1, N_MAIN)
    b_gate = b_in[N_MAIN:]
    row = lambda i: (i, 0)
    col = lambda i: (0, i)
    fixed = lambda i: (0, 0)
    head_rows = jax.ShapeDtypeStruct((rows * H_B, DV_B), F32)
    head_spec = pl.BlockSpec((tm * H_B, DV_B), row)
    out_shape = [jax.ShapeDtypeStruct((rows, OFF_Q - OFF_A), F32),
                 jax.ShapeDtypeStruct((rows, OFF_K - OFF_Q), BF16),
                 head_rows, head_rows,
                 jax.ShapeDtypeStruct((rows, OFF_C - OFF_Z), F32),
                 jax.ShapeDtypeStruct((rows, OFF_G - OFF_C), F32)]
    out_specs = [pl.BlockSpec((tm, OFF_Q - OFF_A), row), pl.BlockSpec((tm, OFF_K - OFF_Q), row),
                 head_spec, head_spec,
                 pl.BlockSpec((tm, OFF_C - OFF_Z), row), pl.BlockSpec((tm, OFF_G - OFF_C), row)]
    out_shape += [jax.ShapeDtypeStruct((rows, N_GATE), F32), jax.ShapeDtypeStruct((N_GATE, rows), F32)]
    out_specs += [pl.BlockSpec((tm, N_GATE), row), pl.BlockSpec((N_GATE, tm), col)]
    in_specs = [pl.BlockSpec((tm, d), row),
                pl.BlockSpec((1, d), fixed),
                pl.BlockSpec((d, N_MAIN), fixed),
                pl.BlockSpec((1, N_MAIN), fixed),
                pl.BlockSpec((d, N_GATE), fixed),
                pl.BlockSpec((N_GATE, d), fixed),
                pl.BlockSpec((1, N_GATE), fixed),
                pl.BlockSpec((N_GATE, 1), fixed)]
    args = [x2d, ln_pre.reshape(1, d), w_main, b_main, w_gate, w_gate.T,
            b_gate.reshape(1, N_GATE), b_gate.reshape(N_GATE, 1)]
    if flash_operands:
        out_shape += [jax.ShapeDtypeStruct((rows, D_B), BF16), jax.ShapeDtypeStruct((D_B, rows), BF16)]
        out_specs += [pl.BlockSpec((tm, D_B), row), pl.BlockSpec((D_B, tm), col)]
    return pl.pallas_call(
        functools.partial(_inproj_body, flash_operands=flash_operands),
        grid=(rows // tm,),
        in_specs=in_specs,
        out_specs=out_specs,
        out_shape=out_shape,
        compiler_params=_cparams(1),
        name="inproj",
    )(*args)


SUBLANES = 8


def _conv_body(a_ref, st_ref, cw_ref, cb_ref, lg_ref, lb_ref, y_ref, hist_ref, fbuf, shifted,
               *, tt, rc):
    t = pl.program_id(1)

    @pl.when(t == 0)
    def _():
        fbuf[0:HIST, :] = st_ref[0]

    a = a_ref[0]
    fbuf[HIST:HIST + tt, :] = a[:, 0:C_A] * _sigmoid(a[:, C_A:2 * C_A])
    n_rows = shifted.shape[1]
    for k in range(1, SUBLANES):
        shifted[k] = fbuf[pl.ds(k, n_rows), :]
    first = HIST - (CONV_W - 1)
    for r0 in range(0, tt, rc):
        acc = jnp.zeros((rc, C_A), F32)
        for j in range(CONV_W):
            k = (first + j) % SUBLANES
            base = first + j - k + r0
            win = fbuf[base:base + rc, :] if k == 0 else shifted[k, base:base + rc, :]
            acc = acc + cw_ref[j:j + 1, :] * win
        y = acc + cb_ref[...]
        mu = jnp.mean(y, axis=-1, keepdims=True)
        yc = y - mu
        yn = yc * lax.rsqrt(jnp.mean(yc * yc, axis=-1, keepdims=True) + 1e-5)
        y_ref[0, r0:r0 + rc, :] = _silu(yn * lg_ref[...] + lb_ref[...]).astype(y_ref.dtype)
    new_hist = fbuf[pl.ds(tt, HIST), :]
    hist_ref[0] = new_hist
    fbuf[0:HIST, :] = new_hist


def _conv(a3d, state, conv_w, conv_b, ln_g, ln_b):
    bsz, t_len, _ = a3d.shape
    tt = _pick(t_len, (256, 128, 64, 32, 16, 8))
    rc = min(tt, 64)
    buf_rows = HIST + max(tt, SUBLANES)
    st = jnp.pad(state, ((0, 0), (HIST - (CONV_W - 1), 0), (0, 0)))
    fixed = lambda b, t: (0, 0)
    y, hist = pl.pallas_call(
        functools.partial(_conv_body, tt=tt, rc=rc),
        grid=(bsz, t_len // tt),
        in_specs=[pl.BlockSpec((1, tt, 3 * C_A), lambda b, t: (b, t, 0)),
                  pl.BlockSpec((1, HIST, C_A), lambda b, t: (b, 0, 0)),
                  pl.BlockSpec((CONV_W, C_A), fixed),
                  pl.BlockSpec((1, C_A), fixed),
                  pl.BlockSpec((1, C_A), fixed),
                  pl.BlockSpec((1, C_A), fixed)],
        out_specs=[pl.BlockSpec((1, tt, C_A), lambda b, t: (b, t, 0)),
                   pl.BlockSpec((1, HIST, C_A), lambda b, t: (b, 0, 0))],
        out_shape=[jax.ShapeDtypeStruct((bsz, t_len, C_A), BF16),
                   jax.ShapeDtypeStruct((bsz, HIST, C_A), F32)],
        scratch_shapes=[pltpu.VMEM((buf_rows, C_A), F32),
                        pltpu.VMEM((SUBLANES, buf_rows - SUBLANES, C_A), F32)],
        compiler_params=_cparams(2),
        name="conv",
    )(a3d, st, conv_w, conv_b.reshape(1, C_A), ln_g.reshape(1, C_A), ln_b.reshape(1, C_A))
    return y, hist[:, HIST - (CONV_W - 1):, :]


def _lambda(lq, lam_init):
    s01 = jnp.sum(lq[0:1, :] * lq[1:2, :], axis=-1, keepdims=True)
    s23 = jnp.sum(lq[2:3, :] * lq[3:4, :], axis=-1, keepdims=True)
    return jnp.exp(s01) - jnp.exp(s23) + lam_init


def _attn_finish(o, g, z, lam_init):
    y = o * lax.rsqrt(jnp.mean(o * o, axis=-1, keepdims=True) + 1e-6) * g * (1.0 - lam_init)
    return y * _silu(z)


SUM_ROWS = 16


def _attn_body(it_ref, jt_ref, q_ref, k_ref, vt_ref, z_ref, lq_ref, g_ref, o_ref,
               m_scr, acc_scr, *, blk, kb_step, lam_init):
    step_id = pl.program_id(0)
    i = it_ref[step_id]
    jj = jt_ref[step_id]

    @pl.when(jj == 0)
    def _():
        m_scr[...] = jnp.full(m_scr.shape, NEG, F32)
        acc_scr[...] = jnp.zeros(acc_scr.shape, F32)

    def sweep(diagonal, sub):
        keys = pl.ds(pl.multiple_of(sub * blk, blk), blk)
        if diagonal:
            kpos = lax.broadcasted_iota(jnp.int32, (blk, 2 * blk), 0)
            qpos = lax.broadcasted_iota(jnp.int32, (blk, 2 * blk), 1) % blk
            keep = kpos <= qpos
        lane = lax.broadcasted_iota(jnp.int32, (blk, DV_B), 1)
        ones = jnp.ones((SUM_ROWS, blk), BF16)

        def scores(h):
            hs = slice(h * DV_B, (h + 1) * DV_B)
            qh = q_ref[:, hs]
            zero = jnp.zeros_like(qh)
            q12 = jnp.concatenate([jnp.where(lane < DK_B, qh, zero), jnp.where(lane < DK_B, zero, qh)],
                                  axis=0)
            st = lax.dot_general(k_ref[keys, hs], q12, (((1,), (1,)), ((), ())),
                                 preferred_element_type=F32)
            if diagonal:
                st = jnp.where(keep, st, NEG)
            m_prev = m_scr[h:h + 1, :]
            m_new = jnp.maximum(m_prev, jnp.max(st, axis=0, keepdims=True))
            return st, m_prev, m_new

        def values(h, pt, alpha):
            vth = jnp.concatenate([vt_ref[h * DV_B:(h + 1) * DV_B, keys], ones], axis=0)
            acc_scr[h] = alpha * acc_scr[h] + jnp.dot(vth, pt, preferred_element_type=F32)

        nxt = scores(0)
        pending = None
        for h in range(H_B):
            st, m_prev, m_new = nxt
            if h + 1 < H_B:
                nxt = scores(h + 1)
            alpha = jnp.exp2(m_prev - m_new)
            pt = jnp.exp2(st - m_new).astype(BF16)
            m_scr[h:h + 1, :] = m_new
            if pending is not None:
                values(*pending)
            pending = (h, pt, alpha)
        values(*pending)

    first = jj * kb_step
    n_below = jnp.clip(i - first, 0, kb_step)

    def below(sub, carry):
        sweep(False, sub)
        return carry

    lax.fori_loop(0, n_below, below, 0)

    @pl.when(i - first < kb_step)
    def _():
        sweep(True, i - first)
        lam = _lambda(lq_ref[...], lam_init)
        for h in range(H_B):
            hs = slice(h * DV_B, (h + 1) * DV_B)
            a1 = acc_scr[h, 0:DV_B, 0:blk] * (1.0 / acc_scr[h, DV_B:DV_B + 1, 0:blk])
            a2 = acc_scr[h, 0:DV_B, blk:2 * blk] * (1.0 / acc_scr[h, DV_B:DV_B + 1, blk:2 * blk])
            ot = a1 - lam * a2
            yt = ot * lax.rsqrt(jnp.mean(ot * ot, axis=0, keepdims=True) + 1e-6) * g_ref[...]
            o_ref[:, hs] = (yt.T * (1.0 - lam_init) * _silu(z_ref[:, hs])).astype(o_ref.dtype)


def _attn_prompt(q, kb, vt, z, lam_qk, attn_g, lam_init):
    t_len = q.shape[0]
    blk = _pick(t_len, (1024, 512, 256, 128))
    nb = t_len // blk
    kb_step = _pick(nb, (2, 1))
    pairs = [(i, jj) for i in range(nb) for jj in range(i // kb_step + 1)]
    i_tab = jnp.asarray([p[0] for p in pairs], jnp.int32)
    j_tab = jnp.asarray([p[1] for p in pairs], jnp.int32)
    q_map = lambda s, it, jt: (it[s], 0)
    k_map = lambda s, it, jt: (jt[s], 0)
    vt_map = lambda s, it, jt: (0, jt[s])
    fixed = lambda s, it, jt: (0, 0)
    return pl.pallas_call(
        functools.partial(_attn_body, blk=blk, kb_step=kb_step, lam_init=lam_init),
        grid_spec=pltpu.PrefetchScalarGridSpec(
            num_scalar_prefetch=2,
            grid=(len(pairs),),
            in_specs=[pl.BlockSpec((blk, D_B), q_map),
                      pl.BlockSpec((kb_step * blk, D_B), k_map),
                      pl.BlockSpec((D_B, kb_step * blk), vt_map),
                      pl.BlockSpec((blk, D_B), q_map),
                      pl.BlockSpec((4, DK_B), fixed),
                      pl.BlockSpec((DV_B, 1), fixed)],
            out_specs=pl.BlockSpec((blk, D_B), q_map),
            scratch_shapes=[pltpu.VMEM((H_B, 2 * blk), F32),
                            pltpu.VMEM((H_B, DV_B + SUM_ROWS, 2 * blk), F32)]),
        out_shape=jax.ShapeDtypeStruct((t_len, D_B), BF16),
        compiler_params=_cparams(1),
        name="attn_prompt",
    )(i_tab, j_tab, q, kb, vt, z, lam_qk, attn_g.reshape(DV_B, 1))


QROWS = 16


QR = H_B * QROWS


def _decode_body(pt_ref, q_ref, kn_ref, vn_ref, z_ref, lq_ref, g_ref, *rest,
                 n_pages_step, t_new, lam_init):
    k_refs = rest[:n_pages_step]
    v_refs = rest[n_pages_step:2 * n_pages_step]
    o_ref, m_scr, l_scr, acc_scr = rest[2 * n_pages_step:]
    g = pl.program_id(1)
    n_groups = pl.num_programs(1)

    @pl.when(g == 0)
    def _():
        m_scr[...] = jnp.full(m_scr.shape, NEG, F32)
        l_scr[...] = jnp.zeros(l_scr.shape, F32)
        acc_scr[...] = jnp.zeros(acc_scr.shape, F32)

    q64 = q_ref[0]
    row = lax.broadcasted_iota(jnp.int32, q64.shape, 0)
    lane = lax.broadcasted_iota(jnp.int32, q64.shape, 1)
    qd = jnp.where((lane < DK_B) == ((row % QROWS) < QROWS // 2), q64, jnp.zeros_like(q64))

    page = k_refs[0].shape[0] // H_B

    def head_rows(refs, h):
        return jnp.concatenate([r[pl.ds(h, page, stride=H_B), :] for r in refs], axis=0).astype(BF16)

    heads = range(H_B)
    rows = [slice(h * QROWS, (h + 1) * QROWS) for h in heads]
    s = [lax.dot_general(qd[rows[h], :], head_rows(k_refs, h), (((1,), (1,)), ((), ())),
                         preferred_element_type=F32) for h in heads]
    m_prev = [m_scr[rows[h], :] for h in heads]
    m_new = [jnp.maximum(m_prev[h], jnp.max(s[h], axis=-1, keepdims=True)) for h in heads]
    alpha = [jnp.exp(m_prev[h] - m_new[h]) for h in heads]
    p = [jnp.exp(s[h] - m_new[h]) for h in heads]
    pv = [jnp.dot(p[h].astype(BF16), head_rows(v_refs, h), preferred_element_type=F32) for h in heads]
    for h in heads:
        l_scr[rows[h], :] = alpha[h] * l_scr[rows[h], :] + jnp.sum(p[h], axis=-1, keepdims=True)
        acc_scr[rows[h], :] = alpha[h] * acc_scr[rows[h], :] + pv[h]
        m_scr[rows[h], :] = m_new[h]

    @pl.when(g == n_groups - 1)
    def _():
        lam = _lambda(lq_ref[...], lam_init)
        qf = qd.astype(F32)
        tok = lax.broadcasted_iota(jnp.int32, (QR, 1), 0) % (QROWS // 2)
        s_new = []
        for jn in range(t_new):
            sj = jnp.sum(qf * kn_ref[0, jn].astype(BF16).astype(F32), axis=-1, keepdims=True)
            s_new.append(jnp.where(tok >= jn, sj, NEG))
        m_prev = m_scr[...]
        m_fin = m_prev
        for sj in s_new:
            m_fin = jnp.maximum(m_fin, sj)
        alpha = jnp.exp(m_prev - m_fin)
        l_fin = alpha * l_scr[...]
        acc = alpha * acc_scr[...]
        for jn, sj in enumerate(s_new):
            pj = jnp.exp(sj - m_fin)
            l_fin = l_fin + pj
            acc = acc + pj.astype(BF16).astype(F32) * vn_ref[0, jn].astype(BF16).astype(F32)
        a = acc / l_fin
        half = QROWS // 2
        for h in range(H_B):
            hs = slice(h * DV_B, (h + 1) * DV_B)
            o = a[h * QROWS:h * QROWS + half, :] - lam * a[h * QROWS + half:(h + 1) * QROWS, :]
            o_ref[0, :, hs] = _attn_finish(o, g_ref[...], z_ref[0, :, hs], lam_init)


def _attn_decode(q, k_new, v_new, z, cache_k_l, cache_v_l, page_table, lam_qk, attn_g, lam_init,
                 layer):
    bsz, t_new, _ = q.shape
    n_pages = page_table.shape[1]
    page_rows = cache_k_l.shape[2]
    pps = _pick(n_pages, (16, 8, 4, 2, 1))
    half = QROWS // 2
    qh = jnp.swapaxes(q.reshape(bsz, t_new, H_B, DV_B), 1, 2)
    q64 = jnp.zeros((bsz, H_B, QROWS, DV_B), BF16)
    q64 = q64.at[:, :, 0:t_new].set(qh).at[:, :, half:half + t_new].set(qh).reshape(bsz, QR, DV_B)
    on_rows = lambda x: jnp.repeat(x.reshape(bsz, t_new, H_B, DV_B), QROWS, axis=2)
    z8 = jnp.pad(z, ((0, 0), (0, half - t_new), (0, 0)))
    fixed = lambda b, g, pt: (0, 0)
    per_b = lambda b, g, pt: (b, 0, 0)
    per_b4 = lambda b, g, pt: (b, 0, 0, 0)

    def page_spec(jp):
        return pl.BlockSpec((None, None, page_rows, DV_B),
                            lambda b, g, pt: (layer, pt[b, g * pps + jp], 0, 0))

    in_specs = [pl.BlockSpec((1, QR, DV_B), per_b),
                pl.BlockSpec((1, t_new, QR, DV_B), per_b4),
                pl.BlockSpec((1, t_new, QR, DV_B), per_b4),
                pl.BlockSpec((1, half, D_B), per_b),
                pl.BlockSpec((4, DK_B), fixed),
                pl.BlockSpec((1, DV_B), fixed)]
    in_specs += [page_spec(jp) for jp in range(pps)] * 2
    out = pl.pallas_call(
        functools.partial(_decode_body, n_pages_step=pps, t_new=t_new, lam_init=lam_init),
        grid_spec=pltpu.PrefetchScalarGridSpec(
            num_scalar_prefetch=1,
            grid=(bsz, n_pages // pps),
            in_specs=in_specs,
            out_specs=pl.BlockSpec((1, half, D_B), per_b),
            scratch_shapes=[pltpu.VMEM((QR, 1), F32), pltpu.VMEM((QR, 1), F32),
                            pltpu.VMEM((QR, DV_B), F32)]),
        out_shape=jax.ShapeDtypeStruct((bsz, half, D_B), F32),
        compiler_params=_cparams(2),
        name="attn_decode",
    )(page_table, q64, on_rows(k_new), on_rows(v_new), z8, lam_qk, attn_g.reshape(1, DV_B),
      *([cache_k_l] * pps), *([cache_v_l] * pps))
    return out[:, :t_new, :]


MLSTM_CHUNK = 128
PAIR = 2 * DK_C
ONES_ROWS = 16
ST_ROWS = DV_C + ONES_ROWS


def _mlstm_body(c_ref, gc_ref, gr_ref, fbc_ref, fbr_ref, mg_ref, s0_ref, m0_ref,
                y_ref, sn_ref, mn_ref, s_scr, m_scr, *, ln, t_valid, t_pad):
    t = pl.program_id(1)
    n_t = pl.num_programs(1)

    @pl.when(t == 0)
    def _():
        s_scr[...] = s0_ref[0]
        m_scr[...] = m0_ref[0]

    gcol = gc_ref[0]
    grow = gr_ref[0]
    is_f_c = lax.broadcasted_iota(jnp.int32, (ln, N_GATE), 1) >= H_C
    is_f_r = lax.broadcasted_iota(jnp.int32, (N_GATE, ln), 0) >= H_C
    g_c = jnp.where(is_f_c, _log_sigmoid(gcol + fbc_ref[...]), gcol)
    g_r = jnp.where(is_f_r, _log_sigmoid(grow + fbr_ref[...]), grow)
    if t_pad != t_valid:
        ok_c = (t * ln + lax.broadcasted_iota(jnp.int32, (ln, N_GATE), 0)) < t_valid
        ok_r = (t * ln + lax.broadcasted_iota(jnp.int32, (N_GATE, ln), 1)) < t_valid
        g_c = jnp.where(ok_c, g_c, jnp.where(is_f_c, 0.0, NEG))
        g_r = jnp.where(ok_r, g_r, jnp.where(is_f_r, 0.0, NEG))
    rr = lax.broadcasted_iota(jnp.int32, (ln, ln), 0)
    cc = lax.broadcasted_iota(jnp.int32, (ln, ln), 1)
    tri = (rr >= cc).astype(F32)
    src_before = rr <= cc
    cum_c = jnp.dot(tri, jnp.where(is_f_c, g_c, 0.0), preferred_element_type=F32,
                    precision=lax.Precision.HIGHEST)
    cum_r = lax.dot_general(jnp.where(is_f_r, g_r, 0.0), tri, (((1,), (1,)), ((), ())),
                            preferred_element_type=F32, precision=lax.Precision.HIGHEST)
    m_all = m_scr[...]
    lane1 = lax.broadcasted_iota(jnp.int32, (1, 128), 1)
    lane_k = lax.broadcasted_iota(jnp.int32, (ln, PAIR), 1)
    m_out = jnp.zeros((1, 128), F32)
    ones = jnp.ones((ONES_ROWS, ln), F32)

    for p in range(H_C // 2):
        def pair_block(seg):
            return c_ref[0, :, seg * D_C + p * PAIR:seg * D_C + (p + 1) * PAIR]

        q_t = pair_block(0).T.astype(BF16)
        k_p = pair_block(1) * KC_SCALE
        v_t = pair_block(2).T
        o_t = pair_block(3).T
        z_t = pair_block(4).T
        y_halves = []
        for half in range(2):
            h = 2 * p + half
            rs = slice(half * DK_C, (half + 1) * DK_C)
            k_own = jnp.where((lane_k < DK_C) == (half == 0), k_p, 0.0).astype(BF16)
            a_c = g_c[:, h:h + 1] - cum_c[:, H_C + h:H_C + h + 1]
            br = cum_r[H_C + h:H_C + h + 1, :]
            igr = g_r[h:h + 1, :]
            m_prev = m_all[:, h:h + 1]
            s_mat = s_scr[h]

            log_d = jnp.where(src_before, br + a_c, NEG)
            inter = br + m_prev
            m_t = jnp.maximum(inter, jnp.max(log_d, axis=0, keepdims=True))
            d_w = jnp.where(src_before, jnp.exp(log_d - m_t), 0.0)
            i_w = jnp.exp(inter - m_t)
            qk = jnp.dot(k_own, q_t, preferred_element_type=F32)
            w = (d_w * qk).astype(BF16)
            v1 = jnp.concatenate([v_t[rs, :], ones], axis=0)
            nd = (jnp.dot(v1.astype(BF16), w, preferred_element_type=F32)
                  + i_w * jnp.dot(s_mat.astype(BF16), q_t, preferred_element_type=F32))
            den = nd[DV_C:DV_C + 1, :]
            hh = nd[0:DV_C, :] / jnp.maximum(jnp.abs(den), jnp.exp(-m_t))

            b_last = cum_c[ln - 1:ln, H_C + h:H_C + h + 1]
            m_last = jnp.maximum(b_last + m_prev, b_last + jnp.max(a_c, axis=0, keepdims=True))
            i_last = jnp.exp(b_last + m_prev - m_last)
            w_last = jnp.exp(b_last - br + igr - m_last)
            s_scr[h] = i_last * s_mat + jnp.dot((v1 * w_last).astype(BF16), k_own,
                                                preferred_element_type=F32)
            m_out = jnp.where(lane1 == h, m_last, m_out)

            hg = _sigmoid(o_t[rs, :]) * hh
            yh = hg * lax.rsqrt(jnp.mean(hg * hg, axis=0, keepdims=True) + 1e-6)
            yh = yh * mg_ref[h * DV_C:(h + 1) * DV_C, :]
            y_halves.append(yh * _silu(z_t[rs, :]))
        y_pair = jnp.concatenate(y_halves, axis=0)
        y_ref[0, :, p * PAIR:(p + 1) * PAIR] = y_pair.T.astype(y_ref.dtype)

    m_scr[...] = m_out

    @pl.when(t == n_t - 1)
    def _():
        sn_ref[0] = s_scr[...]
        mn_ref[0] = m_scr[...]


def _mlstm(c3d, gcol, grow, f_bias, mlstm_g, c0, n0, m0, ln):
    bsz, t_valid, _ = c3d.shape
    t_pad = -(-t_valid // ln) * ln
    if t_pad != t_valid:
        extra = t_pad - t_valid
        c3d = jnp.pad(c3d, ((0, 0), (0, extra), (0, 0)))
        gcol = jnp.pad(gcol, ((0, 0), (0, extra), (0, 0)))
        grow = jnp.pad(grow, ((0, 0), (0, 0), (0, extra)))
    m0p = jnp.pad(m0, ((0, 0), (0, 128 - H_C))).reshape(bsz, 1, 128)
    own = jnp.concatenate([jnp.swapaxes(c0, 2, 3),
                           jnp.broadcast_to(n0[:, :, None, :], (bsz, H_C, ONES_ROWS, DK_C))], axis=2)
    even = jnp.pad(own[:, 0::2], ((0, 0), (0, 0), (0, 0), (0, DK_C)))
    odd = jnp.pad(own[:, 1::2], ((0, 0), (0, 0), (0, 0), (DK_C, 0)))
    s0 = jnp.stack([even, odd], axis=2).reshape(bsz, H_C, ST_ROWS, PAIR)
    fb8 = jnp.concatenate([jnp.zeros((H_C,), F32), f_bias])
    fixed = lambda b, t: (0, 0)
    per_b = lambda b, t: (b, 0, 0, 0)
    y, sn, mn = pl.pallas_call(
        functools.partial(_mlstm_body, ln=ln, t_valid=t_valid, t_pad=t_pad),
        grid=(bsz, t_pad // ln),
        in_specs=[pl.BlockSpec((1, ln, 5 * D_C), lambda b, t: (b, t, 0)),
                  pl.BlockSpec((1, ln, N_GATE), lambda b, t: (b, t, 0)),
                  pl.BlockSpec((1, N_GATE, ln), lambda b, t: (b, 0, t)),
                  pl.BlockSpec((1, N_GATE), fixed),
                  pl.BlockSpec((N_GATE, 1), fixed),
                  pl.BlockSpec((D_C, 1), fixed),
                  pl.BlockSpec((1, H_C, ST_ROWS, PAIR), per_b),
                  pl.BlockSpec((1, 1, 128), lambda b, t: (b, 0, 0))],
        out_specs=[pl.BlockSpec((1, ln, D_C), lambda b, t: (b, t, 0)),
                   pl.BlockSpec((1, H_C, ST_ROWS, PAIR), per_b),
                   pl.BlockSpec((1, 1, 128), lambda b, t: (b, 0, 0))],
        out_shape=[jax.ShapeDtypeStruct((bsz, t_pad, D_C), BF16),
                   jax.ShapeDtypeStruct((bsz, H_C, ST_ROWS, PAIR), F32),
                   jax.ShapeDtypeStruct((bsz, 1, 128), F32)],
        scratch_shapes=[pltpu.VMEM((H_C, ST_ROWS, PAIR), F32), pltpu.VMEM((1, 128), F32)],
        compiler_params=_cparams(2),
        name="mlstm",
    )(c3d, gcol, grow, fb8.reshape(1, N_GATE), fb8.reshape(N_GATE, 1), mlstm_g.reshape(D_C, 1),
      s0, m0p)
    odd_head = (jnp.arange(H_C) % 2 == 1)[None, :, None, None]
    own_n = jnp.where(odd_head, sn[..., DK_C:], sn[..., :DK_C])
    c_new = jnp.swapaxes(own_n[:, :, :DV_C, :], 2, 3)
    n_new = own_n[:, :, DV_C, :]
    return y[:, :t_valid], c_new, n_new, mn[:, 0, :H_C]


def _outproj_body(x_ref, ya_ref, az_ref, yb_ref, yc_ref, pw_ref, w_ref, g_ref, o_ref):
    ya = jnp.dot(ya_ref[...].astype(BF16), pw_ref[...], preferred_element_type=F32)
    ya = ya * _silu(az_ref[...])
    out = jnp.dot(ya.astype(BF16), w_ref[0:C_A, :], preferred_element_type=F32)
    out = out + jnp.dot(yb_ref[...].astype(BF16), w_ref[C_A:C_A + D_B, :], preferred_element_type=F32)
    out = out + jnp.dot(yc_ref[...].astype(BF16), w_ref[C_A + D_B:, :], preferred_element_type=F32)
    y = out * lax.rsqrt(jnp.mean(out * out, axis=-1, keepdims=True) + 1e-6) * g_ref[...]
    o_ref[...] = x_ref[...] + y


def _outproj(x2d, ya, a2d, yb, yc, conv_pw, w_out, ln_post):
    rows, d = x2d.shape
    tm = _pick(rows, (1024, 512, 256, 128))
    row = lambda i: (i, 0)
    fixed = lambda i: (0, 0)
    return pl.pallas_call(
        _outproj_body,
        grid=(rows // tm,),
        in_specs=[pl.BlockSpec((tm, d), row),
                  pl.BlockSpec((tm, C_A), row),
                  pl.BlockSpec((tm, C_A), lambda i: (i, 2)),
                  pl.BlockSpec((tm, D_B), row),
                  pl.BlockSpec((tm, D_C), row),
                  pl.BlockSpec((C_A, C_A), fixed),
                  pl.BlockSpec((C_A + D_B + D_C, d), fixed),
                  pl.BlockSpec((1, d), fixed)],
        out_specs=pl.BlockSpec((tm, d), row),
        out_shape=jax.ShapeDtypeStruct((rows, d), F32),
        compiler_params=_cparams(1),
        name="outproj",
    )(x2d, ya, a2d, yb, yc, conv_pw.astype(BF16), w_out.astype(BF16), ln_post.reshape(1, d))


def _layer(x, lidx, conv_state, mstate, attn, flash_operands, weights, mlstm_chunk):
    (ln_pre, w_in, b_in, conv_w, conv_b, conv_ln_g, conv_ln_b, conv_pw, lam_qk, attn_g, f_bias,
     mlstm_g, w_out, ln_post) = weights
    bsz, t_len, d = x.shape
    rows = bsz * t_len
    lam_init = 0.8 - 0.6 * math.exp(-0.3 * lidx)
    x2d = x.reshape(rows, d)
    a, q, k, v, z, c, gcol, grow, *flash = _inproj(x2d, ln_pre, w_in, b_in, flash_operands)

    ya, conv_new = _conv(a.reshape(bsz, t_len, 3 * C_A), conv_state, conv_w, conv_b, conv_ln_g,
                         conv_ln_b)
    yb = attn(q, k, v, z, lam_qk, attn_g, lam_init, *flash)
    grow3 = jnp.swapaxes(grow.reshape(N_GATE, bsz, t_len), 0, 1)
    yc, c_new, n_new, m_new = _mlstm(c.reshape(bsz, t_len, 5 * D_C), gcol.reshape(bsz, t_len, N_GATE),
                                     grow3, f_bias, mlstm_g, *mstate, mlstm_chunk)
    x_new = _outproj(x2d, ya.reshape(rows, C_A), a, yb.reshape(rows, D_B), yc.reshape(rows, D_C),
                     conv_pw, w_out, ln_post)
    return (x_new.reshape(bsz, t_len, d), k.reshape(bsz, t_len, H_B, 2 * DK_B),
            v.reshape(bsz, t_len, H_B, DV_B), conv_new, (c_new, n_new, m_new))


def kernel(x_prompt, x_sample, cache_k, cache_v, state_conv, state_C, state_n, state_m, page_table,
           ln_pre, w_in, b_in, conv_w, conv_b, conv_ln_g, conv_ln_b, conv_pw, lam_qk, attn_g, f_bias,
           mlstm_g, w_out, ln_post):
    depth = w_in.shape[0]
    bp, tp, _ = x_prompt.shape
    bs, ts, _ = x_sample.shape
    assert bp == 1, "prompt attention sweep is written for a single prompt sequence"
    n_pool, page = cache_k.shape[1], cache_k.shape[2]
    ck = cache_k.reshape(depth, n_pool, page * H_B, DV_B)
    cv = cache_v.reshape(depth, n_pool, page * H_B, DV_B)

    def weights(l):
        return (ln_pre[l], w_in[l], b_in[l], conv_w[l], conv_b[l], conv_ln_g[l], conv_ln_b[l],
                conv_pw[l], lam_qk[l], attn_g[l], f_bias[l], mlstm_g[l], w_out[l], ln_post[l])

    def prompt_attn(q, k, v, z, lq, g, lam_init, kb, vt):
        return _attn_prompt(q, kb, vt, z, lq, g, lam_init)

    xp = x_prompt
    outs_p = [[] for _ in range(6)]
    for l in range(depth):
        buf0 = jnp.zeros((bp, CONV_W - 1, C_A), F32)
        m0 = (jnp.zeros((bp, H_C, DK_C, DV_C), F32), jnp.zeros((bp, H_C, DK_C), F32),
              jnp.zeros((bp, H_C), F32))
        xp, kr, vr, cb, (cm, nv, mv) = _layer(xp, l, buf0, m0, prompt_attn, True, weights(l),
                                              MLSTM_CHUNK)
        for lst, val in zip(outs_p, (kr, vr, cb, cm, nv, mv)):
            lst.append(val)

    xs = x_sample
    outs_s = [[] for _ in range(6)]
    for l in range(depth):
        def sample_attn(q, k, v, z, lq, g, lam_init, l=l):
            return _attn_decode(q.reshape(bs, ts, D_B), k.reshape(bs, ts, D_B), v.reshape(bs, ts, D_B),
                                z.reshape(bs, ts, D_B), ck, cv, page_table, lq, g, lam_init, l)

        st = (state_C[l], state_n[l], state_m[l])
        xs, kr, vr, cb, (cm, nv, mv) = _layer(xs, l, state_conv[l], st, sample_attn, False,
                                              weights(l), MLSTM_CHUNK)
        for lst, val in zip(outs_s, (kr, vr, cb, cm, nv, mv)):
            lst.append(val)

    return (xp, xs, *[jnp.stack(o) for o in outs_p], *[jnp.stack(o) for o in outs_s])
```

```python
import functools
import math

import jax
import jax.numpy as jnp
from jax import lax
from jax.experimental import pallas as pl
from jax.experimental.pallas import tpu as pltpu

F32 = jnp.float32
BF16 = jnp.bfloat16

C_A = 256
CONV_W = 31
H_B = 4
DK_B = 64
DV_B = 128
D_B = H_B * DV_B
H_C = 4
DK_C = 64
DV_C = 64
D_C = H_C * DV_C

OFF_A = 0
OFF_Q = 3 * C_A
OFF_K = OFF_Q + 2 * H_B * DK_B
OFF_V = OFF_K + 2 * H_B * DK_B
OFF_Z = OFF_V + D_B
OFF_C = OFF_Z + D_B
OFF_G = OFF_C + 5 * D_C
N_MAIN = OFF_G
N_GATE = 2 * H_C

HIST = 32
NEG = -1e30
QK_SCALE = DK_B ** -0.5
KC_SCALE = DK_C ** -0.5
LOG2E = math.log2(math.e)

VMEM_LIMIT_BYTES = 48 * 1024 * 1024


def _cparams(n_axes):
    return pltpu.CompilerParams(dimension_semantics=("arbitrary",) * n_axes,
                                vmem_limit_bytes=VMEM_LIMIT_BYTES)


def _pick(n, candidates):
    for c in candidates:
        if n % c == 0:
            return c
    return n


def _sigmoid(x):
    return 1.0 / (1.0 + jnp.exp(-x))


def _silu(x):
    return x * _sigmoid(x)


def _log_sigmoid(x):
    return jnp.minimum(x, 0.0) - jnp.log(1.0 + jnp.exp(-jnp.abs(x)))


def _inproj_body(x_ref, g_ref, w_ref, b_ref, wg_ref, wgt_ref, bgc_ref, bgr_ref, *rest, flash_operands):
    a_ref, q_ref, k_ref, v_ref, z_ref, c_ref, gc_ref, gr_ref = rest[:8]
    xf = x_ref[...]
    h = xf * lax.rsqrt(jnp.mean(xf * xf, axis=-1, keepdims=True) + 1e-6) * g_ref[...]
    hb = h.astype(BF16)
    rows_last = (((1,), (1,)), ((), ()))

    def seg(lo, hi):
        return jnp.dot(hb, w_ref[:, lo:hi], preferred_element_type=F32) + b_ref[:, lo:hi]

    a_ref[...] = seg(OFF_A, OFF_Q)
    q_scale = QK_SCALE * LOG2E if flash_operands else QK_SCALE
    q_ref[...] = (seg(OFF_Q, OFF_K) * q_scale).astype(BF16)
    kk = seg(OFF_K, OFF_V)
    vv = seg(OFF_V, OFF_Z)
    tm = kk.shape[0]
    for h in range(H_B):
        k_ref[pl.ds(h, tm, stride=H_B), :] = kk[:, h * DV_B:(h + 1) * DV_B]
        v_ref[pl.ds(h, tm, stride=H_B), :] = vv[:, h * DV_B:(h + 1) * DV_B]
    z_ref[...] = seg(OFF_Z, OFF_C)
    c_ref[...] = seg(OFF_C, OFF_G)
    gc_ref[...] = jnp.dot(hb, wg_ref[...], preferred_element_type=F32) + bgc_ref[...]
    gr_ref[...] = lax.dot_general(wgt_ref[...], hb, rows_last, preferred_element_type=F32) + bgr_ref[...]
    if flash_operands:
        kb_ref, vt_ref = rest[8:]
        kb_ref[...] = kk.astype(BF16)
        vt_ref[...] = vv.T.astype(BF16)


def _inproj(x2d, ln_pre, w_in, b_in, flash_operands):
    rows, d = x2d.shape
    tm = _pick(rows, (512, 256, 128))
    w_main = w_in[:, :N_MAIN].astype(BF16)
    w_gate = w_in[:, N_MAIN:].astype(BF16)
    b_main = b_in[:N_MAIN].reshape(1, N_MAIN)
    b_gate = b_in[N_MAIN:]
    row = lambda i: (i, 0)
    col = lambda i: (0, i)
    fixed = lambda i: (0, 0)
    head_rows = jax.ShapeDtypeStruct((rows * H_B, DV_B), F32)
    head_spec = pl.BlockSpec((tm * H_B, DV_B), row)
    out_shape = [jax.ShapeDtypeStruct((rows, OFF_Q - OFF_A), F32),
                 jax.ShapeDtypeStruct((rows, OFF_K - OFF_Q), BF16),
                 head_rows, head_rows,
                 jax.ShapeDtypeStruct((rows, OFF_C - OFF_Z), F32),
                 jax.ShapeDtypeStruct((rows, OFF_G - OFF_C), F32)]
    out_specs = [pl.BlockSpec((tm, OFF_Q - OFF_A), row), pl.BlockSpec((tm, OFF_K - OFF_Q), row),
                 head_spec, head_spec,
                 pl.BlockSpec((tm, OFF_C - OFF_Z), row), pl.BlockSpec((tm, OFF_G - OFF_C), row)]
    out_shape += [jax.ShapeDtypeStruct((rows, N_GATE), F32), jax.ShapeDtypeStruct((N_GATE, rows), F32)]
    out_specs += [pl.BlockSpec((tm, N_GATE), row), pl.BlockSpec((N_GATE, tm), col)]
    in_specs = [pl.BlockSpec((tm, d), row),
                pl.BlockSpec((1, d), fixed),
                pl.BlockSpec((d, N_MAIN), fixed),
                pl.BlockSpec((1, N_MAIN), fixed),
                pl.BlockSpec((d, N_GATE), fixed),
                pl.BlockSpec((N_GATE, d), fixed),
                pl.BlockSpec((1, N_GATE), fixed),
                pl.BlockSpec((N_GATE, 1), fixed)]
    args = [x2d, ln_pre.reshape(1, d), w_main, b_main, w_gate, w_gate.T,
            b_gate.reshape(1, N_GATE), b_gate.reshape(N_GATE, 1)]
    if flash_operands:
        out_shape += [jax.ShapeDtypeStruct((rows, D_B), BF16), jax.ShapeDtypeStruct((D_B, rows), BF16)]
        out_specs += [pl.BlockSpec((tm, D_B), row), pl.BlockSpec((D_B, tm), col)]
    return pl.pallas_call(
        functools.partial(_inproj_body, flash_operands=flash_operands),
        grid=(rows // tm,),
        in_specs=in_specs,
        out_specs=out_specs,
        out_shape=out_shape,
        compiler_params=_cparams(1),
        name="inproj",
    )(*args)


SUBLANES = 8


def _conv_body(a_ref, st_ref, cw_ref, cb_ref, lg_ref, lb_ref, y_ref, hist_ref, fbuf, shifted,
               *, tt, rc):
    t = pl.program_id(1)

    @pl.when(t == 0)
    def _():
        fbuf[0:HIST, :] = st_ref[0]

    a = a_ref[0]
    fbuf[HIST:HIST + tt, :] = a[:, 0:C_A] * _sigmoid(a[:, C_A:2 * C_A])
    n_rows = shifted.shape[1]
    for k in range(1, SUBLANES):
        shifted[k] = fbuf[pl.ds(k, n_rows), :]
    first = HIST - (CONV_W - 1)
    for r0 in range(0, tt, rc):
        acc = jnp.zeros((rc, C_A), F32)
        for j in range(CONV_W):
            k = (first + j) % SUBLANES
            base = first + j - k + r0
            win = fbuf[base:base + rc, :] if k == 0 else shifted[k, base:base + rc, :]
            acc = acc + cw_ref[j:j + 1, :] * win
        y = acc + cb_ref[...]
        mu = jnp.mean(y, axis=-1, keepdims=True)
        yc = y - mu
        yn = yc * lax.rsqrt(jnp.mean(yc * yc, axis=-1, keepdims=True) + 1e-5)
        y_ref[0, r0:r0 + rc, :] = _silu(yn * lg_ref[...] + lb_ref[...]).astype(y_ref.dtype)
    new_hist = fbuf[pl.ds(tt, HIST), :]
    hist_ref[0] = new_hist
    fbuf[0:HIST, :] = new_hist


def _conv(a3d, state, conv_w, conv_b, ln_g, ln_b):
    bsz, t_len, _ = a3d.shape
    tt = _pick(t_len, (256, 128, 64, 32, 16, 8))
    rc = min(tt, 64)
    buf_rows = HIST + max(tt, SUBLANES)
    st = jnp.pad(state, ((0, 0), (HIST - (CONV_W - 1), 0), (0, 0)))
    fixed = lambda b, t: (0, 0)
    y, hist = pl.pallas_call(
        functools.partial(_conv_body, tt=tt, rc=rc),
        grid=(bsz, t_len // tt),
        in_specs=[pl.BlockSpec((1, tt, 3 * C_A), lambda b, t: (b, t, 0)),
                  pl.BlockSpec((1, HIST, C_A), lambda b, t: (b, 0, 0)),
                  pl.BlockSpec((CONV_W, C_A), fixed),
                  pl.BlockSpec((1, C_A), fixed),
                  pl.BlockSpec((1, C_A), fixed),
                  pl.BlockSpec((1, C_A), fixed)],
        out_specs=[pl.BlockSpec((1, tt, C_A), lambda b, t: (b, t, 0)),
                   pl.BlockSpec((1, HIST, C_A), lambda b, t: (b, 0, 0))],
        out_shape=[jax.ShapeDtypeStruct((bsz, t_len, C_A), BF16),
                   jax.ShapeDtypeStruct((bsz, HIST, C_A), F32)],
        scratch_shapes=[pltpu.VMEM((buf_rows, C_A), F32),
                        pltpu.VMEM((SUBLANES, buf_rows - SUBLANES, C_A), F32)],
        compiler_params=_cparams(2),
        name="conv",
    )(a3d, st, conv_w, conv_b.reshape(1, C_A), ln_g.reshape(1, C_A), ln_b.reshape(1, C_A))
    return y, hist[:, HIST - (CONV_W - 1):, :]


def _lambda(lq, lam_init):
    s01 = jnp.sum(lq[0:1, :] * lq[1:2, :], axis=-1, keepdims=True)
    s23 = jnp.sum(lq[2:3, :] * lq[3:4, :], axis=-1, keepdims=True)
    return jnp.exp(s01) - jnp.exp(s23) + lam_init


def _attn_finish(o, g, z, lam_init):
    y = o * lax.rsqrt(jnp.mean(o * o, axis=-1, keepdims=True) + 1e-6) * g * (1.0 - lam_init)
    return y * _silu(z)


SUM_ROWS = 16


def _attn_body(it_ref, jt_ref, q_ref, k_ref, vt_ref, z_ref, lq_ref, g_ref, o_ref,
               m_scr, acc_scr, *, blk, kb_step, lam_init):
    step_id = pl.program_id(0)
    i = it_ref[step_id]
    jj = jt_ref[step_id]

    @pl.when(jj == 0)
    def _():
        m_scr[...] = jnp.full(m_scr.shape, NEG, F32)
        acc_scr[...] = jnp.zeros(acc_scr.shape, F32)

    def sweep(diagonal, sub):
        keys = pl.ds(pl.multiple_of(sub * blk, blk), blk)
        if diagonal:
            kpos = lax.broadcasted_iota(jnp.int32, (blk, 2 * blk), 0)
            qpos = lax.broadcasted_iota(jnp.int32, (blk, 2 * blk), 1) % blk
            keep = kpos <= qpos
        lane = lax.broadcasted_iota(jnp.int32, (blk, DV_B), 1)
        ones = jnp.ones((SUM_ROWS, blk), BF16)

        def scores(h):
            hs = slice(h * DV_B, (h + 1) * DV_B)
            qh = q_ref[:, hs]
            zero = jnp.zeros_like(qh)
            q12 = jnp.concatenate([jnp.where(lane < DK_B, qh, zero), jnp.where(lane < DK_B, zero, qh)],
                                  axis=0)
            st = lax.dot_general(k_ref[keys, hs], q12, (((1,), (1,)), ((), ())),
                                 preferred_element_type=F32)
            if diagonal:
                st = jnp.where(keep, st, NEG)
            m_prev = m_scr[h:h + 1, :]
            m_new = jnp.maximum(m_prev, jnp.max(st, axis=0, keepdims=True))
            return st, m_prev, m_new

        def values(h, pt, alpha):
            vth = jnp.concatenate([vt_ref[h * DV_B:(h + 1) * DV_B, keys], ones], axis=0)
            acc_scr[h] = alpha * acc_scr[h] + jnp.dot(vth, pt, preferred_element_type=F32)

        nxt = scores(0)
        pending = None
        for h in range(H_B):
            st, m_prev, m_new = nxt
            if h + 1 < H_B:
                nxt = scores(h + 1)
            alpha = jnp.exp2(m_prev - m_new)
            pt = jnp.exp2(st - m_new).astype(BF16)
            m_scr[h:h + 1, :] = m_new
            if pending is not None:
                values(*pending)
            pending = (h, pt, alpha)
        values(*pending)

    first = jj * kb_step
    n_below = jnp.clip(i - first, 0, kb_step)

    def below(sub, carry):
        sweep(False, sub)
        return carry

    lax.fori_loop(0, n_below, below, 0)

    @pl.when(i - first < kb_step)
    def _():
        sweep(True, i - first)
        lam = _lambda(lq_ref[...], lam_init)
        for h in range(H_B):
            hs = slice(h * DV_B, (h + 1) * DV_B)
            a1 = acc_scr[h, 0:DV_B, 0:blk] * (1.0 / acc_scr[h, DV_B:DV_B + 1, 0:blk])
            a2 = acc_scr[h, 0:DV_B, blk:2 * blk] * (1.0 / acc_scr[h, DV_B:DV_B + 1, blk:2 * blk])
            ot = a1 - lam * a2
            yt = ot * lax.rsqrt(jnp.mean(ot * ot, axis=0, keepdims=True) + 1e-6) * g_ref[...]
            o_ref[:, hs] = (yt.T * (1.0 - lam_init) * _silu(z_ref[:, hs])).astype(o_ref.dtype)


def _attn_prompt(q, kb, vt, z, lam_qk, attn_g, lam_init):
    t_len = q.shape[0]
    blk = _pick(t_len, (1024, 512, 256, 128))
    nb = t_len // blk
    kb_step = _pick(nb, (2, 1))
    pairs = [(i, jj) for i in range(nb) for jj in range(i // kb_step + 1)]
    i_tab = jnp.asarray([p[0] for p in pairs], jnp.int32)
    j_tab = jnp.asarray([p[1] for p in pairs], jnp.int32)
    q_map = lambda s, it, jt: (it[s], 0)
    k_map = lambda s, it, jt: (jt[s], 0)
    vt_map = lambda s, it, jt: (0, jt[s])
    fixed = lambda s, it, jt: (0, 0)
    return pl.pallas_call(
        functools.partial(_attn_body, blk=blk, kb_step=kb_step, lam_init=lam_init),
        grid_spec=pltpu.PrefetchScalarGridSpec(
            num_scalar_prefetch=2,
            grid=(len(pairs),),
            in_specs=[pl.BlockSpec((blk, D_B), q_map),
                      pl.BlockSpec((kb_step * blk, D_B), k_map),
                      pl.BlockSpec((D_B, kb_step * blk), vt_map),
                      pl.BlockSpec((blk, D_B), q_map),
                      pl.BlockSpec((4, DK_B), fixed),
                      pl.BlockSpec((DV_B, 1), fixed)],
            out_specs=pl.BlockSpec((blk, D_B), q_map),
            scratch_shapes=[pltpu.VMEM((H_B, 2 * blk), F32),
                            pltpu.VMEM((H_B, DV_B + SUM_ROWS, 2 * blk), F32)]),
        out_shape=jax.ShapeDtypeStruct((t_len, D_B), BF16),
        compiler_params=_cparams(1),
        name="attn_prompt",
    )(i_tab, j_tab, q, kb, vt, z, lam_qk, attn_g.reshape(DV_B, 1))


QROWS = 16


QR = H_B * QROWS


def _decode_body(pt_ref, q_ref, kn_ref, vn_ref, z_ref, lq_ref, g_ref, *rest,
                 n_pages_step, t_new, lam_init):
    k_refs = rest[:n_pages_step]
    v_refs = rest[n_pages_step:2 * n_pages_step]
    o_ref, m_scr, l_scr, acc_scr = rest[2 * n_pages_step:]
    g = pl.program_id(1)
    n_groups = pl.num_programs(1)

    @pl.when(g == 0)
    def _():
        m_scr[...] = jnp.full(m_scr.shape, NEG, F32)
        l_scr[...] = jnp.zeros(l_scr.shape, F32)
        acc_scr[...] = jnp.zeros(acc_scr.shape, F32)

    q64 = q_ref[0]
    row = lax.broadcasted_iota(jnp.int32, q64.shape, 0)
    lane = lax.broadcasted_iota(jnp.int32, q64.shape, 1)
    qd = jnp.where((lane < DK_B) == ((row % QROWS) < QROWS // 2), q64, jnp.zeros_like(q64))

    page = k_refs[0].shape[0] // H_B

    def head_rows(refs, h):
        return jnp.concatenate([r[pl.ds(h, page, stride=H_B), :] for r in refs], axis=0).astype(BF16)

    heads = range(H_B)
    rows = [slice(h * QROWS, (h + 1) * QROWS) for h in heads]
    s = [lax.dot_general(qd[rows[h], :], head_rows(k_refs, h), (((1,), (1,)), ((), ())),
                         preferred_element_type=F32) for h in heads]
    m_prev = [m_scr[rows[h], :] for h in heads]
    m_new = [jnp.maximum(m_prev[h], jnp.max(s[h], axis=-1, keepdims=True)) for h in heads]
    alpha = [jnp.exp(m_prev[h] - m_new[h]) for h in heads]
    p = [jnp.exp(s[h] - m_new[h]) for h in heads]
    pv = [jnp.dot(p[h].astype(BF16), head_rows(v_refs, h), preferred_element_type=F32) for h in heads]
    for h in heads:
        l_scr[rows[h], :] = alpha[h] * l_scr[rows[h], :] + jnp.sum(p[h], axis=-1, keepdims=True)
        acc_scr[rows[h], :] = alpha[h] * acc_scr[rows[h], :] + pv[h]
        m_scr[rows[h], :] = m_new[h]

    @pl.when(g == n_groups - 1)
    def _():
        lam = _lambda(lq_ref[...], lam_init)
        qf = qd.astype(F32)
        tok = lax.broadcasted_iota(jnp.int32, (QR, 1), 0) % (QROWS // 2)
        s_new = []
        for jn in range(t_new):
            sj = jnp.sum(qf * kn_ref[0, jn].astype(BF16).astype(F32), axis=-1, keepdims=True)
            s_new.append(jnp.where(tok >= jn, sj, NEG))
        m_prev = m_scr[...]
        m_fin = m_prev
        for sj in s_new:
            m_fin = jnp.maximum(m_fin, sj)
        alpha = jnp.exp(m_prev - m_fin)
        l_fin = alpha * l_scr[...]
        acc = alpha * acc_scr[...]
        for jn, sj in enumerate(s_new):
            pj = jnp.exp(sj - m_fin)
            l_fin = l_fin + pj
            acc = acc + pj.astype(BF16).astype(F32) * vn_ref[0, jn].astype(BF16).astype(F32)
        a = acc / l_fin
        half = QROWS // 2
        for h in range(H_B):
            hs = slice(h * DV_B, (h + 1) * DV_B)
            o = a[h * QROWS:h * QROWS + half, :] - lam * a[h * QROWS + half:(h + 1) * QROWS, :]
            o_ref[0, :, hs] = _attn_finish(o, g_ref[...], z_ref[0, :, hs], lam_init)


def _attn_decode(q, k_new, v_new, z, cache_k_l, cache_v_l, page_table, lam_qk, attn_g, lam_init,
                 layer):
    bsz, t_new, _ = q.shape
    n_pages = page_table.shape[1]
    page_rows = cache_k_l.shape[2]
    pps = _pick(n_pages, (16, 8, 4, 2, 1))
    half = QROWS // 2
    qh = jnp.swapaxes(q.reshape(bsz, t_new, H_B, DV_B), 1, 2)
    q64 = jnp.zeros((bsz, H_B, QROWS, DV_B), BF16)
    q64 = q64.at[:, :, 0:t_new].set(qh).at[:, :, half:half + t_new].set(qh).reshape(bsz, QR, DV_B)
    on_rows = lambda x: jnp.repeat(x.reshape(bsz, t_new, H_B, DV_B), QROWS, axis=2)
    z8 = jnp.pad(z, ((0, 0), (0, half - t_new), (0, 0)))
    fixed = lambda b, g, pt: (0, 0)
    per_b = lambda b, g, pt: (b, 0, 0)
    per_b4 = lambda b, g, pt: (b, 0, 0, 0)

    def page_spec(jp):
        return pl.BlockSpec((None, None, page_rows, DV_B),
                            lambda b, g, pt: (layer, pt[b, g * pps + jp], 0, 0))

    in_specs = [pl.BlockSpec((1, QR, DV_B), per_b),
                pl.BlockSpec((1, t_new, QR, DV_B), per_b4),
                pl.BlockSpec((1, t_new, QR, DV_B), per_b4),
                pl.BlockSpec((1, half, D_B), per_b),
                pl.BlockSpec((4, DK_B), fixed),
                pl.BlockSpec((1, DV_B), fixed)]
    in_specs += [page_spec(jp) for jp in range(pps)] * 2
    out = pl.pallas_call(
        functools.partial(_decode_body, n_pages_step=pps, t_new=t_new, lam_init=lam_init),
        grid_spec=pltpu.PrefetchScalarGridSpec(
            num_scalar_prefetch=1,
            grid=(bsz, n_pages // pps),
            in_specs=in_specs,
            out_specs=pl.BlockSpec((1, half, D_B), per_b),
            scratch_shapes=[pltpu.VMEM((QR, 1), F32), pltpu.VMEM((QR, 1), F32),
                            pltpu.VMEM((QR, DV_B), F32)]),
        out_shape=jax.ShapeDtypeStruct((bsz, half, D_B), F32),
        compiler_params=_cparams(2),
        name="attn_decode",
    )(page_table, q64, on_rows(k_new), on_rows(v_new), z8, lam_qk, attn_g.reshape(1, DV_B),
      *([cache_k_l] * pps), *([cache_v_l] * pps))
    return out[:, :t_new, :]


MLSTM_CHUNK = 128
PAIR = 2 * DK_C
ONES_ROWS = 16
ST_ROWS = DV_C + ONES_ROWS


def _mlstm_body(c_ref, gc_ref, gr_ref, fbc_ref, fbr_ref, mg_ref, s0_ref, m0_ref,
                y_ref, sn_ref, mn_ref, s_scr, m_scr, *pad_scr, ln, t_valid, t_pad):
    t = pl.program_id(1)
    n_t = pl.num_programs(1)

    @pl.when(t == 0)
    def _():
        s_scr[...] = s0_ref[0]
        m_scr[...] = m0_ref[0]

    rows_in = c_ref.shape[1]
    if rows_in == ln:
        c_blk, gcol, grow = c_ref.at[0], gc_ref[0], gr_ref[0]
    else:
        c_blk, gc_pad, gr_pad = pad_scr

        @pl.when(pl.program_id(0) == 0)
        def _():
            c_blk[...] = jnp.zeros(c_blk.shape, F32)
            gc_pad[...] = jnp.zeros(gc_pad.shape, F32)
            gr_pad[...] = jnp.zeros(gr_pad.shape, F32)

        c_blk[0:rows_in, :] = c_ref[0]
        gc_pad[0:rows_in, :] = gc_ref[0]
        gr_pad[:, 0:rows_in] = gr_ref[0]
        gcol, grow = gc_pad[...], gr_pad[...]
    is_f_c = lax.broadcasted_iota(jnp.int32, (ln, N_GATE), 1) >= H_C
    is_f_r = lax.broadcasted_iota(jnp.int32, (N_GATE, ln), 0) >= H_C
    g_c = jnp.where(is_f_c, _log_sigmoid(gcol + fbc_ref[...]), gcol)
    g_r = jnp.where(is_f_r, _log_sigmoid(grow + fbr_ref[...]), grow)
    if t_pad != t_valid:
        ok_c = (t * ln + lax.broadcasted_iota(jnp.int32, (ln, N_GATE), 0)) < t_valid
        ok_r = (t * ln + lax.broadcasted_iota(jnp.int32, (N_GATE, ln), 1)) < t_valid
        g_c = jnp.where(ok_c, g_c, jnp.where(is_f_c, 0.0, NEG))
        g_r = jnp.where(ok_r, g_r, jnp.where(is_f_r, 0.0, NEG))
    rr = lax.broadcasted_iota(jnp.int32, (ln, ln), 0)
    cc = lax.broadcasted_iota(jnp.int32, (ln, ln), 1)
    tri = (rr >= cc).astype(F32)
    src_before = rr <= cc
    cum_c = jnp.dot(tri, jnp.where(is_f_c, g_c, 0.0), preferred_element_type=F32,
                    precision=lax.Precision.HIGHEST)
    cum_r = lax.dot_general(jnp.where(is_f_r, g_r, 0.0), tri, (((1,), (1,)), ((), ())),
                            preferred_element_type=F32, precision=lax.Precision.HIGHEST)
    m_all = m_scr[...]
    lane1 = lax.broadcasted_iota(jnp.int32, (1, 128), 1)
    lane_k = lax.broadcasted_iota(jnp.int32, (ln, PAIR), 1)
    m_out = jnp.zeros((1, 128), F32)
    ones = jnp.ones((ONES_ROWS, ln), F32)

    for p in range(H_C // 2):
        def pair_block(seg):
            return c_blk[:, seg * D_C + p * PAIR:seg * D_C + (p + 1) * PAIR]

        q_t = pair_block(0).T.astype(BF16)
        k_p = pair_block(1) * KC_SCALE
        v_t = pair_block(2).T
        o_t = pair_block(3).T
        z_t = pair_block(4).T
        y_halves = []
        for half in range(2):
            h = 2 * p + half
            rs = slice(half * DK_C, (half + 1) * DK_C)
            k_own = jnp.where((lane_k < DK_C) == (half == 0), k_p, 0.0).astype(BF16)
            a_c = g_c[:, h:h + 1] - cum_c[:, H_C + h:H_C + h + 1]
            br = cum_r[H_C + h:H_C + h + 1, :]
            igr = g_r[h:h + 1, :]
            m_prev = m_all[:, h:h + 1]
            s_mat = s_scr[h]

            log_d = jnp.where(src_before, br + a_c, NEG)
            inter = br + m_prev
            m_t = jnp.maximum(inter, jnp.max(log_d, axis=0, keepdims=True))
            d_w = jnp.where(src_before, jnp.exp(log_d - m_t), 0.0)
            i_w = jnp.exp(inter - m_t)
            qk = jnp.dot(k_own, q_t, preferred_element_type=F32)
            w = (d_w * qk).astype(BF16)
            v1 = jnp.concatenate([v_t[rs, :], ones], axis=0)
            nd = (jnp.dot(v1.astype(BF16), w, preferred_element_type=F32)
                  + i_w * jnp.dot(s_mat.astype(BF16), q_t, preferred_element_type=F32))
            den = nd[DV_C:DV_C + 1, :]
            hh = nd[0:DV_C, :] / jnp.maximum(jnp.abs(den), jnp.exp(-m_t))

            b_last = cum_c[ln - 1:ln, H_C + h:H_C + h + 1]
            m_last = jnp.maximum(b_last + m_prev, b_last + jnp.max(a_c, axis=0, keepdims=True))
            i_last = jnp.exp(b_last + m_prev - m_last)
            w_last = jnp.exp(b_last - br + igr - m_last)
            s_scr[h] = i_last * s_mat + jnp.dot((v1 * w_last).astype(BF16), k_own,
                                                preferred_element_type=F32)
            m_out = jnp.where(lane1 == h, m_last, m_out)

            hg = _sigmoid(o_t[rs, :]) * hh
            yh = hg * lax.rsqrt(jnp.mean(hg * hg, axis=0, keepdims=True) + 1e-6)
            yh = yh * mg_ref[h * DV_C:(h + 1) * DV_C, :]
            y_halves.append(yh * _silu(z_t[rs, :]))
        y_pair = jnp.concatenate(y_halves, axis=0)
        y_ref[0, :, p * PAIR:(p + 1) * PAIR] = y_pair.T[0:rows_in, :].astype(y_ref.dtype)

    m_scr[...] = m_out

    @pl.when(t == n_t - 1)
    def _():
        sn_ref[0] = s_scr[...]
        mn_ref[0] = m_scr[...]


def _mlstm(c3d, gcol, grow, f_bias, mlstm_g, c0, n0, m0, ln):
    bsz, t_valid, _ = c3d.shape
    t_pad = -(-t_valid // ln) * ln
    pad_in_kernel = t_valid < ln
    rows_blk = t_valid if pad_in_kernel else ln
    pad_scratch = []
    if pad_in_kernel:
        pad_scratch = [pltpu.VMEM((ln, 5 * D_C), F32), pltpu.VMEM((ln, N_GATE), F32),
                       pltpu.VMEM((N_GATE, ln), F32)]
    elif t_pad != t_valid:
        extra = t_pad - t_valid
        c3d = jnp.pad(c3d, ((0, 0), (0, extra), (0, 0)))
        gcol = jnp.pad(gcol, ((0, 0), (0, extra), (0, 0)))
        grow = jnp.pad(grow, ((0, 0), (0, 0), (0, extra)))
    m0p = jnp.pad(m0, ((0, 0), (0, 128 - H_C))).reshape(bsz, 1, 128)
    own = jnp.concatenate([jnp.swapaxes(c0, 2, 3),
                           jnp.broadcast_to(n0[:, :, None, :], (bsz, H_C, ONES_ROWS, DK_C))], axis=2)
    even = jnp.pad(own[:, 0::2], ((0, 0), (0, 0), (0, 0), (0, DK_C)))
    odd = jnp.pad(own[:, 1::2], ((0, 0), (0, 0), (0, 0), (DK_C, 0)))
    s0 = jnp.stack([even, odd], axis=2).reshape(bsz, H_C, ST_ROWS, PAIR)
    fb8 = jnp.concatenate([jnp.zeros((H_C,), F32), f_bias])
    fixed = lambda b, t: (0, 0)
    per_b = lambda b, t: (b, 0, 0, 0)
    y, sn, mn = pl.pallas_call(
        functools.partial(_mlstm_body, ln=ln, t_valid=t_valid, t_pad=t_pad),
        grid=(bsz, t_pad // ln),
        in_specs=[pl.BlockSpec((1, rows_blk, 5 * D_C), lambda b, t: (b, t, 0)),
                  pl.BlockSpec((1, rows_blk, N_GATE), lambda b, t: (b, t, 0)),
                  pl.BlockSpec((1, N_GATE, rows_blk), lambda b, t: (b, 0, t)),
                  pl.BlockSpec((1, N_GATE), fixed),
                  pl.BlockSpec((N_GATE, 1), fixed),
                  pl.BlockSpec((D_C, 1), fixed),
                  pl.BlockSpec((1, H_C, ST_ROWS, PAIR), per_b),
                  pl.BlockSpec((1, 1, 128), lambda b, t: (b, 0, 0))],
        out_specs=[pl.BlockSpec((1, rows_blk, D_C), lambda b, t: (b, t, 0)),
                   pl.BlockSpec((1, H_C, ST_ROWS, PAIR), per_b),
                   pl.BlockSpec((1, 1, 128), lambda b, t: (b, 0, 0))],
        out_shape=[jax.ShapeDtypeStruct((bsz, c3d.shape[1], D_C), BF16),
                   jax.ShapeDtypeStruct((bsz, H_C, ST_ROWS, PAIR), F32),
                   jax.ShapeDtypeStruct((bsz, 1, 128), F32)],
        scratch_shapes=[pltpu.VMEM((H_C, ST_ROWS, PAIR), F32), pltpu.VMEM((1, 128), F32)]
        + pad_scratch,
        compiler_params=_cparams(2),
        name="mlstm",
    )(c3d, gcol, grow, fb8.reshape(1, N_GATE), fb8.reshape(N_GATE, 1), mlstm_g.reshape(D_C, 1),
      s0, m0p)
    odd_head = (jnp.arange(H_C) % 2 == 1)[None, :, None, None]
    own_n = jnp.where(odd_head, sn[..., DK_C:], sn[..., :DK_C])
    c_new = jnp.swapaxes(own_n[:, :, :DV_C, :], 2, 3)
    n_new = own_n[:, :, DV_C, :]
    return y[:, :t_valid], c_new, n_new, mn[:, 0, :H_C]


def _outproj_body(x_ref, ya_ref, az_ref, yb_ref, yc_ref, pw_ref, w_ref, g_ref, o_ref):
    ya = jnp.dot(ya_ref[...].astype(BF16), pw_ref[...], preferred_element_type=F32)
    ya = ya * _silu(az_ref[...])
    out = jnp.dot(ya.astype(BF16), w_ref[0:C_A, :], preferred_element_type=F32)
    out = out + jnp.dot(yb_ref[...].astype(BF16), w_ref[C_A:C_A + D_B, :], preferred_element_type=F32)
    out = out + jnp.dot(yc_ref[...].astype(BF16), w_ref[C_A + D_B:, :], preferred_element_type=F32)
    y = out * lax.rsqrt(jnp.mean(out * out, axis=-1, keepdims=True) + 1e-6) * g_ref[...]
    o_ref[...] = x_ref[...] + y


def _outproj(x2d, ya, a2d, yb, yc, conv_pw, w_out, ln_post):
    rows, d = x2d.shape
    tm = _pick(rows, (1024, 512, 256, 128))
    row = lambda i: (i, 0)
    fixed = lambda i: (0, 0)
    return pl.pallas_call(
        _outproj_body,
        grid=(rows // tm,),
        in_specs=[pl.BlockSpec((tm, d), row),
                  pl.BlockSpec((tm, C_A), row),
                  pl.BlockSpec((tm, C_A), lambda i: (i, 2)),
                  pl.BlockSpec((tm, D_B), row),
                  pl.BlockSpec((tm, D_C), row),
                  pl.BlockSpec((C_A, C_A), fixed),
                  pl.BlockSpec((C_A + D_B + D_C, d), fixed),
                  pl.BlockSpec((1, d), fixed)],
        out_specs=pl.BlockSpec((tm, d), row),
        out_shape=jax.ShapeDtypeStruct((rows, d), F32),
        compiler_params=_cparams(1),
        name="outproj",
    )(x2d, ya, a2d, yb, yc, conv_pw.astype(BF16), w_out.astype(BF16), ln_post.reshape(1, d))


def _layer(x, lidx, conv_state, mstate, attn, flash_operands, weights, mlstm_chunk):
    (ln_pre, w_in, b_in, conv_w, conv_b, conv_ln_g, conv_ln_b, conv_pw, lam_qk, attn_g, f_bias,
     mlstm_g, w_out, ln_post) = weights
    bsz, t_len, d = x.shape
    rows = bsz * t_len
    lam_init = 0.8 - 0.6 * math.exp(-0.3 * lidx)
    x2d = x.reshape(rows, d)
    a, q, k, v, z, c, gcol, grow, *flash = _inproj(x2d, ln_pre, w_in, b_in, flash_operands)

    ya, conv_new = _conv(a.reshape(bsz, t_len, 3 * C_A), conv_state, conv_w, conv_b, conv_ln_g,
                         conv_ln_b)
    yb = attn(q, k, v, z, lam_qk, attn_g, lam_init, *flash)
    grow3 = jnp.swapaxes(grow.reshape(N_GATE, bsz, t_len), 0, 1)
    yc, c_new, n_new, m_new = _mlstm(c.reshape(bsz, t_len, 5 * D_C), gcol.reshape(bsz, t_len, N_GATE),
                                     grow3, f_bias, mlstm_g, *mstate, mlstm_chunk)
    x_new = _outproj(x2d, ya.reshape(rows, C_A), a, yb.reshape(rows, D_B), yc.reshape(rows, D_C),
                     conv_pw, w_out, ln_post)
    return (x_new.reshape(bsz, t_len, d), k.reshape(bsz, t_len, H_B, 2 * DK_B),
            v.reshape(bsz, t_len, H_B, DV_B), conv_new, (c_new, n_new, m_new))


def kernel(x_prompt, x_sample, cache_k, cache_v, state_conv, state_C, state_n, state_m, page_table,
           ln_pre, w_in, b_in, conv_w, conv_b, conv_ln_g, conv_ln_b, conv_pw, lam_qk, attn_g, f_bias,
           mlstm_g, w_out, ln_post):
    depth = w_in.shape[0]
    bp, tp, _ = x_prompt.shape
    bs, ts, _ = x_sample.shape
    assert bp == 1, "prompt attention sweep is written for a single prompt sequence"
    n_pool, page = cache_k.shape[1], cache_k.shape[2]
    ck = cache_k.reshape(depth, n_pool, page * H_B, DV_B)
    cv = cache_v.reshape(depth, n_pool, page * H_B, DV_B)

    def weights(l):
        return (ln_pre[l], w_in[l], b_in[l], conv_w[l], conv_b[l], conv_ln_g[l], conv_ln_b[l],
                conv_pw[l], lam_qk[l], attn_g[l], f_bias[l], mlstm_g[l], w_out[l], ln_post[l])

    def prompt_attn(q, k, v, z, lq, g, lam_init, kb, vt):
        return _attn_prompt(q, kb, vt, z, lq, g, lam_init)

    xp = x_prompt
    outs_p = [[] for _ in range(6)]
    for l in range(depth):
        buf0 = jnp.zeros((bp, CONV_W - 1, C_A), F32)
        m0 = (jnp.zeros((bp, H_C, DK_C, DV_C), F32), jnp.zeros((bp, H_C, DK_C), F32),
              jnp.zeros((bp, H_C), F32))
        xp, kr, vr, cb, (cm, nv, mv) = _layer(xp, l, buf0, m0, prompt_attn, True, weights(l),
                                              MLSTM_CHUNK)
        for lst, val in zip(outs_p, (kr, vr, cb, cm, nv, mv)):
            lst.append(val)

    xs = x_sample
    outs_s = [[] for _ in range(6)]
    for l in range(depth):
        def sample_attn(q, k, v, z, lq, g, lam_init, l=l):
            return _attn_decode(q.reshape(bs, ts, D_B), k.reshape(bs, ts, D_B), v.reshape(bs, ts, D_B),
                                z.reshape(bs, ts, D_B), ck, cv, page_table, lq, g, lam_init, l)

        st = (state_C[l], state_n[l], state_m[l])
        xs, kr, vr, cb, (cm, nv, mv) = _layer(xs, l, state_conv[l], st, sample_attn, False,
                                              weights(l), MLSTM_CHUNK)
        for lst, val in zip(outs_s, (kr, vr, cb, cm, nv, mv)):
            lst.append(val)

    return (xp, xs, *[jnp.stack(o) for o in outs_p], *[jnp.stack(o) for o in outs_s])
```

```python
import functools
import math

import jax
import jax.numpy as jnp
from jax import lax
from jax.experimental import pallas as pl
from jax.experimental.pallas import tpu as pltpu

F32 = jnp.float32
BF16 = jnp.bfloat16

C_A = 256
CONV_W = 31
H_B = 4
DK_B = 64
DV_B = 128
D_B = H_B * DV_B
H_C = 4
DK_C = 64
DV_C = 64
D_C = H_C * DV_C

OFF_A = 0
OFF_Q = 3 * C_A
OFF_K = OFF_Q + 2 * H_B * DK_B
OFF_V = OFF_K + 2 * H_B * DK_B
OFF_Z = OFF_V + D_B
OFF_C = OFF_Z + D_B
OFF_G = OFF_C + 5 * D_C
N_MAIN = OFF_G
N_GATE = 2 * H_C

HIST = 32
NEG = -1e30
QK_SCALE = DK_B ** -0.5
KC_SCALE = DK_C ** -0.5
LOG2E = math.log2(math.e)

VMEM_LIMIT_BYTES = 48 * 1024 * 1024


def _cparams(n_axes):
    return pltpu.CompilerParams(dimension_semantics=("arbitrary",) * n_axes,
                                vmem_limit_bytes=VMEM_LIMIT_BYTES)


def _pick(n, candidates):
    for c in candidates:
        if n % c == 0:
            return c
    return n


def _sigmoid(x):
    return 1.0 / (1.0 + jnp.exp(-x))


def _silu(x):
    return x * _sigmoid(x)


def _log_sigmoid(x):
    return jnp.minimum(x, 0.0) - jnp.log(1.0 + jnp.exp(-jnp.abs(x)))


def _inproj_body(x_ref, g_ref, w_ref, b_ref, wg_ref, wgt_ref, bgc_ref, bgr_ref, *rest, flash_operands):
    a_ref, q_ref, k_ref, v_ref, z_ref, c_ref, gc_ref, gr_ref = rest[:8]
    xf = x_ref[...]
    h = xf * lax.rsqrt(jnp.mean(xf * xf, axis=-1, keepdims=True) + 1e-6) * g_ref[...]
    hb = h.astype(BF16)
    rows_last = (((1,), (1,)), ((), ()))

    def seg(lo, hi):
        return jnp.dot(hb, w_ref[:, lo:hi], preferred_element_type=F32) + b_ref[:, lo:hi]

    a_ref[...] = seg(OFF_A, OFF_Q)
    q_scale = QK_SCALE * LOG2E if flash_operands else QK_SCALE
    q_ref[...] = (seg(OFF_Q, OFF_K) * q_scale).astype(BF16)
    kk = seg(OFF_K, OFF_V)
    vv = seg(OFF_V, OFF_Z)
    tm = kk.shape[0]
    for h in range(H_B):
        k_ref[pl.ds(h, tm, stride=H_B), :] = kk[:, h * DV_B:(h + 1) * DV_B]
        v_ref[pl.ds(h, tm, stride=H_B), :] = vv[:, h * DV_B:(h + 1) * DV_B]
    z_ref[...] = seg(OFF_Z, OFF_C)
    c_ref[...] = seg(OFF_C, OFF_G)
    gc_ref[...] = jnp.dot(hb, wg_ref[...], preferred_element_type=F32) + bgc_ref[...]
    gr_ref[...] = lax.dot_general(wgt_ref[...], hb, rows_last, preferred_element_type=F32) + bgr_ref[...]
    if flash_operands:
        kb_ref, vt_ref = rest[8:]
        kb_ref[...] = kk.astype(BF16)
        vt_ref[...] = vv.T.astype(BF16)


def _inproj(x2d, ln_pre, w_in, b_in, flash_operands):
    rows, d = x2d.shape
    tm = _pick(rows, (512, 256, 128))
    w_main = w_in[:, :N_MAIN].astype(BF16)
    w_gate = w_in[:, N_MAIN:].astype(BF16)
    b_main = b_in[:N_MAIN].reshape(1, N_MAIN)
    b_gate = b_in[N_MAIN:]
    row = lambda i: (i, 0)
    col = lambda i: (0, i)
    fixed = lambda i: (0, 0)
    head_rows = jax.ShapeDtypeStruct((rows * H_B, DV_B), F32)
    head_spec = pl.BlockSpec((tm * H_B, DV_B), row)
    out_shape = [jax.ShapeDtypeStruct((rows, OFF_Q - OFF_A), F32),
                 jax.ShapeDtypeStruct((rows, OFF_K - OFF_Q), BF16),
                 head_rows, head_rows,
                 jax.ShapeDtypeStruct((rows, OFF_C - OFF_Z), F32),
                 jax.ShapeDtypeStruct((rows, OFF_G - OFF_C), F32)]
    out_specs = [pl.BlockSpec((tm, OFF_Q - OFF_A), row), pl.BlockSpec((tm, OFF_K - OFF_Q), row),
                 head_spec, head_spec,
                 pl.BlockSpec((tm, OFF_C - OFF_Z), row), pl.BlockSpec((tm, OFF_G - OFF_C), row)]
    out_shape += [jax.ShapeDtypeStruct((rows, N_GATE), F32), jax.ShapeDtypeStruct((N_GATE, rows), F32)]
    out_specs += [pl.BlockSpec((tm, N_GATE), row), pl.BlockSpec((N_GATE, tm), col)]
    in_specs = [pl.BlockSpec((tm, d), row),
                pl.BlockSpec((1, d), fixed),
                pl.BlockSpec((d, N_MAIN), fixed),
                pl.BlockSpec((1, N_MAIN), fixed),
                pl.BlockSpec((d, N_GATE), fixed),
                pl.BlockSpec((N_GATE, d), fixed),
                pl.BlockSpec((1, N_GATE), fixed),
                pl.BlockSpec((N_GATE, 1), fixed)]
    args = [x2d, ln_pre.reshape(1, d), w_main, b_main, w_gate, w_gate.T,
            b_gate.reshape(1, N_GATE), b_gate.reshape(N_GATE, 1)]
    if flash_operands:
        out_shape += [jax.ShapeDtypeStruct((rows, D_B), BF16), jax.ShapeDtypeStruct((D_B, rows), BF16)]
        out_specs += [pl.BlockSpec((tm, D_B), row), pl.BlockSpec((D_B, tm), col)]
    return pl.pallas_call(
        functools.partial(_inproj_body, flash_operands=flash_operands),
        grid=(rows // tm,),
        in_specs=in_specs,
        out_specs=out_specs,
        out_shape=out_shape,
        compiler_params=_cparams(1),
        name="inproj",
    )(*args)


SUBLANES = 8


def _conv_body(a_ref, st_ref, cw_ref, cb_ref, lg_ref, lb_ref, y_ref, hist_ref, fbuf, shifted,
               *, tt, rc):
    t = pl.program_id(1)

    @pl.when(t == 0)
    def _():
        fbuf[0:HIST, :] = st_ref[0]

    a = a_ref[0]
    fbuf[HIST:HIST + tt, :] = a[:, 0:C_A] * _sigmoid(a[:, C_A:2 * C_A])
    n_rows = shifted.shape[1]
    for k in range(1, SUBLANES):
        shifted[k] = fbuf[pl.ds(k, n_rows), :]
    first = HIST - (CONV_W - 1)
    for r0 in range(0, tt, rc):
        acc = jnp.zeros((rc, C_A), F32)
        for j in range(CONV_W):
            k = (first + j) % SUBLANES
            base = first + j - k + r0
            win = fbuf[base:base + rc, :] if k == 0 else shifted[k, base:base + rc, :]
            acc = acc + cw_ref[j:j + 1, :] * win
        y = acc + cb_ref[...]
        mu = jnp.mean(y, axis=-1, keepdims=True)
        yc = y - mu
        yn = yc * lax.rsqrt(jnp.mean(yc * yc, axis=-1, keepdims=True) + 1e-5)
        y_ref[0, r0:r0 + rc, :] = _silu(yn * lg_ref[...] + lb_ref[...]).astype(y_ref.dtype)
    new_hist = fbuf[pl.ds(tt, HIST), :]
    hist_ref[0] = new_hist
    fbuf[0:HIST, :] = new_hist


def _conv(a3d, state, conv_w, conv_b, ln_g, ln_b):
    bsz, t_len, _ = a3d.shape
    tt = _pick(t_len, (256, 128, 64, 32, 16, 8))
    rc = min(tt, 64)
    buf_rows = HIST + max(tt, SUBLANES)
    st = jnp.pad(state, ((0, 0), (HIST - (CONV_W - 1), 0), (0, 0)))
    fixed = lambda b, t: (0, 0)
    y, hist = pl.pallas_call(
        functools.partial(_conv_body, tt=tt, rc=rc),
        grid=(bsz, t_len // tt),
        in_specs=[pl.BlockSpec((1, tt, 3 * C_A), lambda b, t: (b, t, 0)),
                  pl.BlockSpec((1, HIST, C_A), lambda b, t: (b, 0, 0)),
                  pl.BlockSpec((CONV_W, C_A), fixed),
                  pl.BlockSpec((1, C_A), fixed),
                  pl.BlockSpec((1, C_A), fixed),
                  pl.BlockSpec((1, C_A), fixed)],
        out_specs=[pl.BlockSpec((1, tt, C_A), lambda b, t: (b, t, 0)),
                   pl.BlockSpec((1, HIST, C_A), lambda b, t: (b, 0, 0))],
        out_shape=[jax.ShapeDtypeStruct((bsz, t_len, C_A), BF16),
                   jax.ShapeDtypeStruct((bsz, HIST, C_A), F32)],
        scratch_shapes=[pltpu.VMEM((buf_rows, C_A), F32),
                        pltpu.VMEM((SUBLANES, buf_rows - SUBLANES, C_A), F32)],
        compiler_params=_cparams(2),
        name="conv",
    )(a3d, st, conv_w, conv_b.reshape(1, C_A), ln_g.reshape(1, C_A), ln_b.reshape(1, C_A))
    return y, hist[:, HIST - (CONV_W - 1):, :]


def _lambda(lq, lam_init):
    s01 = jnp.sum(lq[0:1, :] * lq[1:2, :], axis=-1, keepdims=True)
    s23 = jnp.sum(lq[2:3, :] * lq[3:4, :], axis=-1, keepdims=True)
    return jnp.exp(s01) - jnp.exp(s23) + lam_init


def _attn_finish(o, g, z, lam_init):
    y = o * lax.rsqrt(jnp.mean(o * o, axis=-1, keepdims=True) + 1e-6) * g * (1.0 - lam_init)
    return y * _silu(z)


SUM_ROWS = 16


def _attn_body(it_ref, jt_ref, q_ref, k_ref, vt_ref, z_ref, lq_ref, g_ref, o_ref,
               m_scr, acc_scr, *, blk, kb_step, lam_init):
    step_id = pl.program_id(0)
    i = it_ref[step_id]
    jj = jt_ref[step_id]

    @pl.when(jj == 0)
    def _():
        m_scr[...] = jnp.full(m_scr.shape, NEG, F32)
        acc_scr[...] = jnp.zeros(acc_scr.shape, F32)

    def sweep(diagonal, sub):
        keys = pl.ds(pl.multiple_of(sub * blk, blk), blk)
        if diagonal:
            kpos = lax.broadcasted_iota(jnp.int32, (blk, 2 * blk), 0)
            qpos = lax.broadcasted_iota(jnp.int32, (blk, 2 * blk), 1) % blk
            keep = kpos <= qpos
        lane = lax.broadcasted_iota(jnp.int32, (blk, DV_B), 1)
        ones = jnp.ones((SUM_ROWS, blk), BF16)

        def scores(h):
            hs = slice(h * DV_B, (h + 1) * DV_B)
            qh = q_ref[:, hs]
            zero = jnp.zeros_like(qh)
            q12 = jnp.concatenate([jnp.where(lane < DK_B, qh, zero), jnp.where(lane < DK_B, zero, qh)],
                                  axis=0)
            st = lax.dot_general(k_ref[keys, hs], q12, (((1,), (1,)), ((), ())),
                                 preferred_element_type=F32)
            if diagonal:
                st = jnp.where(keep, st, NEG)
            m_prev = m_scr[h:h + 1, :]
            m_new = jnp.maximum(m_prev, jnp.max(st, axis=0, keepdims=True))
            return st, m_prev, m_new

        def values(h, pt, alpha):
            vth = jnp.concatenate([vt_ref[h * DV_B:(h + 1) * DV_B, keys], ones], axis=0)
            acc_scr[h] = alpha * acc_scr[h] + jnp.dot(vth, pt, preferred_element_type=F32)

        nxt = scores(0)
        pending = None
        for h in range(H_B):
            st, m_prev, m_new = nxt
            if h + 1 < H_B:
                nxt = scores(h + 1)
            alpha = jnp.exp2(m_prev - m_new)
            pt = jnp.exp2(st - m_new).astype(BF16)
            m_scr[h:h + 1, :] = m_new
            if pending is not None:
                values(*pending)
            pending = (h, pt, alpha)
        values(*pending)

    first = jj * kb_step
    n_below = jnp.clip(i - first, 0, kb_step)

    def below(sub, carry):
        sweep(False, sub)
        return carry

    lax.fori_loop(0, n_below, below, 0)

    @pl.when(i - first < kb_step)
    def _():
        sweep(True, i - first)
        lam = _lambda(lq_ref[...], lam_init)
        for h in range(H_B):
            hs = slice(h * DV_B, (h + 1) * DV_B)
            a1 = acc_scr[h, 0:DV_B, 0:blk] * (1.0 / acc_scr[h, DV_B:DV_B + 1, 0:blk])
            a2 = acc_scr[h, 0:DV_B, blk:2 * blk] * (1.0 / acc_scr[h, DV_B:DV_B + 1, blk:2 * blk])
            ot = a1 - lam * a2
            yt = ot * lax.rsqrt(jnp.mean(ot * ot, axis=0, keepdims=True) + 1e-6) * g_ref[...]
            o_ref[:, hs] = (yt.T * (1.0 - lam_init) * _silu(z_ref[:, hs])).astype(o_ref.dtype)


def _attn_prompt(q, kb, vt, z, lam_qk, attn_g, lam_init):
    t_len = q.shape[0]
    blk = _pick(t_len, (1024, 512, 256, 128))
    nb = t_len // blk
    kb_step = _pick(nb, (2, 1))
    pairs = [(i, jj) for i in range(nb) for jj in range(i // kb_step + 1)]
    i_tab = jnp.asarray([p[0] for p in pairs], jnp.int32)
    j_tab = jnp.asarray([p[1] for p in pairs], jnp.int32)
    q_map = lambda s, it, jt: (it[s], 0)
    k_map = lambda s, it, jt: (jt[s], 0)
    vt_map = lambda s, it, jt: (0, jt[s])
    fixed = lambda s, it, jt: (0, 0)
    return pl.pallas_call(
        functools.partial(_attn_body, blk=blk, kb_step=kb_step, lam_init=lam_init),
        grid_spec=pltpu.PrefetchScalarGridSpec(
            num_scalar_prefetch=2,
            grid=(len(pairs),),
            in_specs=[pl.BlockSpec((blk, D_B), q_map),
                      pl.BlockSpec((kb_step * blk, D_B), k_map),
                      pl.BlockSpec((D_B, kb_step * blk), vt_map),
                      pl.BlockSpec((blk, D_B), q_map),
                      pl.BlockSpec((4, DK_B), fixed),
                      pl.BlockSpec((DV_B, 1), fixed)],
            out_specs=pl.BlockSpec((blk, D_B), q_map),
            scratch_shapes=[pltpu.VMEM((H_B, 2 * blk), F32),
                            pltpu.VMEM((H_B, DV_B + SUM_ROWS, 2 * blk), F32)]),
        out_shape=jax.ShapeDtypeStruct((t_len, D_B), BF16),
        compiler_params=_cparams(1),
        name="attn_prompt",
    )(i_tab, j_tab, q, kb, vt, z, lam_qk, attn_g.reshape(DV_B, 1))


QROWS = 16


QR = H_B * QROWS


def _decode_body(pt_ref, q_ref, kn_ref, vn_ref, z_ref, lq_ref, g_ref, *rest,
                 n_pages_step, t_new, lam_init):
    k_refs = rest[:n_pages_step]
    v_refs = rest[n_pages_step:2 * n_pages_step]
    o_ref, m_scr, l_scr, acc_scr = rest[2 * n_pages_step:]
    g = pl.program_id(1)
    n_groups = pl.num_programs(1)

    @pl.when(g == 0)
    def _():
        m_scr[...] = jnp.full(m_scr.shape, NEG, F32)
        l_scr[...] = jnp.zeros(l_scr.shape, F32)
        acc_scr[...] = jnp.zeros(acc_scr.shape, F32)

    q64 = q_ref[0]
    row = lax.broadcasted_iota(jnp.int32, q64.shape, 0)
    lane = lax.broadcasted_iota(jnp.int32, q64.shape, 1)
    qd = jnp.where((lane < DK_B) == ((row % QROWS) < QROWS // 2), q64, jnp.zeros_like(q64))

    page = k_refs[0].shape[0] // H_B

    def head_rows(refs, h):
        return jnp.concatenate([r[pl.ds(h, page, stride=H_B), :] for r in refs], axis=0).astype(BF16)

    heads = range(H_B)
    rows = [slice(h * QROWS, (h + 1) * QROWS) for h in heads]
    s = [lax.dot_general(qd[rows[h], :], head_rows(k_refs, h), (((1,), (1,)), ((), ())),
                         preferred_element_type=F32) for h in heads]
    m_prev = [m_scr[rows[h], :] for h in heads]
    m_new = [jnp.maximum(m_prev[h], jnp.max(s[h], axis=-1, keepdims=True)) for h in heads]
    alpha = [jnp.exp(m_prev[h] - m_new[h]) for h in heads]
    p = [jnp.exp(s[h] - m_new[h]) for h in heads]
    pv = [jnp.dot(p[h].astype(BF16), head_rows(v_refs, h), preferred_element_type=F32) for h in heads]
    for h in heads:
        l_scr[rows[h], :] = alpha[h] * l_scr[rows[h], :] + jnp.sum(p[h], axis=-1, keepdims=True)
        acc_scr[rows[h], :] = alpha[h] * acc_scr[rows[h], :] + pv[h]
        m_scr[rows[h], :] = m_new[h]

    @pl.when(g == n_groups - 1)
    def _():
        lam = _lambda(lq_ref[...], lam_init)
        qf = qd.astype(F32)
        tok = lax.broadcasted_iota(jnp.int32, (QR, 1), 0) % (QROWS // 2)
        s_new = []
        for jn in range(t_new):
            sj = jnp.sum(qf * kn_ref[0, jn].astype(BF16).astype(F32), axis=-1, keepdims=True)
            s_new.append(jnp.where(tok >= jn, sj, NEG))
        m_prev = m_scr[...]
        m_fin = m_prev
        for sj in s_new:
            m_fin = jnp.maximum(m_fin, sj)
        alpha = jnp.exp(m_prev - m_fin)
        l_fin = alpha * l_scr[...]
        acc = alpha * acc_scr[...]
        for jn, sj in enumerate(s_new):
            pj = jnp.exp(sj - m_fin)
            l_fin = l_fin + pj
            acc = acc + pj.astype(BF16).astype(F32) * vn_ref[0, jn].astype(BF16).astype(F32)
        a = acc / l_fin
        half = QROWS // 2
        for h in range(H_B):
            hs = slice(h * DV_B, (h + 1) * DV_B)
            o = a[h * QROWS:h * QROWS + half, :] - lam * a[h * QROWS + half:(h + 1) * QROWS, :]
            o_ref[0, :, hs] = _attn_finish(o, g_ref[...], z_ref[0, :, hs], lam_init)


def _attn_decode(q, k_new, v_new, z, cache_k_l, cache_v_l, page_table, lam_qk, attn_g, lam_init,
                 layer):
    bsz, t_new, _ = q.shape
    n_pages = page_table.shape[1]
    page_rows = cache_k_l.shape[2]
    pps = _pick(n_pages, (16, 8, 4, 2, 1))
    half = QROWS // 2
    qh = jnp.swapaxes(q.reshape(bsz, t_new, H_B, DV_B), 1, 2)
    q64 = jnp.zeros((bsz, H_B, QROWS, DV_B), BF16)
    q64 = q64.at[:, :, 0:t_new].set(qh).at[:, :, half:half + t_new].set(qh).reshape(bsz, QR, DV_B)
    on_rows = lambda x: jnp.repeat(x.reshape(bsz, t_new, H_B, DV_B), QROWS, axis=2)
    z8 = jnp.pad(z, ((0, 0), (0, half - t_new), (0, 0)))
    fixed = lambda b, g, pt: (0, 0)
    per_b = lambda b, g, pt: (b, 0, 0)
    per_b4 = lambda b, g, pt: (b, 0, 0, 0)

    def page_spec(jp):
        return pl.BlockSpec((None, None, page_rows, DV_B),
                            lambda b, g, pt: (layer, pt[b, g * pps + jp], 0, 0))

    in_specs = [pl.BlockSpec((1, QR, DV_B), per_b),
                pl.BlockSpec((1, t_new, QR, DV_B), per_b4),
                pl.BlockSpec((1, t_new, QR, DV_B), per_b4),
                pl.BlockSpec((1, half, D_B), per_b),
                pl.BlockSpec((4, DK_B), fixed),
                pl.BlockSpec((1, DV_B), fixed)]
    in_specs += [page_spec(jp) for jp in range(pps)] * 2
    out = pl.pallas_call(
        functools.partial(_decode_body, n_pages_step=pps, t_new=t_new, lam_init=lam_init),
        grid_spec=pltpu.PrefetchScalarGridSpec(
            num_scalar_prefetch=1,
            grid=(bsz, n_pages // pps),
            in_specs=in_specs,
            out_specs=pl.BlockSpec((1, half, D_B), per_b),
            scratch_shapes=[pltpu.VMEM((QR, 1), F32), pltpu.VMEM((QR, 1), F32),
                            pltpu.VMEM((QR, DV_B), F32)]),
        out_shape=jax.ShapeDtypeStruct((bsz, half, D_B), F32),
        compiler_params=_cparams(2),
        name="attn_decode",
    )(page_table, q64, on_rows(k_new), on_rows(v_new), z8, lam_qk, attn_g.reshape(1, DV_B),
      *([cache_k_l] * pps), *([cache_v_l] * pps))
    return out[:, :t_new, :]


MLSTM_CHUNK = 128
MLSTM_LONG_CHUNK = 256
PAIR = 2 * DK_C
ONES_ROWS = 16
ST_ROWS = DV_C + ONES_ROWS


def _mlstm_body(c_ref, gc_ref, gr_ref, fbc_ref, fbr_ref, mg_ref, s0_ref, m0_ref,
                y_ref, sn_ref, mn_ref, s_scr, m_scr, *pad_scr, ln, t_valid, t_pad):
    t = pl.program_id(1)
    n_t = pl.num_programs(1)

    @pl.when(t == 0)
    def _():
        s_scr[...] = s0_ref[0]
        m_scr[...] = m0_ref[0]

    rows_in = c_ref.shape[1]
    if rows_in == ln:
        c_blk, gcol, grow = c_ref.at[0], gc_ref[0], gr_ref[0]
    else:
        c_blk, gc_pad, gr_pad = pad_scr

        @pl.when(pl.program_id(0) == 0)
        def _():
            c_blk[...] = jnp.zeros(c_blk.shape, F32)
            gc_pad[...] = jnp.zeros(gc_pad.shape, F32)
            gr_pad[...] = jnp.zeros(gr_pad.shape, F32)

        c_blk[0:rows_in, :] = c_ref[0]
        gc_pad[0:rows_in, :] = gc_ref[0]
        gr_pad[:, 0:rows_in] = gr_ref[0]
        gcol, grow = gc_pad[...], gr_pad[...]
    is_f_c = lax.broadcasted_iota(jnp.int32, (ln, N_GATE), 1) >= H_C
    is_f_r = lax.broadcasted_iota(jnp.int32, (N_GATE, ln), 0) >= H_C
    g_c = jnp.where(is_f_c, _log_sigmoid(gcol + fbc_ref[...]), gcol)
    g_r = jnp.where(is_f_r, _log_sigmoid(grow + fbr_ref[...]), grow)
    if t_pad != t_valid:
        ok_c = (t * ln + lax.broadcasted_iota(jnp.int32, (ln, N_GATE), 0)) < t_valid
        ok_r = (t * ln + lax.broadcasted_iota(jnp.int32, (N_GATE, ln), 1)) < t_valid
        g_c = jnp.where(ok_c, g_c, jnp.where(is_f_c, 0.0, NEG))
        g_r = jnp.where(ok_r, g_r, jnp.where(is_f_r, 0.0, NEG))
    rr = lax.broadcasted_iota(jnp.int32, (ln, ln), 0)
    cc = lax.broadcasted_iota(jnp.int32, (ln, ln), 1)
    tri = (rr >= cc).astype(F32)
    src_before = rr <= cc
    cum_c = jnp.dot(tri, jnp.where(is_f_c, g_c, 0.0), preferred_element_type=F32,
                    precision=lax.Precision.HIGHEST)
    cum_r = lax.dot_general(jnp.where(is_f_r, g_r, 0.0), tri, (((1,), (1,)), ((), ())),
                            preferred_element_type=F32, precision=lax.Precision.HIGHEST)
    m_all = m_scr[...]
    lane1 = lax.broadcasted_iota(jnp.int32, (1, 128), 1)
    lane_k = lax.broadcasted_iota(jnp.int32, (ln, PAIR), 1)
    m_out = jnp.zeros((1, 128), F32)
    ones = jnp.ones((ONES_ROWS, ln), F32)

    for p in range(H_C // 2):
        def pair_block(seg):
            return c_blk[:, seg * D_C + p * PAIR:seg * D_C + (p + 1) * PAIR]

        q_t = pair_block(0).T.astype(BF16)
        k_p = pair_block(1) * KC_SCALE
        v_t = pair_block(2).T
        o_t = pair_block(3).T
        z_t = pair_block(4).T
        y_halves = []
        for half in range(2):
            h = 2 * p + half
            rs = slice(half * DK_C, (half + 1) * DK_C)
            k_own = jnp.where((lane_k < DK_C) == (half == 0), k_p, 0.0).astype(BF16)
            a_c = g_c[:, h:h + 1] - cum_c[:, H_C + h:H_C + h + 1]
            br = cum_r[H_C + h:H_C + h + 1, :]
            igr = g_r[h:h + 1, :]
            m_prev = m_all[:, h:h + 1]
            s_mat = s_scr[h]

            log_d = jnp.where(src_before, br + a_c, NEG)
            inter = br + m_prev
            m_t = jnp.maximum(inter, jnp.max(log_d, axis=0, keepdims=True))
            d_w = jnp.where(src_before, jnp.exp(log_d - m_t), 0.0)
            i_w = jnp.exp(inter - m_t)
            qk = jnp.dot(k_own, q_t, preferred_element_type=F32)
            w = (d_w * qk).astype(BF16)
            v1 = jnp.concatenate([v_t[rs, :], ones], axis=0)
            nd = (jnp.dot(v1.astype(BF16), w, preferred_element_type=F32)
                  + i_w * jnp.dot(s_mat.astype(BF16), q_t, preferred_element_type=F32))
            den = nd[DV_C:DV_C + 1, :]
            hh = nd[0:DV_C, :] / jnp.maximum(jnp.abs(den), jnp.exp(-m_t))

            b_last = cum_c[ln - 1:ln, H_C + h:H_C + h + 1]
            m_last = jnp.maximum(b_last + m_prev, b_last + jnp.max(a_c, axis=0, keepdims=True))
            i_last = jnp.exp(b_last + m_prev - m_last)
            w_last = jnp.exp(b_last - br + igr - m_last)
            s_scr[h] = i_last * s_mat + jnp.dot((v1 * w_last).astype(BF16), k_own,
                                                preferred_element_type=F32)
            m_out = jnp.where(lane1 == h, m_last, m_out)

            hg = _sigmoid(o_t[rs, :]) * hh
            yh = hg * lax.rsqrt(jnp.mean(hg * hg, axis=0, keepdims=True) + 1e-6)
            yh = yh * mg_ref[h * DV_C:(h + 1) * DV_C, :]
            y_halves.append(yh * _silu(z_t[rs, :]))
        y_pair = jnp.concatenate(y_halves, axis=0)
        y_ref[0, :, p * PAIR:(p + 1) * PAIR] = y_pair.T[0:rows_in, :].astype(y_ref.dtype)

    m_scr[...] = m_out

    @pl.when(t == n_t - 1)
    def _():
        sn_ref[0] = s_scr[...]
        mn_ref[0] = m_scr[...]


def _mlstm(c3d, gcol, grow, f_bias, mlstm_g, c0, n0, m0, ln):
    bsz, t_valid, _ = c3d.shape
    t_pad = -(-t_valid // ln) * ln
    pad_in_kernel = t_valid < ln
    rows_blk = t_valid if pad_in_kernel else ln
    pad_scratch = []
    if pad_in_kernel:
        pad_scratch = [pltpu.VMEM((ln, 5 * D_C), F32), pltpu.VMEM((ln, N_GATE), F32),
                       pltpu.VMEM((N_GATE, ln), F32)]
    elif t_pad != t_valid:
        extra = t_pad - t_valid
        c3d = jnp.pad(c3d, ((0, 0), (0, extra), (0, 0)))
        gcol = jnp.pad(gcol, ((0, 0), (0, extra), (0, 0)))
        grow = jnp.pad(grow, ((0, 0), (0, 0), (0, extra)))
    m0p = jnp.pad(m0, ((0, 0), (0, 128 - H_C))).reshape(bsz, 1, 128)
    own = jnp.concatenate([jnp.swapaxes(c0, 2, 3),
                           jnp.broadcast_to(n0[:, :, None, :], (bsz, H_C, ONES_ROWS, DK_C))], axis=2)
    even = jnp.pad(own[:, 0::2], ((0, 0), (0, 0), (0, 0), (0, DK_C)))
    odd = jnp.pad(own[:, 1::2], ((0, 0), (0, 0), (0, 0), (DK_C, 0)))
    s0 = jnp.stack([even, odd], axis=2).reshape(bsz, H_C, ST_ROWS, PAIR)
    fb8 = jnp.concatenate([jnp.zeros((H_C,), F32), f_bias])
    fixed = lambda b, t: (0, 0)
    per_b = lambda b, t: (b, 0, 0, 0)
    y, sn, mn = pl.pallas_call(
        functools.partial(_mlstm_body, ln=ln, t_valid=t_valid, t_pad=t_pad),
        grid=(bsz, t_pad // ln),
        in_specs=[pl.BlockSpec((1, rows_blk, 5 * D_C), lambda b, t: (b, t, 0)),
                  pl.BlockSpec((1, rows_blk, N_GATE), lambda b, t: (b, t, 0)),
                  pl.BlockSpec((1, N_GATE, rows_blk), lambda b, t: (b, 0, t)),
                  pl.BlockSpec((1, N_GATE), fixed),
                  pl.BlockSpec((N_GATE, 1), fixed),
                  pl.BlockSpec((D_C, 1), fixed),
                  pl.BlockSpec((1, H_C, ST_ROWS, PAIR), per_b),
                  pl.BlockSpec((1, 1, 128), lambda b, t: (b, 0, 0))],
        out_specs=[pl.BlockSpec((1, rows_blk, D_C), lambda b, t: (b, t, 0)),
                   pl.BlockSpec((1, H_C, ST_ROWS, PAIR), per_b),
                   pl.BlockSpec((1, 1, 128), lambda b, t: (b, 0, 0))],
        out_shape=[jax.ShapeDtypeStruct((bsz, c3d.shape[1], D_C), BF16),
                   jax.ShapeDtypeStruct((bsz, H_C, ST_ROWS, PAIR), F32),
                   jax.ShapeDtypeStruct((bsz, 1, 128), F32)],
        scratch_shapes=[pltpu.VMEM((H_C, ST_ROWS, PAIR), F32), pltpu.VMEM((1, 128), F32)]
        + pad_scratch,
        compiler_params=_cparams(2),
        name="mlstm",
    )(c3d, gcol, grow, fb8.reshape(1, N_GATE), fb8.reshape(N_GATE, 1), mlstm_g.reshape(D_C, 1),
      s0, m0p)
    odd_head = (jnp.arange(H_C) % 2 == 1)[None, :, None, None]
    own_n = jnp.where(odd_head, sn[..., DK_C:], sn[..., :DK_C])
    c_new = jnp.swapaxes(own_n[:, :, :DV_C, :], 2, 3)
    n_new = own_n[:, :, DV_C, :]
    return y[:, :t_valid], c_new, n_new, mn[:, 0, :H_C]


def _outproj_body(x_ref, ya_ref, az_ref, yb_ref, yc_ref, pw_ref, w_ref, g_ref, o_ref):
    ya = jnp.dot(ya_ref[...].astype(BF16), pw_ref[...], preferred_element_type=F32)
    ya = ya * _silu(az_ref[...])
    out = jnp.dot(ya.astype(BF16), w_ref[0:C_A, :], preferred_element_type=F32)
    out = out + jnp.dot(yb_ref[...].astype(BF16), w_ref[C_A:C_A + D_B, :], preferred_element_type=F32)
    out = out + jnp.dot(yc_ref[...].astype(BF16), w_ref[C_A + D_B:, :], preferred_element_type=F32)
    y = out * lax.rsqrt(jnp.mean(out * out, axis=-1, keepdims=True) + 1e-6) * g_ref[...]
    o_ref[...] = x_ref[...] + y


def _outproj(x2d, ya, a2d, yb, yc, conv_pw, w_out, ln_post):
    rows, d = x2d.shape
    tm = _pick(rows, (1024, 512, 256, 128))
    row = lambda i: (i, 0)
    fixed = lambda i: (0, 0)
    return pl.pallas_call(
        _outproj_body,
        grid=(rows // tm,),
        in_specs=[pl.BlockSpec((tm, d), row),
                  pl.BlockSpec((tm, C_A), row),
                  pl.BlockSpec((tm, C_A), lambda i: (i, 2)),
                  pl.BlockSpec((tm, D_B), row),
                  pl.BlockSpec((tm, D_C), row),
                  pl.BlockSpec((C_A, C_A), fixed),
                  pl.BlockSpec((C_A + D_B + D_C, d), fixed),
                  pl.BlockSpec((1, d), fixed)],
        out_specs=pl.BlockSpec((tm, d), row),
        out_shape=jax.ShapeDtypeStruct((rows, d), F32),
        compiler_params=_cparams(1),
        name="outproj",
    )(x2d, ya, a2d, yb, yc, conv_pw.astype(BF16), w_out.astype(BF16), ln_post.reshape(1, d))


def _layer(x, lidx, conv_state, mstate, attn, flash_operands, weights, mlstm_chunk):
    (ln_pre, w_in, b_in, conv_w, conv_b, conv_ln_g, conv_ln_b, conv_pw, lam_qk, attn_g, f_bias,
     mlstm_g, w_out, ln_post) = weights
    bsz, t_len, d = x.shape
    rows = bsz * t_len
    lam_init = 0.8 - 0.6 * math.exp(-0.3 * lidx)
    x2d = x.reshape(rows, d)
    a, q, k, v, z, c, gcol, grow, *flash = _inproj(x2d, ln_pre, w_in, b_in, flash_operands)

    ya, conv_new = _conv(a.reshape(bsz, t_len, 3 * C_A), conv_state, conv_w, conv_b, conv_ln_g,
                         conv_ln_b)
    yb = attn(q, k, v, z, lam_qk, attn_g, lam_init, *flash)
    grow3 = jnp.swapaxes(grow.reshape(N_GATE, bsz, t_len), 0, 1)
    yc, c_new, n_new, m_new = _mlstm(c.reshape(bsz, t_len, 5 * D_C), gcol.reshape(bsz, t_len, N_GATE),
                                     grow3, f_bias, mlstm_g, *mstate, mlstm_chunk)
    x_new = _outproj(x2d, ya.reshape(rows, C_A), a, yb.reshape(rows, D_B), yc.reshape(rows, D_C),
                     conv_pw, w_out, ln_post)
    return (x_new.reshape(bsz, t_len, d), k.reshape(bsz, t_len, H_B, 2 * DK_B),
            v.reshape(bsz, t_len, H_B, DV_B), conv_new, (c_new, n_new, m_new))


def kernel(x_prompt, x_sample, cache_k, cache_v, state_conv, state_C, state_n, state_m, page_table,
           ln_pre, w_in, b_in, conv_w, conv_b, conv_ln_g, conv_ln_b, conv_pw, lam_qk, attn_g, f_bias,
           mlstm_g, w_out, ln_post):
    depth = w_in.shape[0]
    bp, tp, _ = x_prompt.shape
    bs, ts, _ = x_sample.shape
    assert bp == 1, "prompt attention sweep is written for a single prompt sequence"
    n_pool, page = cache_k.shape[1], cache_k.shape[2]
    ck = cache_k.reshape(depth, n_pool, page * H_B, DV_B)
    cv = cache_v.reshape(depth, n_pool, page * H_B, DV_B)

    def weights(l):
        return (ln_pre[l], w_in[l], b_in[l], conv_w[l], conv_b[l], conv_ln_g[l], conv_ln_b[l],
                conv_pw[l], lam_qk[l], attn_g[l], f_bias[l], mlstm_g[l], w_out[l], ln_post[l])

    def prompt_attn(q, k, v, z, lq, g, lam_init, kb, vt):
        return _attn_prompt(q, kb, vt, z, lq, g, lam_init)

    xp = x_prompt
    outs_p = [[] for _ in range(6)]
    for l in range(depth):
        buf0 = jnp.zeros((bp, CONV_W - 1, C_A), F32)
        m0 = (jnp.zeros((bp, H_C, DK_C, DV_C), F32), jnp.zeros((bp, H_C, DK_C), F32),
              jnp.zeros((bp, H_C), F32))
        xp, kr, vr, cb, (cm, nv, mv) = _layer(xp, l, buf0, m0, prompt_attn, True, weights(l),
                                              _pick(tp, (MLSTM_LONG_CHUNK, MLSTM_CHUNK)))
        for lst, val in zip(outs_p, (kr, vr, cb, cm, nv, mv)):
            lst.append(val)

    xs = x_sample
    outs_s = [[] for _ in range(6)]
    for l in range(depth):
        def sample_attn(q, k, v, z, lq, g, lam_init, l=l):
            return _attn_decode(q.reshape(bs, ts, D_B), k.reshape(bs, ts, D_B), v.reshape(bs, ts, D_B),
                                z.reshape(bs, ts, D_B), ck, cv, page_table, lq, g, lam_init, l)

        st = (state_C[l], state_n[l], state_m[l])
        xs, kr, vr, cb, (cm, nv, mv) = _layer(xs, l, state_conv[l], st, sample_attn, False,
                                              weights(l), MLSTM_CHUNK)
        for lst, val in zip(outs_s, (kr, vr, cb, cm, nv, mv)):
            lst.append(val)

    return (xp, xs, *[jnp.stack(o) for o in outs_p], *[jnp.stack(o) for o in outs_s])
```

```python
import functools
import math

import jax
import jax.numpy as jnp
from jax import lax
from jax.experimental import pallas as pl
from jax.experimental.pallas import tpu as pltpu

F32 = jnp.float32
BF16 = jnp.bfloat16

C_A = 256
CONV_W = 31
H_B = 4
DK_B = 64
DV_B = 128
D_B = H_B * DV_B
H_C = 4
DK_C = 64
DV_C = 64
D_C = H_C * DV_C

OFF_A = 0
OFF_Q = 3 * C_A
OFF_K = OFF_Q + 2 * H_B * DK_B
OFF_V = OFF_K + 2 * H_B * DK_B
OFF_Z = OFF_V + D_B
OFF_C = OFF_Z + D_B
OFF_G = OFF_C + 5 * D_C
N_MAIN = OFF_G
N_GATE = 2 * H_C

HIST = 32
NEG = -1e30
QK_SCALE = DK_B ** -0.5
KC_SCALE = DK_C ** -0.5
LOG2E = math.log2(math.e)

VMEM_LIMIT_BYTES = 48 * 1024 * 1024


def _cparams(n_axes):
    return pltpu.CompilerParams(dimension_semantics=("arbitrary",) * n_axes,
                                vmem_limit_bytes=VMEM_LIMIT_BYTES)


def _pick(n, candidates):
    for c in candidates:
        if n % c == 0:
            return c
    return n


def _sigmoid(x):
    return 1.0 / (1.0 + jnp.exp(-x))


def _silu(x):
    return x * _sigmoid(x)


def _log_sigmoid(x):
    return jnp.minimum(x, 0.0) - jnp.log(1.0 + jnp.exp(-jnp.abs(x)))


def _inproj_body(x_ref, g_ref, w_ref, b_ref, wg_ref, wgt_ref, bgc_ref, bgr_ref, *rest, flash_operands):
    a_ref, q_ref, k_ref, v_ref, z_ref, c_ref, gc_ref, gr_ref = rest[:8]
    xf = x_ref[...]
    h = xf * lax.rsqrt(jnp.mean(xf * xf, axis=-1, keepdims=True) + 1e-6) * g_ref[...]
    hb = h.astype(BF16)
    rows_last = (((1,), (1,)), ((), ()))

    def seg(lo, hi):
        return jnp.dot(hb, w_ref[:, lo:hi], preferred_element_type=F32) + b_ref[:, lo:hi]

    a_ref[...] = seg(OFF_A, OFF_Q)
    q_scale = QK_SCALE * LOG2E if flash_operands else QK_SCALE
    q_ref[...] = (seg(OFF_Q, OFF_K) * q_scale).astype(BF16)
    kk = seg(OFF_K, OFF_V)
    vv = seg(OFF_V, OFF_Z)
    tm = kk.shape[0]
    for h in range(H_B):
        k_ref[pl.ds(h, tm, stride=H_B), :] = kk[:, h * DV_B:(h + 1) * DV_B]
        v_ref[pl.ds(h, tm, stride=H_B), :] = vv[:, h * DV_B:(h + 1) * DV_B]
    z_ref[...] = seg(OFF_Z, OFF_C)
    c_ref[...] = seg(OFF_C, OFF_G)
    gc_ref[...] = jnp.dot(hb, wg_ref[...], preferred_element_type=F32) + bgc_ref[...]
    gr_ref[...] = lax.dot_general(wgt_ref[...], hb, rows_last, preferred_element_type=F32) + bgr_ref[...]
    if flash_operands:
        kb_ref, vt_ref = rest[8:]
        kb_ref[...] = kk.astype(BF16)
        vt_ref[...] = vv.T.astype(BF16)


def _inproj(x2d, ln_pre, w_in, b_in, flash_operands):
    rows, d = x2d.shape
    tm = _pick(rows, (512, 256, 128))
    w_main = w_in[:, :N_MAIN].astype(BF16)
    w_gate = w_in[:, N_MAIN:].astype(BF16)
    b_main = b_in[:N_MAIN].reshape(1, N_MAIN)
    b_gate = b_in[N_MAIN:]
    row = lambda i: (i, 0)
    col = lambda i: (0, i)
    fixed = lambda i: (0, 0)
    head_rows = jax.ShapeDtypeStruct((rows * H_B, DV_B), F32)
    head_spec = pl.BlockSpec((tm * H_B, DV_B), row)
    out_shape = [jax.ShapeDtypeStruct((rows, OFF_Q - OFF_A), F32),
                 jax.ShapeDtypeStruct((rows, OFF_K - OFF_Q), BF16),
                 head_rows, head_rows,
                 jax.ShapeDtypeStruct((rows, OFF_C - OFF_Z), F32),
                 jax.ShapeDtypeStruct((rows, OFF_G - OFF_C), F32)]
    out_specs = [pl.BlockSpec((tm, OFF_Q - OFF_A), row), pl.BlockSpec((tm, OFF_K - OFF_Q), row),
                 head_spec, head_spec,
                 pl.BlockSpec((tm, OFF_C - OFF_Z), row), pl.BlockSpec((tm, OFF_G - OFF_C), row)]
    out_shape += [jax.ShapeDtypeStruct((rows, N_GATE), F32), jax.ShapeDtypeStruct((N_GATE, rows), F32)]
    out_specs += [pl.BlockSpec((tm, N_GATE), row), pl.BlockSpec((N_GATE, tm), col)]
    in_specs = [pl.BlockSpec((tm, d), row),
                pl.BlockSpec((1, d), fixed),
                pl.BlockSpec((d, N_MAIN), fixed),
                pl.BlockSpec((1, N_MAIN), fixed),
                pl.BlockSpec((d, N_GATE), fixed),
                pl.BlockSpec((N_GATE, d), fixed),
                pl.BlockSpec((1, N_GATE), fixed),
                pl.BlockSpec((N_GATE, 1), fixed)]
    args = [x2d, ln_pre.reshape(1, d), w_main, b_main, w_gate, w_gate.T,
            b_gate.reshape(1, N_GATE), b_gate.reshape(N_GATE, 1)]
    if flash_operands:
        out_shape += [jax.ShapeDtypeStruct((rows, D_B), BF16), jax.ShapeDtypeStruct((D_B, rows), BF16)]
        out_specs += [pl.BlockSpec((tm, D_B), row), pl.BlockSpec((D_B, tm), col)]
    return pl.pallas_call(
        functools.partial(_inproj_body, flash_operands=flash_operands),
        grid=(rows // tm,),
        in_specs=in_specs,
        out_specs=out_specs,
        out_shape=out_shape,
        compiler_params=_cparams(1),
        name="inproj",
    )(*args)


SUBLANES = 8


def _conv_body(a_ref, st_ref, cw_ref, cb_ref, lg_ref, lb_ref, y_ref, hist_ref, fbuf, shifted,
               *, tt, rc):
    t = pl.program_id(1)

    @pl.when(t == 0)
    def _():
        fbuf[0:HIST, :] = st_ref[0]

    a = a_ref[0]
    fbuf[HIST:HIST + tt, :] = a[:, 0:C_A] * _sigmoid(a[:, C_A:2 * C_A])
    n_rows = shifted.shape[1]
    for k in range(1, SUBLANES):
        shifted[k] = fbuf[pl.ds(k, n_rows), :]
    first = HIST - (CONV_W - 1)
    for r0 in range(0, tt, rc):
        acc = jnp.zeros((rc, C_A), F32)
        for j in range(CONV_W):
            k = (first + j) % SUBLANES
            base = first + j - k + r0
            win = fbuf[base:base + rc, :] if k == 0 else shifted[k, base:base + rc, :]
            acc = acc + cw_ref[j:j + 1, :] * win
        y = acc + cb_ref[...]
        mu = jnp.mean(y, axis=-1, keepdims=True)
        yc = y - mu
        yn = yc * lax.rsqrt(jnp.mean(yc * yc, axis=-1, keepdims=True) + 1e-5)
        y_ref[0, r0:r0 + rc, :] = _silu(yn * lg_ref[...] + lb_ref[...]).astype(y_ref.dtype)
    new_hist = fbuf[pl.ds(tt, HIST), :]
    hist_ref[0] = new_hist
    fbuf[0:HIST, :] = new_hist


def _conv(a3d, state, conv_w, conv_b, ln_g, ln_b):
    bsz, t_len, _ = a3d.shape
    tt = _pick(t_len, (256, 128, 64, 32, 16, 8))
    rc = min(tt, 64)
    buf_rows = HIST + max(tt, SUBLANES)
    st = jnp.pad(state, ((0, 0), (HIST - (CONV_W - 1), 0), (0, 0)))
    fixed = lambda b, t: (0, 0)
    y, hist = pl.pallas_call(
        functools.partial(_conv_body, tt=tt, rc=rc),
        grid=(bsz, t_len // tt),
        in_specs=[pl.BlockSpec((1, tt, 3 * C_A), lambda b, t: (b, t, 0)),
                  pl.BlockSpec((1, HIST, C_A), lambda b, t: (b, 0, 0)),
                  pl.BlockSpec((CONV_W, C_A), fixed),
                  pl.BlockSpec((1, C_A), fixed),
                  pl.BlockSpec((1, C_A), fixed),
                  pl.BlockSpec((1, C_A), fixed)],
        out_specs=[pl.BlockSpec((1, tt, C_A), lambda b, t: (b, t, 0)),
                   pl.BlockSpec((1, HIST, C_A), lambda b, t: (b, 0, 0))],
        out_shape=[jax.ShapeDtypeStruct((bsz, t_len, C_A), BF16),
                   jax.ShapeDtypeStruct((bsz, HIST, C_A), F32)],
        scratch_shapes=[pltpu.VMEM((buf_rows, C_A), F32),
                        pltpu.VMEM((SUBLANES, buf_rows - SUBLANES, C_A), F32)],
        compiler_params=_cparams(2),
        name="conv",
    )(a3d, st, conv_w, conv_b.reshape(1, C_A), ln_g.reshape(1, C_A), ln_b.reshape(1, C_A))
    return y, hist[:, HIST - (CONV_W - 1):, :]


def _lambda(lq, lam_init):
    s01 = jnp.sum(lq[0:1, :] * lq[1:2, :], axis=-1, keepdims=True)
    s23 = jnp.sum(lq[2:3, :] * lq[3:4, :], axis=-1, keepdims=True)
    return jnp.exp(s01) - jnp.exp(s23) + lam_init


def _attn_finish(o, g, z, lam_init):
    y = o * lax.rsqrt(jnp.mean(o * o, axis=-1, keepdims=True) + 1e-6) * g * (1.0 - lam_init)
    return y * _silu(z)


SUM_ROWS = 16


def _attn_body(it_ref, jt_ref, q_ref, k_ref, vt_ref, z_ref, lq_ref, g_ref, o_ref,
               m_scr, acc_scr, *, blk, kb_step, lam_init):
    step_id = pl.program_id(0)
    i = it_ref[step_id]
    jj = jt_ref[step_id]

    @pl.when(jj == 0)
    def _():
        m_scr[...] = jnp.full(m_scr.shape, NEG, F32)
        acc_scr[...] = jnp.zeros(acc_scr.shape, F32)

    def sweep(diagonal, sub):
        keys = pl.ds(pl.multiple_of(sub * blk, blk), blk)
        if diagonal:
            kpos = lax.broadcasted_iota(jnp.int32, (blk, 2 * blk), 0)
            qpos = lax.broadcasted_iota(jnp.int32, (blk, 2 * blk), 1) % blk
            keep = kpos <= qpos
        lane = lax.broadcasted_iota(jnp.int32, (blk, DV_B), 1)
        ones = jnp.ones((SUM_ROWS, blk), BF16)

        def scores(h):
            hs = slice(h * DV_B, (h + 1) * DV_B)
            qh = q_ref[:, hs]
            zero = jnp.zeros_like(qh)
            q12 = jnp.concatenate([jnp.where(lane < DK_B, qh, zero), jnp.where(lane < DK_B, zero, qh)],
                                  axis=0)
            st = lax.dot_general(k_ref[keys, hs], q12, (((1,), (1,)), ((), ())),
                                 preferred_element_type=F32)
            if diagonal:
                st = jnp.where(keep, st, NEG)
            m_prev = m_scr[h:h + 1, :]
            m_new = jnp.maximum(m_prev, jnp.max(st, axis=0, keepdims=True))
            return st, m_prev, m_new

        def values(h, pt, alpha):
            vth = jnp.concatenate([vt_ref[h * DV_B:(h + 1) * DV_B, keys], ones], axis=0)
            acc_scr[h] = alpha * acc_scr[h] + jnp.dot(vth, pt, preferred_element_type=F32)

        nxt = scores(0)
        pending = None
        for h in range(H_B):
            st, m_prev, m_new = nxt
            if h + 1 < H_B:
                nxt = scores(h + 1)
            alpha = jnp.exp2(m_prev - m_new)
            pt = jnp.exp2(st - m_new).astype(BF16)
            m_scr[h:h + 1, :] = m_new
            if pending is not None:
                values(*pending)
            pending = (h, pt, alpha)
        values(*pending)

    first = jj * kb_step
    n_below = jnp.clip(i - first, 0, kb_step)

    def below(sub, carry):
        sweep(False, sub)
        return carry

    lax.fori_loop(0, n_below, below, 0)

    @pl.when(i - first < kb_step)
    def _():
        sweep(True, i - first)
        lam = _lambda(lq_ref[...], lam_init)
        for h in range(H_B):
            hs = slice(h * DV_B, (h + 1) * DV_B)
            a1 = acc_scr[h, 0:DV_B, 0:blk] * (1.0 / acc_scr[h, DV_B:DV_B + 1, 0:blk])
            a2 = acc_scr[h, 0:DV_B, blk:2 * blk] * (1.0 / acc_scr[h, DV_B:DV_B + 1, blk:2 * blk])
            ot = a1 - lam * a2
            yt = ot * lax.rsqrt(jnp.mean(ot * ot, axis=0, keepdims=True) + 1e-6) * g_ref[...]
            o_ref[:, hs] = (yt.T * (1.0 - lam_init) * _silu(z_ref[:, hs])).astype(o_ref.dtype)


def _attn_prompt(q, kb, vt, z, lam_qk, attn_g, lam_init):
    t_len = q.shape[0]
    blk = _pick(t_len, (1024, 512, 256, 128))
    nb = t_len // blk
    kb_step = _pick(nb, (2, 1))
    pairs = [(i, jj) for i in range(nb) for jj in range(i // kb_step + 1)]
    i_tab = jnp.asarray([p[0] for p in pairs], jnp.int32)
    j_tab = jnp.asarray([p[1] for p in pairs], jnp.int32)
    q_map = lambda s, it, jt: (it[s], 0)
    k_map = lambda s, it, jt: (jt[s], 0)
    vt_map = lambda s, it, jt: (0, jt[s])
    fixed = lambda s, it, jt: (0, 0)
    return pl.pallas_call(
        functools.partial(_attn_body, blk=blk, kb_step=kb_step, lam_init=lam_init),
        grid_spec=pltpu.PrefetchScalarGridSpec(
            num_scalar_prefetch=2,
            grid=(len(pairs),),
            in_specs=[pl.BlockSpec((blk, D_B), q_map),
                      pl.BlockSpec((kb_step * blk, D_B), k_map),
                      pl.BlockSpec((D_B, kb_step * blk), vt_map),
                      pl.BlockSpec((blk, D_B), q_map),
                      pl.BlockSpec((4, DK_B), fixed),
                      pl.BlockSpec((DV_B, 1), fixed)],
            out_specs=pl.BlockSpec((blk, D_B), q_map),
            scratch_shapes=[pltpu.VMEM((H_B, 2 * blk), F32),
                            pltpu.VMEM((H_B, DV_B + SUM_ROWS, 2 * blk), F32)]),
        out_shape=jax.ShapeDtypeStruct((t_len, D_B), BF16),
        compiler_params=_cparams(1),
        name="attn_prompt",
    )(i_tab, j_tab, q, kb, vt, z, lam_qk, attn_g.reshape(DV_B, 1))


QROWS = 16


QR = H_B * QROWS


def _decode_body(pt_ref, q_ref, kn_ref, vn_ref, z_ref, lq_ref, g_ref, *rest,
                 n_pages_step, t_new, lam_init):
    k_refs = rest[:n_pages_step]
    v_refs = rest[n_pages_step:2 * n_pages_step]
    o_ref, m_scr, l_scr, acc_scr = rest[2 * n_pages_step:]
    g = pl.program_id(1)
    n_groups = pl.num_programs(1)

    @pl.when(g == 0)
    def _():
        m_scr[...] = jnp.full(m_scr.shape, NEG, F32)
        l_scr[...] = jnp.zeros(l_scr.shape, F32)
        acc_scr[...] = jnp.zeros(acc_scr.shape, F32)

    q64 = q_ref[0]
    row = lax.broadcasted_iota(jnp.int32, q64.shape, 0)
    lane = lax.broadcasted_iota(jnp.int32, q64.shape, 1)
    qd = jnp.where((lane < DK_B) == ((row % QROWS) < QROWS // 2), q64, jnp.zeros_like(q64))

    page = k_refs[0].shape[0] // H_B

    def head_rows(refs, h):
        return jnp.concatenate([r[pl.ds(h, page, stride=H_B), :] for r in refs], axis=0).astype(BF16)

    heads = range(H_B)
    rows = [slice(h * QROWS, (h + 1) * QROWS) for h in heads]
    s = [lax.dot_general(qd[rows[h], :], head_rows(k_refs, h), (((1,), (1,)), ((), ())),
                         preferred_element_type=F32) for h in heads]
    m_prev = [m_scr[rows[h], :] for h in heads]
    m_new = [jnp.maximum(m_prev[h], jnp.max(s[h], axis=-1, keepdims=True)) for h in heads]
    alpha = [jnp.exp(m_prev[h] - m_new[h]) for h in heads]
    p = [jnp.exp(s[h] - m_new[h]) for h in heads]
    pv = [jnp.dot(p[h].astype(BF16), head_rows(v_refs, h), preferred_element_type=F32) for h in heads]
    for h in heads:
        l_scr[rows[h], :] = alpha[h] * l_scr[rows[h], :] + jnp.sum(p[h], axis=-1, keepdims=True)
        acc_scr[rows[h], :] = alpha[h] * acc_scr[rows[h], :] + pv[h]
        m_scr[rows[h], :] = m_new[h]

    @pl.when(g == n_groups - 1)
    def _():
        lam = _lambda(lq_ref[...], lam_init)
        qf = qd.astype(F32)
        tok = lax.broadcasted_iota(jnp.int32, (QR, 1), 0) % (QROWS // 2)
        s_new = []
        for jn in range(t_new):
            sj = jnp.sum(qf * kn_ref[0, jn].astype(BF16).astype(F32), axis=-1, keepdims=True)
            s_new.append(jnp.where(tok >= jn, sj, NEG))
        m_prev = m_scr[...]
        m_fin = m_prev
        for sj in s_new:
            m_fin = jnp.maximum(m_fin, sj)
        alpha = jnp.exp(m_prev - m_fin)
        l_fin = alpha * l_scr[...]
        acc = alpha * acc_scr[...]
        for jn, sj in enumerate(s_new):
            pj = jnp.exp(sj - m_fin)
            l_fin = l_fin + pj
            acc = acc + pj.astype(BF16).astype(F32) * vn_ref[0, jn].astype(BF16).astype(F32)
        a = acc / l_fin
        half = QROWS // 2
        for h in range(H_B):
            hs = slice(h * DV_B, (h + 1) * DV_B)
            o = a[h * QROWS:h * QROWS + half, :] - lam * a[h * QROWS + half:(h + 1) * QROWS, :]
            o_ref[0, :, hs] = _attn_finish(o, g_ref[...], z_ref[0, :, hs], lam_init)


def _attn_decode(q, k_new, v_new, z, cache_k_l, cache_v_l, page_table, lam_qk, attn_g, lam_init,
                 layer):
    bsz, t_new, _ = q.shape
    n_pages = page_table.shape[1]
    page_rows = cache_k_l.shape[2]
    pps = _pick(n_pages, (32, 16, 8, 4, 2, 1))
    half = QROWS // 2
    qh = jnp.swapaxes(q.reshape(bsz, t_new, H_B, DV_B), 1, 2)
    q64 = jnp.zeros((bsz, H_B, QROWS, DV_B), BF16)
    q64 = q64.at[:, :, 0:t_new].set(qh).at[:, :, half:half + t_new].set(qh).reshape(bsz, QR, DV_B)
    on_rows = lambda x: jnp.repeat(x.reshape(bsz, t_new, H_B, DV_B), QROWS, axis=2)
    z8 = jnp.pad(z, ((0, 0), (0, half - t_new), (0, 0)))
    fixed = lambda b, g, pt: (0, 0)
    per_b = lambda b, g, pt: (b, 0, 0)
    per_b4 = lambda b, g, pt: (b, 0, 0, 0)

    def page_spec(jp):
        return pl.BlockSpec((None, None, page_rows, DV_B),
                            lambda b, g, pt: (layer, pt[b, g * pps + jp], 0, 0))

    in_specs = [pl.BlockSpec((1, QR, DV_B), per_b),
                pl.BlockSpec((1, t_new, QR, DV_B), per_b4),
                pl.BlockSpec((1, t_new, QR, DV_B), per_b4),
                pl.BlockSpec((1, half, D_B), per_b),
                pl.BlockSpec((4, DK_B), fixed),
                pl.BlockSpec((1, DV_B), fixed)]
    in_specs += [page_spec(jp) for jp in range(pps)] * 2
    out = pl.pallas_call(
        functools.partial(_decode_body, n_pages_step=pps, t_new=t_new, lam_init=lam_init),
        grid_spec=pltpu.PrefetchScalarGridSpec(
            num_scalar_prefetch=1,
            grid=(bsz, n_pages // pps),
            in_specs=in_specs,
            out_specs=pl.BlockSpec((1, half, D_B), per_b),
            scratch_shapes=[pltpu.VMEM((QR, 1), F32), pltpu.VMEM((QR, 1), F32),
                            pltpu.VMEM((QR, DV_B), F32)]),
        out_shape=jax.ShapeDtypeStruct((bsz, half, D_B), F32),
        compiler_params=_cparams(2),
        name="attn_decode",
    )(page_table, q64, on_rows(k_new), on_rows(v_new), z8, lam_qk, attn_g.reshape(1, DV_B),
      *([cache_k_l] * pps), *([cache_v_l] * pps))
    return out[:, :t_new, :]


MLSTM_CHUNK = 128
MLSTM_LONG_CHUNK = 256
PAIR = 2 * DK_C
ONES_ROWS = 16
ST_ROWS = DV_C + ONES_ROWS


def _mlstm_body(c_ref, gc_ref, gr_ref, fbc_ref, fbr_ref, mg_ref, s0_ref, m0_ref,
                y_ref, sn_ref, mn_ref, s_scr, m_scr, *pad_scr, ln, t_valid, t_pad):
    t = pl.program_id(1)
    n_t = pl.num_programs(1)

    @pl.when(t == 0)
    def _():
        s_scr[...] = s0_ref[0]
        m_scr[...] = m0_ref[0]

    rows_in = c_ref.shape[1]
    if rows_in == ln:
        c_blk, gcol, grow = c_ref.at[0], gc_ref[0], gr_ref[0]
    else:
        c_blk, gc_pad, gr_pad = pad_scr

        @pl.when(pl.program_id(0) == 0)
        def _():
            c_blk[...] = jnp.zeros(c_blk.shape, F32)
            gc_pad[...] = jnp.zeros(gc_pad.shape, F32)
            gr_pad[...] = jnp.zeros(gr_pad.shape, F32)

        c_blk[0:rows_in, :] = c_ref[0]
        gc_pad[0:rows_in, :] = gc_ref[0]
        gr_pad[:, 0:rows_in] = gr_ref[0]
        gcol, grow = gc_pad[...], gr_pad[...]
    is_f_c = lax.broadcasted_iota(jnp.int32, (ln, N_GATE), 1) >= H_C
    is_f_r = lax.broadcasted_iota(jnp.int32, (N_GATE, ln), 0) >= H_C
    g_c = jnp.where(is_f_c, _log_sigmoid(gcol + fbc_ref[...]), gcol)
    g_r = jnp.where(is_f_r, _log_sigmoid(grow + fbr_ref[...]), grow)
    if t_pad != t_valid:
        ok_c = (t * ln + lax.broadcasted_iota(jnp.int32, (ln, N_GATE), 0)) < t_valid
        ok_r = (t * ln + lax.broadcasted_iota(jnp.int32, (N_GATE, ln), 1)) < t_valid
        g_c = jnp.where(ok_c, g_c, jnp.where(is_f_c, 0.0, NEG))
        g_r = jnp.where(ok_r, g_r, jnp.where(is_f_r, 0.0, NEG))
    rr = lax.broadcasted_iota(jnp.int32, (ln, ln), 0)
    cc = lax.broadcasted_iota(jnp.int32, (ln, ln), 1)
    tri = (rr >= cc).astype(F32)
    src_before = rr <= cc
    cum_c = jnp.dot(tri, jnp.where(is_f_c, g_c, 0.0), preferred_element_type=F32,
                    precision=lax.Precision.HIGHEST)
    cum_r = lax.dot_general(jnp.where(is_f_r, g_r, 0.0), tri, (((1,), (1,)), ((), ())),
                            preferred_element_type=F32, precision=lax.Precision.HIGHEST)
    m_all = m_scr[...]
    lane1 = lax.broadcasted_iota(jnp.int32, (1, 128), 1)
    lane_k = lax.broadcasted_iota(jnp.int32, (ln, PAIR), 1)
    m_out = jnp.zeros((1, 128), F32)
    ones = jnp.ones((ONES_ROWS, ln), F32)

    for p in range(H_C // 2):
        def pair_block(seg):
            return c_blk[:, seg * D_C + p * PAIR:seg * D_C + (p + 1) * PAIR]

        q_t = pair_block(0).T.astype(BF16)
        k_p = pair_block(1) * KC_SCALE
        v_t = pair_block(2).T
        o_t = pair_block(3).T
        z_t = pair_block(4).T
        y_halves = []
        for half in range(2):
            h = 2 * p + half
            rs = slice(half * DK_C, (half + 1) * DK_C)
            k_own = jnp.where((lane_k < DK_C) == (half == 0), k_p, 0.0).astype(BF16)
            a_c = g_c[:, h:h + 1] - cum_c[:, H_C + h:H_C + h + 1]
            br = cum_r[H_C + h:H_C + h + 1, :]
            igr = g_r[h:h + 1, :]
            m_prev = m_all[:, h:h + 1]
            s_mat = s_scr[h]

            log_d = jnp.where(src_before, br + a_c, NEG)
            inter = br + m_prev
            m_t = jnp.maximum(inter, jnp.max(log_d, axis=0, keepdims=True))
            d_w = jnp.where(src_before, jnp.exp(log_d - m_t), 0.0)
            i_w = jnp.exp(inter - m_t)
            qk = jnp.dot(k_own, q_t, preferred_element_type=F32)
            w = (d_w * qk).astype(BF16)
            v1 = jnp.concatenate([v_t[rs, :], ones], axis=0)
            nd = (jnp.dot(v1.astype(BF16), w, preferred_element_type=F32)
                  + i_w * jnp.dot(s_mat.astype(BF16), q_t, preferred_element_type=F32))
            den = nd[DV_C:DV_C + 1, :]
            hh = nd[0:DV_C, :] / jnp.maximum(jnp.abs(den), jnp.exp(-m_t))

            b_last = cum_c[ln - 1:ln, H_C + h:H_C + h + 1]
            m_last = jnp.maximum(b_last + m_prev, b_last + jnp.max(a_c, axis=0, keepdims=True))
            i_last = jnp.exp(b_last + m_prev - m_last)
            w_last = jnp.exp(b_last - br + igr - m_last)
            s_scr[h] = i_last * s_mat + jnp.dot((v1 * w_last).astype(BF16), k_own,
                                                preferred_element_type=F32)
            m_out = jnp.where(lane1 == h, m_last, m_out)

            hg = _sigmoid(o_t[rs, :]) * hh
            yh = hg * lax.rsqrt(jnp.mean(hg * hg, axis=0, keepdims=True) + 1e-6)
            yh = yh * mg_ref[h * DV_C:(h + 1) * DV_C, :]
            y_halves.append(yh * _silu(z_t[rs, :]))
        y_pair = jnp.concatenate(y_halves, axis=0)
        y_ref[0, :, p * PAIR:(p + 1) * PAIR] = y_pair.T[0:rows_in, :].astype(y_ref.dtype)

    m_scr[...] = m_out

    @pl.when(t == n_t - 1)
    def _():
        sn_ref[0] = s_scr[...]
        mn_ref[0] = m_scr[...]


def _mlstm(c3d, gcol, grow, f_bias, mlstm_g, c0, n0, m0, ln):
    bsz, t_valid, _ = c3d.shape
    t_pad = -(-t_valid // ln) * ln
    pad_in_kernel = t_valid < ln
    rows_blk = t_valid if pad_in_kernel else ln
    pad_scratch = []
    if pad_in_kernel:
        pad_scratch = [pltpu.VMEM((ln, 5 * D_C), F32), pltpu.VMEM((ln, N_GATE), F32),
                       pltpu.VMEM((N_GATE, ln), F32)]
    elif t_pad != t_valid:
        extra = t_pad - t_valid
        c3d = jnp.pad(c3d, ((0, 0), (0, extra), (0, 0)))
        gcol = jnp.pad(gcol, ((0, 0), (0, extra), (0, 0)))
        grow = jnp.pad(grow, ((0, 0), (0, 0), (0, extra)))
    m0p = jnp.pad(m0, ((0, 0), (0, 128 - H_C))).reshape(bsz, 1, 128)
    own = jnp.concatenate([jnp.swapaxes(c0, 2, 3),
                           jnp.broadcast_to(n0[:, :, None, :], (bsz, H_C, ONES_ROWS, DK_C))], axis=2)
    even = jnp.pad(own[:, 0::2], ((0, 0), (0, 0), (0, 0), (0, DK_C)))
    odd = jnp.pad(own[:, 1::2], ((0, 0), (0, 0), (0, 0), (DK_C, 0)))
    s0 = jnp.stack([even, odd], axis=2).reshape(bsz, H_C, ST_ROWS, PAIR)
    fb8 = jnp.concatenate([jnp.zeros((H_C,), F32), f_bias])
    fixed = lambda b, t: (0, 0)
    per_b = lambda b, t: (b, 0, 0, 0)
    y, sn, mn = pl.pallas_call(
        functools.partial(_mlstm_body, ln=ln, t_valid=t_valid, t_pad=t_pad),
        grid=(bsz, t_pad // ln),
        in_specs=[pl.BlockSpec((1, rows_blk, 5 * D_C), lambda b, t: (b, t, 0)),
                  pl.BlockSpec((1, rows_blk, N_GATE), lambda b, t: (b, t, 0)),
                  pl.BlockSpec((1, N_GATE, rows_blk), lambda b, t: (b, 0, t)),
                  pl.BlockSpec((1, N_GATE), fixed),
                  pl.BlockSpec((N_GATE, 1), fixed),
                  pl.BlockSpec((D_C, 1), fixed),
                  pl.BlockSpec((1, H_C, ST_ROWS, PAIR), per_b),
                  pl.BlockSpec((1, 1, 128), lambda b, t: (b, 0, 0))],
        out_specs=[pl.BlockSpec((1, rows_blk, D_C), lambda b, t: (b, t, 0)),
                   pl.BlockSpec((1, H_C, ST_ROWS, PAIR), per_b),
                   pl.BlockSpec((1, 1, 128), lambda b, t: (b, 0, 0))],
        out_shape=[jax.ShapeDtypeStruct((bsz, c3d.shape[1], D_C), BF16),
                   jax.ShapeDtypeStruct((bsz, H_C, ST_ROWS, PAIR), F32),
                   jax.ShapeDtypeStruct((bsz, 1, 128), F32)],
        scratch_shapes=[pltpu.VMEM((H_C, ST_ROWS, PAIR), F32), pltpu.VMEM((1, 128), F32)]
        + pad_scratch,
        compiler_params=_cparams(2),
        name="mlstm",
    )(c3d, gcol, grow, fb8.reshape(1, N_GATE), fb8.reshape(N_GATE, 1), mlstm_g.reshape(D_C, 1),
      s0, m0p)
    odd_head = (jnp.arange(H_C) % 2 == 1)[None, :, None, None]
    own_n = jnp.where(odd_head, sn[..., DK_C:], sn[..., :DK_C])
    c_new = jnp.swapaxes(own_n[:, :, :DV_C, :], 2, 3)
    n_new = own_n[:, :, DV_C, :]
    return y[:, :t_valid], c_new, n_new, mn[:, 0, :H_C]


def _outproj_body(x_ref, ya_ref, az_ref, yb_ref, yc_ref, pw_ref, w_ref, g_ref, o_ref):
    ya = jnp.dot(ya_ref[...].astype(BF16), pw_ref[...], preferred_element_type=F32)
    ya = ya * _silu(az_ref[...])
    out = jnp.dot(ya.astype(BF16), w_ref[0:C_A, :], preferred_element_type=F32)
    out = out + jnp.dot(yb_ref[...].astype(BF16), w_ref[C_A:C_A + D_B, :], preferred_element_type=F32)
    out = out + jnp.dot(yc_ref[...].astype(BF16), w_ref[C_A + D_B:, :], preferred_element_type=F32)
    y = out * lax.rsqrt(jnp.mean(out * out, axis=-1, keepdims=True) + 1e-6) * g_ref[...]
    o_ref[...] = x_ref[...] + y


def _outproj(x2d, ya, a2d, yb, yc, conv_pw, w_out, ln_post):
    rows, d = x2d.shape
    tm = _pick(rows, (1024, 512, 256, 128))
    row = lambda i: (i, 0)
    fixed = lambda i: (0, 0)
    return pl.pallas_call(
        _outproj_body,
        grid=(rows // tm,),
        in_specs=[pl.BlockSpec((tm, d), row),
                  pl.BlockSpec((tm, C_A), row),
                  pl.BlockSpec((tm, C_A), lambda i: (i, 2)),
                  pl.BlockSpec((tm, D_B), row),
                  pl.BlockSpec((tm, D_C), row),
                  pl.BlockSpec((C_A, C_A), fixed),
                  pl.BlockSpec((C_A + D_B + D_C, d), fixed),
                  pl.BlockSpec((1, d), fixed)],
        out_specs=pl.BlockSpec((tm, d), row),
        out_shape=jax.ShapeDtypeStruct((rows, d), F32),
        compiler_params=_cparams(1),
        name="outproj",
    )(x2d, ya, a2d, yb, yc, conv_pw.astype(BF16), w_out.astype(BF16), ln_post.reshape(1, d))


def _layer(x, lidx, conv_state, mstate, attn, flash_operands, weights, mlstm_chunk):
    (ln_pre, w_in, b_in, conv_w, conv_b, conv_ln_g, conv_ln_b, conv_pw, lam_qk, attn_g, f_bias,
     mlstm_g, w_out, ln_post) = weights
    bsz, t_len, d = x.shape
    rows = bsz * t_len
    lam_init = 0.8 - 0.6 * math.exp(-0.3 * lidx)
    x2d = x.reshape(rows, d)
    a, q, k, v, z, c, gcol, grow, *flash = _inproj(x2d, ln_pre, w_in, b_in, flash_operands)

    ya, conv_new = _conv(a.reshape(bsz, t_len, 3 * C_A), conv_state, conv_w, conv_b, conv_ln_g,
                         conv_ln_b)
    yb = attn(q, k, v, z, lam_qk, attn_g, lam_init, *flash)
    grow3 = jnp.swapaxes(grow.reshape(N_GATE, bsz, t_len), 0, 1)
    yc, c_new, n_new, m_new = _mlstm(c.reshape(bsz, t_len, 5 * D_C), gcol.reshape(bsz, t_len, N_GATE),
                                     grow3, f_bias, mlstm_g, *mstate, mlstm_chunk)
    x_new = _outproj(x2d, ya.reshape(rows, C_A), a, yb.reshape(rows, D_B), yc.reshape(rows, D_C),
                     conv_pw, w_out, ln_post)
    return (x_new.reshape(bsz, t_len, d), k.reshape(bsz, t_len, H_B, 2 * DK_B),
            v.reshape(bsz, t_len, H_B, DV_B), conv_new, (c_new, n_new, m_new))


def kernel(x_prompt, x_sample, cache_k, cache_v, state_conv, state_C, state_n, state_m, page_table,
           ln_pre, w_in, b_in, conv_w, conv_b, conv_ln_g, conv_ln_b, conv_pw, lam_qk, attn_g, f_bias,
           mlstm_g, w_out, ln_post):
    depth = w_in.shape[0]
    bp, tp, _ = x_prompt.shape
    bs, ts, _ = x_sample.shape
    assert bp == 1, "prompt attention sweep is written for a single prompt sequence"
    n_pool, page = cache_k.shape[1], cache_k.shape[2]
    ck = cache_k.reshape(depth, n_pool, page * H_B, DV_B)
    cv = cache_v.reshape(depth, n_pool, page * H_B, DV_B)

    def weights(l):
        return (ln_pre[l], w_in[l], b_in[l], conv_w[l], conv_b[l], conv_ln_g[l], conv_ln_b[l],
                conv_pw[l], lam_qk[l], attn_g[l], f_bias[l], mlstm_g[l], w_out[l], ln_post[l])

    def prompt_attn(q, k, v, z, lq, g, lam_init, kb, vt):
        return _attn_prompt(q, kb, vt, z, lq, g, lam_init)

    xp = x_prompt
    outs_p = [[] for _ in range(6)]
    for l in range(depth):
        buf0 = jnp.zeros((bp, CONV_W - 1, C_A), F32)
        m0 = (jnp.zeros((bp, H_C, DK_C, DV_C), F32), jnp.zeros((bp, H_C, DK_C), F32),
              jnp.zeros((bp, H_C), F32))
        xp, kr, vr, cb, (cm, nv, mv) = _layer(xp, l, buf0, m0, prompt_attn, True, weights(l),
                                              _pick(tp, (MLSTM_LONG_CHUNK, MLSTM_CHUNK)))
        for lst, val in zip(outs_p, (kr, vr, cb, cm, nv, mv)):
            lst.append(val)

    xs = x_sample
    outs_s = [[] for _ in range(6)]
    for l in range(depth):
        def sample_attn(q, k, v, z, lq, g, lam_init, l=l):
            return _attn_decode(q.reshape(bs, ts, D_B), k.reshape(bs, ts, D_B), v.reshape(bs, ts, D_B),
                                z.reshape(bs, ts, D_B), ck, cv, page_table, lq, g, lam_init, l)

        st = (state_C[l], state_n[l], state_m[l])
        xs, kr, vr, cb, (cm, nv, mv) = _layer(xs, l, state_conv[l], st, sample_attn, False,
                                              weights(l), MLSTM_CHUNK)
        for lst, val in zip(outs_s, (kr, vr, cb, cm, nv, mv)):
            lst.append(val)

    return (xp, xs, *[jnp.stack(o) for o in outs_p], *[jnp.stack(o) for o in outs_s])
```
